```python
import jax, jax.numpy as jnp
from jax import lax
import numpy as np

D_MODEL = 1024
BATCH = 8
SEQ = 2048
DEPTH = 1

HEAD_DIM = 64
NA_HEADS = 8
SWA_HEADS = 8
SWA_KV_HEADS = 2
A_WIDTH = NA_HEADS * HEAD_DIM
B_WIDTH = SWA_HEADS * HEAD_DIM
B_KV_WIDTH = SWA_KV_HEADS * HEAD_DIM
MIX_WIDTH = A_WIDTH + B_WIDTH
IN_WIDTH = 3 * A_WIDTH + B_WIDTH + 2 * B_KV_WIDTH
GRID_W = 64
NA_ROWS_MAX = 8
NA_COLS = 16
NA_QROWS = 2
NA_QCOLS = 16
NA_HCOLS = NA_COLS + NA_QCOLS
SWA_WINDOW = 128
SWA_BLOCK = 128
ROPE_THETA = 500000.0
ROPE_DIM = HEAD_DIM // 4
D_FF = -(-8 * D_MODEL // (3 * 256)) * 256
RMS_EPS = 1e-6
NEG_INF = -1e30

kernel_name = "hybrid_natten_swa_sink_encoder_block"


def rmsnorm(x, g):
    xf = x.astype(jnp.float32)
    y = xf * lax.rsqrt(jnp.mean(xf * xf, axis=-1, keepdims=True) + RMS_EPS)
    return (y * g.astype(jnp.float32)).astype(x.dtype)


def partial_rope(x, pos):
    inv_freq = ROPE_THETA ** (-jnp.arange(0, ROPE_DIM, 2, dtype=jnp.float32) / ROPE_DIM)
    ang = pos.astype(jnp.float32)[:, None] * inv_freq[None, :]
    cos = jnp.cos(ang)[None, :, None, :]
    sin = jnp.sin(ang)[None, :, None, :]
    xr = x[..., :ROPE_DIM].astype(jnp.float32)
    x1, x2 = xr[..., : ROPE_DIM // 2], xr[..., ROPE_DIM // 2:]
    rot = jnp.concatenate([x1 * cos - x2 * sin, x2 * cos + x1 * sin], axis=-1)
    return jnp.concatenate([rot.astype(x.dtype), x[..., ROPE_DIM:]], axis=-1)


def _axis_windows(n, k, q_blk, halo):
    nb = n // q_blk
    q_pos = np.arange(n).reshape(nb, q_blk)
    start = np.clip(q_pos - k // 2, 0, n - k)
    h0 = np.clip(q_pos[:, 0] - k // 2, 0, n - halo)
    k_pos = h0[:, None] + np.arange(halo)[None, :]
    ok = (k_pos[:, None, :] >= start[:, :, None]) & (k_pos[:, None, :] < start[:, :, None] + k)
    off = k_pos[:, None, :] - q_pos[:, :, None]
    return q_pos, k_pos, ok, off


def neighbourhood_attention(q, k, v, rpb):
    b, s, h, d = q.shape
    rows = s // GRID_W
    kr = min(NA_ROWS_MAX, rows)
    hr = min(kr + NA_QROWS, rows)
    _, r_keys, r_ok, r_off = _axis_windows(rows, kr, NA_QROWS, hr)
    _, c_keys, c_ok, c_off = _axis_windows(GRID_W, NA_COLS, NA_QCOLS, NA_HCOLS)
    rb, cb = rows // NA_QROWS, GRID_W // NA_QCOLS
    nq, nk = NA_QROWS * NA_QCOLS, hr * NA_HCOLS

    key_idx = (r_keys[:, None, :, None] * GRID_W + c_keys[None, :, None, :]).astype(np.int32)
    key_idx = key_idx.reshape(-1)
    kb = jnp.take(k, key_idx, axis=1).reshape(b, rb, cb, nk, h, d)
    vb = jnp.take(v, key_idx, axis=1).reshape(b, rb, cb, nk, h, d)
    qb = q.reshape(b, rb, NA_QROWS, cb, NA_QCOLS, h, d).transpose(0, 1, 3, 2, 4, 5, 6)
    qb = qb.reshape(b, rb, cb, nq, h, d)

    mask = (r_ok[:, None, :, None, :, None] & c_ok[None, :, None, :, None, :]).reshape(rb, cb, nq, nk)
    r_idx = np.clip(r_off + NA_ROWS_MAX - 1, 0, 2 * NA_ROWS_MAX - 2).astype(np.int32)
    c_idx = np.clip(c_off + NA_COLS - 1, 0, 2 * NA_COLS - 2).astype(np.int32)
    bias = rpb[:, r_idx[:, None, :, None, :, None], c_idx[None, :, None, :, None, :]]
    bias = bias.reshape(h, rb, cb, nq, nk).transpose(1, 2, 0, 3, 4).astype(jnp.float32)

    scale = HEAD_DIM ** -0.5
    logits = jnp.einsum('bnmqhd,bnmkhd->bnmhqk', qb, kb).astype(jnp.float32) * scale + bias
    logits = jnp.where(mask[:, :, None], logits, NEG_INF)
    p = jax.nn.softmax(logits, axis=-1).astype(vb.dtype)
    o = jnp.einsum('bnmhqk,bnmkhd->bnmqhd', p, vb)
    o = o.reshape(b, rb, cb, NA_QROWS, NA_QCOLS, h, d).transpose(0, 1, 3, 2, 4, 5, 6)
    return o.reshape(b, s, h, d)


def windowed_gqa_sink(q, k, v, sink):
    b, s, hq, d = q.shape
    hkv = k.shape[2]
    g = hq // hkv
    nb = s // SWA_BLOCK
    qb = q.reshape(b, nb, SWA_BLOCK, hkv, g, d)

    def band(t):
        tp = jnp.pad(t, ((0, 0), (SWA_BLOCK, SWA_BLOCK), (0, 0), (0, 0)))
        tp = tp.reshape(b, nb + 2, SWA_BLOCK, hkv, d)
        return jnp.concatenate([tp[:, :-2], tp[:, 1:-1], tp[:, 2:]], axis=2)

    kb, vb = band(k), band(v)
    q_pos = np.arange(nb)[:, None] * SWA_BLOCK + np.arange(SWA_BLOCK)[None, :]
    k_pos = (np.arange(nb)[:, None] - 1) * SWA_BLOCK + np.arange(3 * SWA_BLOCK)[None, :]
    mask = ((np.abs(q_pos[:, :, None] - k_pos[:, None, :]) <= SWA_WINDOW)
            & (k_pos[:, None, :] >= 0) & (k_pos[:, None, :] < s))

    scale = HEAD_DIM ** -0.5
    logits = jnp.einsum('bnqkgd,bnjkd->bnkgqj', qb, kb).astype(jnp.float32) * scale
    logits = jnp.where(mask[None, :, None, None], logits, NEG_INF)
    sk = sink.astype(jnp.float32).reshape(hkv, g)[None, None, :, :, None, None]
    m = jnp.maximum(jnp.max(logits, axis=-1, keepdims=True), sk)
    p = jnp.exp(logits - m)
    denom = jnp.sum(p, axis=-1, keepdims=True) + jnp.exp(sk - m)
    p = (p / denom).astype(vb.dtype)
    o = jnp.einsum('bnkgqj,bnjkd->bnqkgd', p, vb)
    return o.reshape(b, s, hq, d)


def setup_inputs(seed: int = 0) -> dict:
    key = jax.random.key(seed)
    ks = jax.random.split(key, 14)
    f32 = jnp.float32

    def nrm(k, shape, scale):
        return jax.random.normal(k, shape, f32) * scale

    def gain(k, shape):
        return 1.0 + 0.05 * jax.random.normal(k, shape, f32)

    return {
        "x": jax.random.normal(ks[0], (BATCH, SEQ, D_MODEL), f32),
        "g_norm_mix": gain(ks[1], (DEPTH, D_MODEL)),
        "w_in": nrm(ks[2], (DEPTH, D_MODEL, IN_WIDTH), D_MODEL ** -0.5),
        "na_rpb": nrm(ks[3], (DEPTH, NA_HEADS, 2 * NA_ROWS_MAX - 1, 2 * NA_COLS - 1), 0.2),
        "swa_sink": nrm(ks[4], (DEPTH, SWA_HEADS), 0.5),
        "g_out_na": gain(ks[5], (DEPTH, A_WIDTH)),
        "g_out_swa": gain(ks[6], (DEPTH, B_WIDTH)),
        "w_out": nrm(ks[7], (DEPTH, MIX_WIDTH, D_MODEL), MIX_WIDTH ** -0.5),
        "g_norm_ffn": gain(ks[8], (DEPTH, D_MODEL)),
        "w_gate": nrm(ks[9], (DEPTH, D_MODEL, D_FF), D_MODEL ** -0.5),
        "w_up": nrm(ks[10], (DEPTH, D_MODEL, D_FF), D_MODEL ** -0.5),
        "w_down": nrm(ks[11], (DEPTH, D_FF, D_MODEL), D_FF ** -0.5),
        "g_final": gain(ks[12], (D_MODEL,)),
    }


def reference(x, g_norm_mix, w_in, na_rpb, swa_sink, g_out_na, g_out_swa, w_out,
              g_norm_ffn, w_gate, w_up, w_down, g_final):
    b, s, _ = x.shape
    pos = jnp.arange(s, dtype=jnp.int32)
    split_at = [A_WIDTH, 2 * A_WIDTH, 3 * A_WIDTH, 3 * A_WIDTH + B_WIDTH,
                3 * A_WIDTH + B_WIDTH + B_KV_WIDTH]
    for l in range(DEPTH):
        h = rmsnorm(x, g_norm_mix[l])
        proj = h @ w_in[l]
        qa, ka, va, qb, kb, vb = jnp.split(proj, split_at, axis=-1)
        qa = qa.reshape(b, s, NA_HEADS, HEAD_DIM)
        ka = ka.reshape(b, s, NA_HEADS, HEAD_DIM)
        va = va.reshape(b, s, NA_HEADS, HEAD_DIM)
        qb = partial_rope(qb.reshape(b, s, SWA_HEADS, HEAD_DIM), pos)
        kb = partial_rope(kb.reshape(b, s, SWA_KV_HEADS, HEAD_DIM), pos)
        vb = vb.reshape(b, s, SWA_KV_HEADS, HEAD_DIM)

        oa = neighbourhood_attention(qa, ka, va, na_rpb[l]).reshape(b, s, A_WIDTH)
        ob = windowed_gqa_sink(qb, kb, vb, swa_sink[l]).reshape(b, s, B_WIDTH)
        o = jnp.concatenate([rmsnorm(oa, g_out_na[l]), rmsnorm(ob, g_out_swa[l])], axis=-1)
        x = x + o @ w_out[l]

        h = rmsnorm(x, g_norm_ffn[l])
        x = x + (jax.nn.silu(h @ w_gate[l]) * (h @ w_up[l])) @ w_down[l]
    return rmsnorm(x, g_final)
```

```python
import functools

import numpy as np
import jax
import jax.numpy as jnp
from jax import lax
from jax.experimental import pallas as pl
from jax.experimental.pallas import tpu as pltpu

D_MODEL = 1024
HEAD_DIM = 64
NA_HEADS = 8
SWA_HEADS = 8
SWA_KV_HEADS = 2
A_WIDTH = NA_HEADS * HEAD_DIM
B_WIDTH = SWA_HEADS * HEAD_DIM
B_KV_WIDTH = SWA_KV_HEADS * HEAD_DIM
IN_WIDTH = 3 * A_WIDTH + B_WIDTH + 2 * B_KV_WIDTH
GRID_W = 64
NA_ROWS = 8
NA_COLS = 16
SWA_WINDOW = 128
ROPE_THETA = 500000.0
ROPE_DIM = HEAD_DIM // 4
RMS_EPS = 1e-6
NEG_INF = -1e30
SCALE = HEAD_DIM ** -0.5

LANES = 128
VMEM_CAPACITY = 64 * 1024 * 1024

PROJ_WIDTH = IN_WIDTH + 2 * B_KV_WIDTH
COL_QA, COL_KA, COL_VA = 0, A_WIDTH, 2 * A_WIDTH
COL_QB = 3 * A_WIDTH
COL_KB = COL_QB + B_WIDTH
COL_VB = COL_KB + B_KV_WIDTH
COL_KB_SW = COL_VB + B_KV_WIDTH
COL_VB_SW = COL_KB_SW + B_KV_WIDTH

Q_BLK = 128
NA_Q_ROWS = Q_BLK // GRID_W
NA_HALO_ROWS = NA_ROWS + NA_Q_ROWS
NA_KEYS = NA_HALO_ROWS * GRID_W
SWA_KEYS = 3 * Q_BLK
NA_CLASSES = 5


def _vmem_limit(nbytes):
    return int(min(VMEM_CAPACITY - (2 << 20), max(nbytes, 16 << 20)))


def _na_class_of_block(i, nblk):
    if i < 2:
        return i
    if i >= nblk - 2:
        return NA_CLASSES - (nblk - i)
    return 2


def _na_key_row0(i, rows):
    return int(np.clip(NA_Q_ROWS * i - NA_ROWS // 2, 0, rows - NA_HALO_ROWS))


def _na_static_tables(seq):
    rows = seq // GRID_W
    nblk = seq // Q_BLK
    masks = np.zeros((NA_CLASSES, Q_BLK, NA_KEYS), np.float32)
    ridx = np.full((NA_CLASSES, NA_Q_ROWS, NA_HALO_ROWS), -1, np.int64)
    seen = {}
    for i in range(nblk):
        cls = _na_class_of_block(i, nblk)
        r0 = _na_key_row0(i, rows)
        q = np.arange(Q_BLK) + i * Q_BLK
        qr, qc = q // GRID_W, q % GRID_W
        k = np.arange(NA_KEYS) + r0 * GRID_W
        kr, kc = k // GRID_W, k % GRID_W
        rs = np.clip(qr - NA_ROWS // 2, 0, rows - NA_ROWS)
        cs = np.clip(qc - NA_COLS // 2, 0, GRID_W - NA_COLS)
        ok_r = (kr[None, :] >= rs[:, None]) & (kr[None, :] < rs[:, None] + NA_ROWS)
        ok_c = (kc[None, :] >= cs[:, None]) & (kc[None, :] < cs[:, None] + NA_COLS)
        m = np.where(ok_r & ok_c, 0.0, NEG_INF).astype(np.float32)
        rr = np.full((NA_Q_ROWS, NA_HALO_ROWS), -1, np.int64)
        for a in range(NA_Q_ROWS):
            for b in range(NA_HALO_ROWS):
                r, krow = NA_Q_ROWS * i + a, r0 + b
                if rs[a * GRID_W] <= krow < rs[a * GRID_W] + NA_ROWS:
                    rr[a, b] = krow - r + NA_ROWS - 1
        if cls in seen:
            assert np.array_equal(seen[cls][0], m) and np.array_equal(seen[cls][1], rr)
        seen[cls] = (m, rr)
        masks[cls], ridx[cls] = m, rr
    return masks, ridx


def _swa_static_masks(seq):
    nblk = seq // Q_BLK
    masks = np.zeros((3, Q_BLK, SWA_KEYS), np.float32)
    for cls, n in ((0, 0), (1, 1), (2, nblk - 1)):
        ks = int(np.clip(Q_BLK * (n - 1), 0, seq - SWA_KEYS))
        q = np.arange(Q_BLK) + n * Q_BLK
        k = np.arange(SWA_KEYS) + ks
        masks[cls] = np.where(np.abs(q[:, None] - k[None, :]) <= SWA_WINDOW, 0.0, NEG_INF)
    return masks


def _rope_tables(seq):
    half = ROPE_DIM // 2
    inv_freq = ROPE_THETA ** (-jnp.arange(0, ROPE_DIM, 2, dtype=jnp.float32) / ROPE_DIM)
    ang = jnp.arange(seq, dtype=jnp.int32).astype(jnp.float32)[:, None] * inv_freq[None, :]
    cos, sin = jnp.cos(ang), jnp.sin(ang)
    ones = jnp.ones((seq, HEAD_DIM - ROPE_DIM), jnp.float32)
    zeros = jnp.zeros((seq, HEAD_DIM - ROPE_DIM), jnp.float32)
    zh = jnp.zeros((seq, half), jnp.float32)
    c = jnp.concatenate([cos, cos, ones], axis=1)
    sa = jnp.concatenate([-sin, zh, zeros], axis=1)
    sb = jnp.concatenate([zh, sin, zeros], axis=1)
    rep = LANES // HEAD_DIM
    return jnp.tile(c, (1, rep)), jnp.tile(sa, (1, rep)), jnp.tile(sb, (1, rep))


def _rms(x, g):
    return x * lax.rsqrt(jnp.mean(x * x, axis=-1, keepdims=True) + RMS_EPS) * g


def _rope_tile(a, c, sa, sb):
    return a * c + pltpu.roll(a, LANES - ROPE_DIM // 2, axis=1) * sa + pltpu.roll(a, ROPE_DIM // 2, axis=1) * sb


def _in_proj_kernel(x_ref, g_ref, w_ref, c_ref, sa_ref, sb_ref, o_ref):
    h = _rms(x_ref[...], g_ref[...]).astype(jnp.bfloat16)
    c, sa, sb = c_ref[...], sa_ref[...], sb_ref[...]

    def proj(col, width):
        return jnp.dot(h, w_ref[:, col:col + width], preferred_element_type=jnp.float32)

    o_ref[:, COL_QA:COL_QA + A_WIDTH] = (proj(COL_QA, A_WIDTH) * SCALE).astype(o_ref.dtype)
    o_ref[:, COL_KA:COL_KA + A_WIDTH] = proj(COL_KA, A_WIDTH).astype(o_ref.dtype)
    o_ref[:, COL_VA:COL_VA + A_WIDTH] = proj(COL_VA, A_WIDTH).astype(o_ref.dtype)
    qb = proj(COL_QB, B_WIDTH)
    for t in range(B_WIDTH // LANES):
        lo = t * LANES
        tile = _rope_tile(qb[:, lo:lo + LANES], c, sa, sb) * SCALE
        o_ref[:, COL_QB + lo:COL_QB + lo + LANES] = tile.astype(o_ref.dtype)
    kv = proj(COL_KB, 2 * B_KV_WIDTH)
    kb = _rope_tile(kv[:, :B_KV_WIDTH], c, sa, sb)
    vb = kv[:, B_KV_WIDTH:]
    o_ref[:, COL_KB:COL_KB + B_KV_WIDTH] = kb.astype(o_ref.dtype)
    o_ref[:, COL_VB:COL_VB + B_KV_WIDTH] = vb.astype(o_ref.dtype)
    o_ref[:, COL_KB_SW:COL_KB_SW + B_KV_WIDTH] = pltpu.roll(kb, HEAD_DIM, axis=1).astype(o_ref.dtype)
    o_ref[:, COL_VB_SW:COL_VB_SW + B_KV_WIDTH] = pltpu.roll(vb, HEAD_DIM, axis=1).astype(o_ref.dtype)


def _in_proj(x2d, g, w_bf16, rope, seq, tm):
    n_tok = x2d.shape[0]
    pos_blocks = seq // tm
    const = lambda i: (0, 0)
    pos = lambda i: (i % pos_blocks, 0)
    vmem = 2 * tm * D_MODEL * 4 + 2 * D_MODEL * IN_WIDTH * 2 + 2 * tm * PROJ_WIDTH * 2 + 3 * tm * B_WIDTH * 4 * 2
    return pl.pallas_call(
        _in_proj_kernel,
        out_shape=jax.ShapeDtypeStruct((n_tok, PROJ_WIDTH), jnp.bfloat16),
        grid=(n_tok // tm,),
        in_specs=[
            pl.BlockSpec((tm, D_MODEL), lambda i: (i, 0)),
            pl.BlockSpec((1, D_MODEL), const),
            pl.BlockSpec((D_MODEL, IN_WIDTH), const),
            pl.BlockSpec((tm, LANES), pos),
            pl.BlockSpec((tm, LANES), pos),
            pl.BlockSpec((tm, LANES), pos),
        ],
        out_specs=pl.BlockSpec((tm, PROJ_WIDTH), lambda i: (i, 0)),
        compiler_params=pltpu.CompilerParams(
            dimension_semantics=("arbitrary",), vmem_limit_bytes=_vmem_limit(vmem + (8 << 20))),
        name="in_proj",
    )(x2d, g, w_bf16, *rope)


def _build_na_bias(rpb_ref, mask_ref, t_scr, hp, ridx):
    n_rrow, n_rcol = 2 * NA_ROWS - 1, 2 * NA_COLS - 1
    lane = lax.broadcasted_iota(jnp.int32, (GRID_W, LANES), 1)
    qc = lax.broadcasted_iota(jnp.int32, (GRID_W, LANES), 0)
    diff = (lane % GRID_W) - qc + (NA_COLS - 1)
    right = lane >= GRID_W
    used_rows = sorted({int(r) for r in ridx.reshape(-1) if r >= 0})
    for hh in range(2):
        h = 2 * hp + hh
        toeplitz = {}
        for rho in used_rows:
            acc = jnp.zeros((GRID_W, LANES), jnp.float32)
            for d in range(n_rcol):
                acc = jnp.where(diff == d, rpb_ref[(h * n_rrow + rho) * n_rcol + d], acc)
            toeplitz[rho] = acc
        for cls in range(NA_CLASSES):
            for a in range(NA_Q_ROWS):
                for j in range(NA_HALO_ROWS // 2):
                    r1, r2 = int(ridx[cls, a, 2 * j]), int(ridx[cls, a, 2 * j + 1])
                    zero = jnp.zeros((GRID_W, LANES), jnp.float32)
                    left = toeplitz[r1] if r1 >= 0 else zero
                    rght = toeplitz[r2] if r2 >= 0 else zero
                    blk = jnp.where(right, rght, left)
                    rs, cs = a * GRID_W, j * LANES
                    t_scr[cls, hh, rs:rs + GRID_W, cs:cs + LANES] = blk + mask_ref[cls, rs:rs + GRID_W, cs:cs + LANES]


def _na_kernel(rpb_ref, q_ref, k_ref, v_ref, mask_ref, o_ref, t_scr, *, ridx, seq):
    hp = pl.program_id(0)
    nblk = seq // Q_BLK
    rows = seq // GRID_W

    @pl.when(pl.program_id(1) == 0)
    def _():
        _build_na_bias(rpb_ref, mask_ref, t_scr, hp, ridx)

    lane = lax.broadcasted_iota(jnp.int32, (Q_BLK, LANES), 1)
    left = lane < HEAD_DIM

    def body(i, carry):
        q0 = pl.multiple_of(i * Q_BLK, Q_BLK)
        r0 = jnp.clip(NA_Q_ROWS * i - NA_ROWS // 2, 0, rows - NA_HALO_ROWS)
        k0 = pl.multiple_of(r0 * GRID_W, GRID_W)
        cls = jnp.where(i < 2, i, jnp.where(i >= nblk - 2, NA_CLASSES - (nblk - i), 2))
        q2 = q_ref[pl.ds(q0, Q_BLK), :]
        k2 = k_ref[pl.ds(k0, NA_KEYS), :]
        v2 = v_ref[pl.ds(k0, NA_KEYS), :]
        zero = jnp.zeros_like(q2)
        outs = []
        for hh, qm in enumerate((jnp.where(left, q2, zero), jnp.where(left, zero, q2))):
            s = lax.dot_general(qm, k2, (((1,), (1,)), ((), ())), preferred_element_type=jnp.float32)
            s = s + t_scr[cls, hh]
            m = jnp.max(s, axis=-1, keepdims=True)
            p = jnp.exp(s - m)
            den = jnp.sum(p, axis=-1, keepdims=True)
            pv = jnp.dot(p.astype(jnp.bfloat16), v2, preferred_element_type=jnp.float32)
            outs.append(pv / den)
        o_ref[pl.ds(q0, Q_BLK), :] = jnp.where(left, outs[0], outs[1]).astype(o_ref.dtype)
        return carry

    lax.fori_loop(0, nblk, body, 0)


def _na_attn(proj, rpb_flat, seq, batch):
    masks, ridx = _na_static_tables(seq)
    n_tok = proj.shape[0]
    n_pairs = A_WIDTH // LANES
    col = lambda base: (lambda hp, b: (b, base // LANES + hp))
    vmem = 2 * 4 * seq * LANES * 2 + 2 * masks.size * 4 + NA_CLASSES * 2 * Q_BLK * NA_KEYS * 4
    return pl.pallas_call(
        functools.partial(_na_kernel, ridx=ridx, seq=seq),
        out_shape=jax.ShapeDtypeStruct((n_tok, A_WIDTH), jnp.bfloat16),
        grid=(n_pairs, batch),
        in_specs=[
            pl.BlockSpec(memory_space=pltpu.SMEM),
            pl.BlockSpec((seq, LANES), col(COL_QA)),
            pl.BlockSpec((seq, LANES), col(COL_KA)),
            pl.BlockSpec((seq, LANES), col(COL_VA)),
            pl.BlockSpec(masks.shape, lambda hp, b: (0, 0, 0)),
        ],
        out_specs=pl.BlockSpec((seq, LANES), lambda hp, b: (b, hp)),
        scratch_shapes=[pltpu.VMEM((NA_CLASSES, 2, Q_BLK, NA_KEYS), jnp.float32)],
        compiler_params=pltpu.CompilerParams(
            dimension_semantics=("arbitrary", "arbitrary"), vmem_limit_bytes=_vmem_limit(vmem + (16 << 20))),
        name="na_attn",
    )(rpb_flat, proj, proj, proj, jnp.asarray(masks))


def _swa_kernel(sink_ref, q_ref, ka_ref, kb_ref, va_ref, vb_ref, mask_ref, o_ref, *, seq):
    t = pl.program_id(1)
    nblk = seq // Q_BLK
    lane = lax.broadcasted_iota(jnp.int32, (Q_BLK, LANES), 1)
    left = lane < HEAD_DIM
    sinks = (sink_ref[2 * t], sink_ref[2 * t + 1])

    def body(n, carry):
        q0 = pl.multiple_of(n * Q_BLK, Q_BLK)
        k0 = pl.multiple_of(jnp.clip(Q_BLK * (n - 1), 0, seq - SWA_KEYS), Q_BLK)
        cls = jnp.where(n == 0, 0, jnp.where(n == nblk - 1, 2, 1))
        q2 = q_ref[pl.ds(q0, Q_BLK), :]
        zero = jnp.zeros_like(q2)
        mask = mask_ref[cls]
        outs = []
        for hh, (qm, k_ref, v_ref) in enumerate(((jnp.where(left, q2, zero), ka_ref, va_ref),
                                                 (jnp.where(left, zero, q2), kb_ref, vb_ref))):
            k2 = k_ref[pl.ds(k0, SWA_KEYS), :]
            v2 = v_ref[pl.ds(k0, SWA_KEYS), :]
            s = lax.dot_general(qm, k2, (((1,), (1,)), ((), ())), preferred_element_type=jnp.float32)
            s = s + mask
            m = jnp.maximum(jnp.max(s, axis=-1, keepdims=True), sinks[hh])
            p = jnp.exp(s - m)
            den = jnp.sum(p, axis=-1, keepdims=True) + jnp.exp(sinks[hh] - m)
            pv = jnp.dot(p.astype(jnp.bfloat16), v2, preferred_element_type=jnp.float32)
            outs.append(pv / den)
        o_ref[pl.ds(q0, Q_BLK), :] = jnp.where(left, outs[0], outs[1]).astype(o_ref.dtype)
        return carry

    lax.fori_loop(0, nblk, body, 0)


def _swa_attn(proj, sink, seq, batch):
    masks = _swa_static_masks(seq)
    n_tok = proj.shape[0]
    n_tiles = B_WIDTH // LANES
    tiles_per_kv = n_tiles // SWA_KV_HEADS
    kb, vb, kbs, vbs = (c // LANES for c in (COL_KB, COL_VB, COL_KB_SW, COL_VB_SW))
    left_kv = lambda plain, swapped: (lambda b, t: (b, plain + (swapped - plain) * (t // tiles_per_kv)))
    right_kv = lambda plain, swapped: (lambda b, t: (b, swapped + (plain - swapped) * (t // tiles_per_kv)))
    vmem = 2 * 6 * seq * LANES * 2 + 2 * masks.size * 4
    return pl.pallas_call(
        functools.partial(_swa_kernel, seq=seq),
        out_shape=jax.ShapeDtypeStruct((n_tok, B_WIDTH), jnp.bfloat16),
        grid=(batch, n_tiles),
        in_specs=[
            pl.BlockSpec(memory_space=pltpu.SMEM),
            pl.BlockSpec((seq, LANES), lambda b, t: (b, COL_QB // LANES + t)),
            pl.BlockSpec((seq, LANES), left_kv(kb, kbs)),
            pl.BlockSpec((seq, LANES), right_kv(kb, kbs)),
            pl.BlockSpec((seq, LANES), left_kv(vb, vbs)),
            pl.BlockSpec((seq, LANES), right_kv(vb, vbs)),
            pl.BlockSpec(masks.shape, lambda b, t: (0, 0, 0)),
        ],
        out_specs=pl.BlockSpec((seq, LANES), lambda b, t: (b, t)),
        compiler_params=pltpu.CompilerParams(
            dimension_semantics=("arbitrary", "arbitrary"), vmem_limit_bytes=_vmem_limit(vmem + (16 << 20))),
        name="swa_attn",
    )(sink, proj, proj, proj, proj, proj, jnp.asarray(masks))


def _out_ffn_kernel(x_ref, oa_ref, ob_ref, gna_ref, gswa_ref, wo_ref, gffn_ref, wg_ref, wu_ref, wd_ref,
                    gfin_ref, o_ref, *, ff_chunk):
    na = _rms(oa_ref[...].astype(jnp.float32), gna_ref[...]).astype(jnp.bfloat16)
    nb = _rms(ob_ref[...].astype(jnp.float32), gswa_ref[...]).astype(jnp.bfloat16)
    x1 = (x_ref[...]
          + jnp.dot(na, wo_ref[:A_WIDTH, :], preferred_element_type=jnp.float32)
          + jnp.dot(nb, wo_ref[A_WIDTH:, :], preferred_element_type=jnp.float32))
    h = _rms(x1, gffn_ref[...]).astype(jnp.bfloat16)
    d_ff = wg_ref.shape[1]
    acc = jnp.zeros(x1.shape, jnp.float32)
    for c0 in range(0, d_ff, ff_chunk):
        gate = jnp.dot(h, wg_ref[:, c0:c0 + ff_chunk], preferred_element_type=jnp.float32)
        up = jnp.dot(h, wu_ref[:, c0:c0 + ff_chunk], preferred_element_type=jnp.float32)
        act = (jax.nn.silu(gate) * up).astype(jnp.bfloat16)
        acc = acc + jnp.dot(act, wd_ref[c0:c0 + ff_chunk, :], preferred_element_type=jnp.float32)
    o_ref[...] = _rms(x1 + acc, gfin_ref[...])


def _out_ffn(x2d, o_na, o_swa, g_na, g_swa, wo, g_ffn, wg, wu, wd, g_fin, tm, ff_chunk):
    n_tok = x2d.shape[0]
    d_ff = wg.shape[1]
    const = lambda i: (0, 0)
    row = lambda i: (i, 0)
    resident = lambda shape: pl.BlockSpec(shape, const, pipeline_mode=pl.Buffered(1))
    weights = (D_MODEL * D_MODEL + 3 * D_MODEL * d_ff) * 2
    vmem = weights + 2 * 2 * tm * D_MODEL * 4 + 2 * 2 * tm * A_WIDTH * 2 + 6 * tm * D_MODEL * 4 + 4 * tm * ff_chunk * 4
    return pl.pallas_call(
        functools.partial(_out_ffn_kernel, ff_chunk=ff_chunk),
        out_shape=jax.ShapeDtypeStruct((n_tok, D_MODEL), jnp.float32),
        grid=(n_tok // tm,),
        in_specs=[
            pl.BlockSpec((tm, D_MODEL), row),
            pl.BlockSpec((tm, A_WIDTH), row),
            pl.BlockSpec((tm, B_WIDTH), row),
            pl.BlockSpec((1, A_WIDTH), const),
            pl.BlockSpec((1, B_WIDTH), const),
            resident((D_MODEL, D_MODEL)),
            pl.BlockSpec((1, D_MODEL), const),
            resident((D_MODEL, d_ff)),
            resident((D_MODEL, d_ff)),
            resident((d_ff, D_MODEL)),
            pl.BlockSpec((1, D_MODEL), const),
        ],
        out_specs=pl.BlockSpec((tm, D_MODEL), row),
        compiler_params=pltpu.CompilerParams(
            dimension_semantics=("arbitrary",), vmem_limit_bytes=_vmem_limit(vmem + (8 << 20))),
        name="out_ffn",
    )(x2d, o_na, o_swa, g_na, g_swa, wo, g_ffn, wg, wu, wd, g_fin)


def kernel(x, g_norm_mix, w_in, na_rpb, swa_sink, g_out_na, g_out_swa, w_out, g_norm_ffn, w_gate, w_up, w_down,
           g_final):
    batch, seq, d_model = x.shape
    assert w_in.shape[0] == 1, "the final RMSNorm is fused into the single layer's FFN call"
    assert d_model == D_MODEL and seq % Q_BLK == 0 and seq // GRID_W >= NA_HALO_ROWS and seq >= SWA_KEYS
    bf16 = jnp.bfloat16
    rope = _rope_tables(seq)
    tm = 512
    x2d = x.reshape(batch * seq, d_model)
    proj = _in_proj(x2d, g_norm_mix[0][None, :], w_in[0].astype(bf16), rope, seq, tm)
    o_na = _na_attn(proj, na_rpb[0].reshape(-1), seq, batch)
    o_swa = _swa_attn(proj, swa_sink[0], seq, batch)
    out = _out_ffn(x2d, o_na, o_swa, g_out_na[0][None, :], g_out_swa[0][None, :], w_out[0].astype(bf16),
                   g_norm_ffn[0][None, :], w_gate[0].astype(bf16), w_up[0].astype(bf16),
                   w_down[0].astype(bf16), g_final[None, :], tm, 256)
    return out.reshape(batch, seq, d_model)
```

```python
import functools

import numpy as np
import jax
import jax.numpy as jnp
from jax import lax
from jax.experimental import pallas as pl
from jax.experimental.pallas import tpu as pltpu

D_MODEL = 1024
HEAD_DIM = 64
NA_HEADS = 8
SWA_HEADS = 8
SWA_KV_HEADS = 2
A_WIDTH = NA_HEADS * HEAD_DIM
B_WIDTH = SWA_HEADS * HEAD_DIM
B_KV_WIDTH = SWA_KV_HEADS * HEAD_DIM
IN_WIDTH = 3 * A_WIDTH + B_WIDTH + 2 * B_KV_WIDTH
GRID_W = 64
NA_ROWS = 8
NA_COLS = 16
SWA_WINDOW = 128
ROPE_THETA = 500000.0
ROPE_DIM = HEAD_DIM // 4
RMS_EPS = 1e-6
NEG_INF = -1e30
SCALE = HEAD_DIM ** -0.5

LANES = 128
VMEM_CAPACITY = 64 * 1024 * 1024

PROJ_WIDTH = IN_WIDTH + 2 * B_KV_WIDTH
COL_QA, COL_KA, COL_VA = 0, A_WIDTH, 2 * A_WIDTH
COL_QB = 3 * A_WIDTH
COL_KB = COL_QB + B_WIDTH
COL_VB = COL_KB + B_KV_WIDTH
COL_KB_SW = COL_VB + B_KV_WIDTH
COL_VB_SW = COL_KB_SW + B_KV_WIDTH

Q_BLK = 128
NA_Q_ROWS = Q_BLK // GRID_W
NA_HALO_ROWS = NA_ROWS + NA_Q_ROWS
NA_KEYS = NA_HALO_ROWS * GRID_W
SWA_KEYS = 3 * Q_BLK
NA_CLASSES = 5
ATTN_UNROLL = 4


def _vmem_limit(nbytes):
    return int(min(VMEM_CAPACITY - (2 << 20), max(nbytes, 16 << 20)))


def _na_class_of_block(i, nblk):
    if i < 2:
        return i
    if i >= nblk - 2:
        return NA_CLASSES - (nblk - i)
    return 2


def _na_key_row0(i, rows):
    return int(np.clip(NA_Q_ROWS * i - NA_ROWS // 2, 0, rows - NA_HALO_ROWS))


def _na_static_tables(seq):
    rows = seq // GRID_W
    nblk = seq // Q_BLK
    masks = np.zeros((NA_CLASSES, Q_BLK, NA_KEYS), np.float32)
    ridx = np.full((NA_CLASSES, NA_Q_ROWS, NA_HALO_ROWS), -1, np.int64)
    seen = {}
    for i in range(nblk):
        cls = _na_class_of_block(i, nblk)
        r0 = _na_key_row0(i, rows)
        q = np.arange(Q_BLK) + i * Q_BLK
        qr, qc = q // GRID_W, q % GRID_W
        k = np.arange(NA_KEYS) + r0 * GRID_W
        kr, kc = k // GRID_W, k % GRID_W
        rs = np.clip(qr - NA_ROWS // 2, 0, rows - NA_ROWS)
        cs = np.clip(qc - NA_COLS // 2, 0, GRID_W - NA_COLS)
        ok_r = (kr[None, :] >= rs[:, None]) & (kr[None, :] < rs[:, None] + NA_ROWS)
        ok_c = (kc[None, :] >= cs[:, None]) & (kc[None, :] < cs[:, None] + NA_COLS)
        m = np.where(ok_r & ok_c, 0.0, NEG_INF).astype(np.float32)
        rr = np.full((NA_Q_ROWS, NA_HALO_ROWS), -1, np.int64)
        for a in range(NA_Q_ROWS):
            for b in range(NA_HALO_ROWS):
                r, krow = NA_Q_ROWS * i + a, r0 + b
                if rs[a * GRID_W] <= krow < rs[a * GRID_W] + NA_ROWS:
                    rr[a, b] = krow - r + NA_ROWS - 1
        if cls in seen:
            assert np.array_equal(seen[cls][0], m) and np.array_equal(seen[cls][1], rr)
        seen[cls] = (m, rr)
        masks[cls], ridx[cls] = m, rr
    return masks, ridx


def _swa_static_masks(seq):
    nblk = seq // Q_BLK
    masks = np.zeros((3, Q_BLK, SWA_KEYS), np.float32)
    for cls, n in ((0, 0), (1, 1), (2, nblk - 1)):
        ks = int(np.clip(Q_BLK * (n - 1), 0, seq - SWA_KEYS))
        q = np.arange(Q_BLK) + n * Q_BLK
        k = np.arange(SWA_KEYS) + ks
        masks[cls] = np.where(np.abs(q[:, None] - k[None, :]) <= SWA_WINDOW, 0.0, NEG_INF)
    return masks


def _rope_tables(seq):
    half = ROPE_DIM // 2
    inv_freq = ROPE_THETA ** (-jnp.arange(0, ROPE_DIM, 2, dtype=jnp.float32) / ROPE_DIM)
    ang = jnp.arange(seq, dtype=jnp.int32).astype(jnp.float32)[:, None] * inv_freq[None, :]
    cos, sin = jnp.cos(ang), jnp.sin(ang)
    ones = jnp.ones((seq, HEAD_DIM - ROPE_DIM), jnp.float32)
    zeros = jnp.zeros((seq, HEAD_DIM - ROPE_DIM), jnp.float32)
    zh = jnp.zeros((seq, half), jnp.float32)
    c = jnp.concatenate([cos, cos, ones], axis=1)
    sa = jnp.concatenate([-sin, zh, zeros], axis=1)
    sb = jnp.concatenate([zh, sin, zeros], axis=1)
    rep = LANES // HEAD_DIM
    return jnp.tile(c, (1, rep)), jnp.tile(sa, (1, rep)), jnp.tile(sb, (1, rep))


def _rms(x, g):
    return x * lax.rsqrt(jnp.mean(x * x, axis=-1, keepdims=True) + RMS_EPS) * g


def _rope_tile(a, c, sa, sb):
    return a * c + pltpu.roll(a, LANES - ROPE_DIM // 2, axis=1) * sa + pltpu.roll(a, ROPE_DIM // 2, axis=1) * sb


def _in_proj_kernel(x_ref, g_ref, w_ref, c_ref, sa_ref, sb_ref, o_ref):
    h = _rms(x_ref[...], g_ref[...]).astype(jnp.bfloat16)
    c, sa, sb = c_ref[...], sa_ref[...], sb_ref[...]

    def proj(col, width):
        return jnp.dot(h, w_ref[:, col:col + width], preferred_element_type=jnp.float32)

    o_ref[:, COL_QA:COL_QA + A_WIDTH] = (proj(COL_QA, A_WIDTH) * SCALE).astype(o_ref.dtype)
    o_ref[:, COL_KA:COL_KA + A_WIDTH] = proj(COL_KA, A_WIDTH).astype(o_ref.dtype)
    o_ref[:, COL_VA:COL_VA + A_WIDTH] = proj(COL_VA, A_WIDTH).astype(o_ref.dtype)
    qb = proj(COL_QB, B_WIDTH)
    for t in range(B_WIDTH // LANES):
        lo = t * LANES
        tile = _rope_tile(qb[:, lo:lo + LANES], c, sa, sb) * SCALE
        o_ref[:, COL_QB + lo:COL_QB + lo + LANES] = tile.astype(o_ref.dtype)
    kv = proj(COL_KB, 2 * B_KV_WIDTH)
    kb = _rope_tile(kv[:, :B_KV_WIDTH], c, sa, sb)
    vb = kv[:, B_KV_WIDTH:]
    o_ref[:, COL_KB:COL_KB + B_KV_WIDTH] = kb.astype(o_ref.dtype)
    o_ref[:, COL_VB:COL_VB + B_KV_WIDTH] = vb.astype(o_ref.dtype)
    o_ref[:, COL_KB_SW:COL_KB_SW + B_KV_WIDTH] = pltpu.roll(kb, HEAD_DIM, axis=1).astype(o_ref.dtype)
    o_ref[:, COL_VB_SW:COL_VB_SW + B_KV_WIDTH] = pltpu.roll(vb, HEAD_DIM, axis=1).astype(o_ref.dtype)


def _in_proj(x2d, g, w_bf16, rope, seq, tm):
    n_tok = x2d.shape[0]
    pos_blocks = seq // tm
    const = lambda i: (0, 0)
    pos = lambda i: (i % pos_blocks, 0)
    vmem = 2 * tm * D_MODEL * 4 + 2 * D_MODEL * IN_WIDTH * 2 + 2 * tm * PROJ_WIDTH * 2 + 3 * tm * B_WIDTH * 4 * 2
    return pl.pallas_call(
        _in_proj_kernel,
        out_shape=jax.ShapeDtypeStruct((n_tok, PROJ_WIDTH), jnp.bfloat16),
        grid=(n_tok // tm,),
        in_specs=[
            pl.BlockSpec((tm, D_MODEL), lambda i: (i, 0)),
            pl.BlockSpec((1, D_MODEL), const),
            pl.BlockSpec((D_MODEL, IN_WIDTH), const),
            pl.BlockSpec((tm, LANES), pos),
            pl.BlockSpec((tm, LANES), pos),
            pl.BlockSpec((tm, LANES), pos),
        ],
        out_specs=pl.BlockSpec((tm, PROJ_WIDTH), lambda i: (i, 0)),
        compiler_params=pltpu.CompilerParams(
            dimension_semantics=("arbitrary",), vmem_limit_bytes=_vmem_limit(vmem + (8 << 20))),
        name="in_proj",
    )(x2d, g, w_bf16, *rope)


def _build_na_bias(rpb_ref, mask_ref, t_scr, hp, ridx):
    n_rrow, n_rcol = 2 * NA_ROWS - 1, 2 * NA_COLS - 1
    lane = lax.broadcasted_iota(jnp.int32, (GRID_W, LANES), 1)
    qc = lax.broadcasted_iota(jnp.int32, (GRID_W, LANES), 0)
    diff = (lane % GRID_W) - qc + (NA_COLS - 1)
    right = lane >= GRID_W
    used_rows = sorted({int(r) for r in ridx.reshape(-1) if r >= 0})
    for hh in range(2):
        h = 2 * hp + hh
        toeplitz = {}
        for rho in used_rows:
            acc = jnp.zeros((GRID_W, LANES), jnp.float32)
            for d in range(n_rcol):
                acc = jnp.where(diff == d, rpb_ref[(h * n_rrow + rho) * n_rcol + d], acc)
            toeplitz[rho] = acc
        for cls in range(NA_CLASSES):
            for a in range(NA_Q_ROWS):
                for j in range(NA_HALO_ROWS // 2):
                    r1, r2 = int(ridx[cls, a, 2 * j]), int(ridx[cls, a, 2 * j + 1])
                    zero = jnp.zeros((GRID_W, LANES), jnp.float32)
                    left = toeplitz[r1] if r1 >= 0 else zero
                    rght = toeplitz[r2] if r2 >= 0 else zero
                    blk = jnp.where(right, rght, left)
                    rs, cs = a * GRID_W, j * LANES
                    t_scr[cls, hh, rs:rs + GRID_W, cs:cs + LANES] = blk + mask_ref[cls, rs:rs + GRID_W, cs:cs + LANES]


def _na_kernel(rpb_ref, q_ref, k_ref, v_ref, mask_ref, o_ref, t_scr, *, ridx, seq):
    hp = pl.program_id(0)
    nblk = seq // Q_BLK
    rows = seq // GRID_W

    @pl.when(pl.program_id(1) == 0)
    def _():
        _build_na_bias(rpb_ref, mask_ref, t_scr, hp, ridx)

    lane = lax.broadcasted_iota(jnp.int32, (Q_BLK, LANES), 1)
    left = lane < HEAD_DIM

    def trip(g, carry):
        blocks, scores = [], []
        for u in range(ATTN_UNROLL):
            i = g * ATTN_UNROLL + u
            q0 = pl.multiple_of(i * Q_BLK, Q_BLK)
            r0 = jnp.clip(NA_Q_ROWS * i - NA_ROWS // 2, 0, rows - NA_HALO_ROWS)
            k0 = pl.multiple_of(r0 * GRID_W, GRID_W)
            cls = jnp.where(i < 2, i, jnp.where(i >= nblk - 2, NA_CLASSES - (nblk - i), 2))
            q2 = q_ref[pl.ds(q0, Q_BLK), :]
            k2 = k_ref[pl.ds(k0, NA_KEYS), :]
            zero = jnp.zeros_like(q2)
            blocks.append((q0, k0))
            for hh, qm in enumerate((jnp.where(left, q2, zero), jnp.where(left, zero, q2))):
                s = lax.dot_general(qm, k2, (((1,), (1,)), ((), ())), preferred_element_type=jnp.float32)
                scores.append(s + t_scr[cls, hh])
        probs = []
        for s in scores:
            p = jnp.exp(s - jnp.max(s, axis=-1, keepdims=True))
            probs.append((p.astype(jnp.bfloat16), jnp.sum(p, axis=-1, keepdims=True)))
        for u, (q0, k0) in enumerate(blocks):
            v2 = v_ref[pl.ds(k0, NA_KEYS), :]
            outs = [jnp.dot(p, v2, preferred_element_type=jnp.float32) / den for p, den in probs[2 * u:2 * u + 2]]
            o_ref[pl.ds(q0, Q_BLK), :] = jnp.where(left, outs[0], outs[1]).astype(o_ref.dtype)
        return carry

    lax.fori_loop(0, nblk // ATTN_UNROLL, trip, 0)


def _na_attn(proj, rpb_flat, seq, batch):
    masks, ridx = _na_static_tables(seq)
    n_tok = proj.shape[0]
    n_pairs = A_WIDTH // LANES
    col = lambda base: (lambda hp, b: (b, base // LANES + hp))
    vmem = 2 * 4 * seq * LANES * 2 + 2 * masks.size * 4 + NA_CLASSES * 2 * Q_BLK * NA_KEYS * 4
    return pl.pallas_call(
        functools.partial(_na_kernel, ridx=ridx, seq=seq),
        out_shape=jax.ShapeDtypeStruct((n_tok, A_WIDTH), jnp.bfloat16),
        grid=(n_pairs, batch),
        in_specs=[
            pl.BlockSpec(memory_space=pltpu.SMEM),
            pl.BlockSpec((seq, LANES), col(COL_QA)),
            pl.BlockSpec((seq, LANES), col(COL_KA)),
            pl.BlockSpec((seq, LANES), col(COL_VA)),
            pl.BlockSpec(masks.shape, lambda hp, b: (0, 0, 0)),
        ],
        out_specs=pl.BlockSpec((seq, LANES), lambda hp, b: (b, hp)),
        scratch_shapes=[pltpu.VMEM((NA_CLASSES, 2, Q_BLK, NA_KEYS), jnp.float32)],
        compiler_params=pltpu.CompilerParams(
            dimension_semantics=("arbitrary", "arbitrary"), vmem_limit_bytes=_vmem_limit(vmem + (16 << 20))),
        name="na_attn",
    )(rpb_flat, proj, proj, proj, jnp.asarray(masks))


def _swa_kernel(sink_ref, q_ref, ka_ref, kb_ref, va_ref, vb_ref, mask_ref, o_ref, *, seq):
    t = pl.program_id(1)
    nblk = seq // Q_BLK
    lane = lax.broadcasted_iota(jnp.int32, (Q_BLK, LANES), 1)
    left = lane < HEAD_DIM
    sinks = (sink_ref[2 * t], sink_ref[2 * t + 1])

    def trip(g, carry):
        blocks, scores = [], []
        for u in range(ATTN_UNROLL):
            n = g * ATTN_UNROLL + u
            q0 = pl.multiple_of(n * Q_BLK, Q_BLK)
            k0 = pl.multiple_of(jnp.clip(Q_BLK * (n - 1), 0, seq - SWA_KEYS), Q_BLK)
            cls = jnp.where(n == 0, 0, jnp.where(n == nblk - 1, 2, 1))
            q2 = q_ref[pl.ds(q0, Q_BLK), :]
            zero = jnp.zeros_like(q2)
            mask = mask_ref[cls]
            blocks.append((q0, k0))
            for qm, k_ref in ((jnp.where(left, q2, zero), ka_ref), (jnp.where(left, zero, q2), kb_ref)):
                k2 = k_ref[pl.ds(k0, SWA_KEYS), :]
                s = lax.dot_general(qm, k2, (((1,), (1,)), ((), ())), preferred_element_type=jnp.float32)
                scores.append(s + mask)
        probs = []
        for c, s in enumerate(scores):
            sink = sinks[c % 2]
            m = jnp.maximum(jnp.max(s, axis=-1, keepdims=True), sink)
            p = jnp.exp(s - m)
            probs.append((p.astype(jnp.bfloat16), jnp.sum(p, axis=-1, keepdims=True) + jnp.exp(sink - m)))
        for u, (q0, k0) in enumerate(blocks):
            outs = []
            for (p, den), v_ref in zip(probs[2 * u:2 * u + 2], (va_ref, vb_ref)):
                v2 = v_ref[pl.ds(k0, SWA_KEYS), :]
                outs.append(jnp.dot(p, v2, preferred_element_type=jnp.float32) / den)
            o_ref[pl.ds(q0, Q_BLK), :] = jnp.where(left, outs[0], outs[1]).astype(o_ref.dtype)
        return carry

    lax.fori_loop(0, nblk // ATTN_UNROLL, trip, 0)


def _swa_attn(proj, sink, seq, batch):
    masks = _swa_static_masks(seq)
    n_tok = proj.shape[0]
    n_tiles = B_WIDTH // LANES
    tiles_per_kv = n_tiles // SWA_KV_HEADS
    kb, vb, kbs, vbs = (c // LANES for c in (COL_KB, COL_VB, COL_KB_SW, COL_VB_SW))
    left_kv = lambda plain, swapped: (lambda b, t: (b, plain + (swapped - plain) * (t // tiles_per_kv)))
    right_kv = lambda plain, swapped: (lambda b, t: (b, swapped + (plain - swapped) * (t // tiles_per_kv)))
    vmem = 2 * 6 * seq * LANES * 2 + 2 * masks.size * 4
    return pl.pallas_call(
        functools.partial(_swa_kernel, seq=seq),
        out_shape=jax.ShapeDtypeStruct((n_tok, B_WIDTH), jnp.bfloat16),
        grid=(batch, n_tiles),
        in_specs=[
            pl.BlockSpec(memory_space=pltpu.SMEM),
            pl.BlockSpec((seq, LANES), lambda b, t: (b, COL_QB // LANES + t)),
            pl.BlockSpec((seq, LANES), left_kv(kb, kbs)),
            pl.BlockSpec((seq, LANES), right_kv(kb, kbs)),
            pl.BlockSpec((seq, LANES), left_kv(vb, vbs)),
            pl.BlockSpec((seq, LANES), right_kv(vb, vbs)),
            pl.BlockSpec(masks.shape, lambda b, t: (0, 0, 0)),
        ],
        out_specs=pl.BlockSpec((seq, LANES), lambda b, t: (b, t)),
        compiler_params=pltpu.CompilerParams(
            dimension_semantics=("arbitrary", "arbitrary"), vmem_limit_bytes=_vmem_limit(vmem + (16 << 20))),
        name="swa_attn",
    )(sink, proj, proj, proj, proj, proj, jnp.asarray(masks))


def _out_ffn_kernel(x_ref, oa_ref, ob_ref, gna_ref, gswa_ref, wo_ref, gffn_ref, wg_ref, wu_ref, wd_ref,
                    gfin_ref, o_ref, *, ff_chunk):
    na = _rms(oa_ref[...].astype(jnp.float32), gna_ref[...]).astype(jnp.bfloat16)
    nb = _rms(ob_ref[...].astype(jnp.float32), gswa_ref[...]).astype(jnp.bfloat16)
    x1 = (x_ref[...]
          + jnp.dot(na, wo_ref[:A_WIDTH, :], preferred_element_type=jnp.float32)
          + jnp.dot(nb, wo_ref[A_WIDTH:, :], preferred_element_type=jnp.float32))
    h = _rms(x1, gffn_ref[...]).astype(jnp.bfloat16)
    d_ff = wg_ref.shape[1]
    acc = jnp.zeros(x1.shape, jnp.float32)
    for c0 in range(0, d_ff, ff_chunk):
        gate = jnp.dot(h, wg_ref[:, c0:c0 + ff_chunk], preferred_element_type=jnp.float32)
        up = jnp.dot(h, wu_ref[:, c0:c0 + ff_chunk], preferred_element_type=jnp.float32)
        act = (jax.nn.silu(gate) * up).astype(jnp.bfloat16)
        acc = acc + jnp.dot(act, wd_ref[c0:c0 + ff_chunk, :], preferred_element_type=jnp.float32)
    o_ref[...] = _rms(x1 + acc, gfin_ref[...])


def _out_ffn(x2d, o_na, o_swa, g_na, g_swa, wo, g_ffn, wg, wu, wd, g_fin, tm, ff_chunk):
    n_tok = x2d.shape[0]
    d_ff = wg.shape[1]
    const = lambda i: (0, 0)
    row = lambda i: (i, 0)
    resident = lambda shape: pl.BlockSpec(shape, const, pipeline_mode=pl.Buffered(1))
    weights = (D_MODEL * D_MODEL + 3 * D_MODEL * d_ff) * 2
    vmem = weights + 2 * 2 * tm * D_MODEL * 4 + 2 * 2 * tm * A_WIDTH * 2 + 6 * tm * D_MODEL * 4 + 4 * tm * ff_chunk * 4
    return pl.pallas_call(
        functools.partial(_out_ffn_kernel, ff_chunk=ff_chunk),
        out_shape=jax.ShapeDtypeStruct((n_tok, D_MODEL), jnp.float32),
        grid=(n_tok // tm,),
        in_specs=[
            pl.BlockSpec((tm, D_MODEL), row),
            pl.BlockSpec((tm, A_WIDTH), row),
            pl.BlockSpec((tm, B_WIDTH), row),
            pl.BlockSpec((1, A_WIDTH), const),
            pl.BlockSpec((1, B_WIDTH), const),
            resident((D_MODEL, D_MODEL)),
            pl.BlockSpec((1, D_MODEL), const),
            resident((D_MODEL, d_ff)),
            resident((D_MODEL, d_ff)),
            resident((d_ff, D_MODEL)),
            pl.BlockSpec((1, D_MODEL), const),
        ],
        out_specs=pl.BlockSpec((tm, D_MODEL), row),
        compiler_params=pltpu.CompilerParams(
            dimension_semantics=("arbitrary",), vmem_limit_bytes=_vmem_limit(vmem + (8 << 20))),
        name="out_ffn",
    )(x2d, o_na, o_swa, g_na, g_swa, wo, g_ffn, wg, wu, wd, g_fin)


def kernel(x, g_norm_mix, w_in, na_rpb, swa_sink, g_out_na, g_out_swa, w_out, g_norm_ffn, w_gate, w_up, w_down,
           g_final):
    batch, seq, d_model = x.shape
    assert w_in.shape[0] == 1, "the final RMSNorm is fused into the single layer's FFN call"
    assert d_model == D_MODEL and seq % Q_BLK == 0 and seq // GRID_W >= NA_HALO_ROWS and seq >= SWA_KEYS
    bf16 = jnp.bfloat16
    rope = _rope_tables(seq)
    tm = 512
    x2d = x.reshape(batch * seq, d_model)
    proj = _in_proj(x2d, g_norm_mix[0][None, :], w_in[0].astype(bf16), rope, seq, tm)
    o_na = _na_attn(proj, na_rpb[0].reshape(-1), seq, batch)
    o_swa = _swa_attn(proj, swa_sink[0], seq, batch)
    out = _out_ffn(x2d, o_na, o_swa, g_out_na[0][None, :], g_out_swa[0][None, :], w_out[0].astype(bf16),
                   g_norm_ffn[0][None, :], w_gate[0].astype(bf16), w_up[0].astype(bf16),
                   w_down[0].astype(bf16), g_final[None, :], tm, 256)
    return out.reshape(batch, seq, d_model)
```

```python
import functools

import numpy as np
import jax
import jax.numpy as jnp
from jax import lax
from jax.experimental import pallas as pl
from jax.experimental.pallas import tpu as pltpu

D_MODEL = 1024
HEAD_DIM = 64
NA_HEADS = 8
SWA_HEADS = 8
SWA_KV_HEADS = 2
A_WIDTH = NA_HEADS * HEAD_DIM
B_WIDTH = SWA_HEADS * HEAD_DIM
B_KV_WIDTH = SWA_KV_HEADS * HEAD_DIM
IN_WIDTH = 3 * A_WIDTH + B_WIDTH + 2 * B_KV_WIDTH
GRID_W = 64
NA_ROWS = 8
NA_COLS = 16
SWA_WINDOW = 128
ROPE_THETA = 500000.0
ROPE_DIM = HEAD_DIM // 4
RMS_EPS = 1e-6
NEG_INF = -1e30
LOG2E = 1.4426950408889634
Q_SCALE = HEAD_DIM ** -0.5 * LOG2E

LANES = 128
VMEM_CAPACITY = 64 * 1024 * 1024

PROJ_WIDTH = IN_WIDTH + 2 * B_KV_WIDTH
COL_QA, COL_KA, COL_VA = 0, A_WIDTH, 2 * A_WIDTH
COL_QB = 3 * A_WIDTH
COL_KB = COL_QB + B_WIDTH
COL_VB = COL_KB + B_KV_WIDTH
COL_KB_SW = COL_VB + B_KV_WIDTH
COL_VB_SW = COL_KB_SW + B_KV_WIDTH

Q_BLK = 128
NA_Q_ROWS = Q_BLK // GRID_W
NA_HALO_ROWS = NA_ROWS + NA_Q_ROWS
NA_KEYS = NA_HALO_ROWS * GRID_W
SWA_KEYS = 3 * Q_BLK
NA_CLASSES = 5
ATTN_UNROLL = 8


def _vmem_limit(nbytes):
    return int(min(VMEM_CAPACITY - (2 << 20), max(nbytes, 16 << 20)))


def _na_class_of_block(i, nblk):
    if i < 2:
        return i
    if i >= nblk - 2:
        return NA_CLASSES - (nblk - i)
    return 2


def _na_key_row0(i, rows):
    return int(np.clip(NA_Q_ROWS * i - NA_ROWS // 2, 0, rows - NA_HALO_ROWS))


def _na_static_tables(seq):
    rows = seq // GRID_W
    nblk = seq // Q_BLK
    masks = np.zeros((NA_CLASSES, Q_BLK, NA_KEYS), np.float32)
    ridx = np.full((NA_CLASSES, NA_Q_ROWS, NA_HALO_ROWS), -1, np.int64)
    seen = {}
    for i in range(nblk):
        cls = _na_class_of_block(i, nblk)
        r0 = _na_key_row0(i, rows)
        q = np.arange(Q_BLK) + i * Q_BLK
        qr, qc = q // GRID_W, q % GRID_W
        k = np.arange(NA_KEYS) + r0 * GRID_W
        kr, kc = k // GRID_W, k % GRID_W
        rs = np.clip(qr - NA_ROWS // 2, 0, rows - NA_ROWS)
        cs = np.clip(qc - NA_COLS // 2, 0, GRID_W - NA_COLS)
        ok_r = (kr[None, :] >= rs[:, None]) & (kr[None, :] < rs[:, None] + NA_ROWS)
        ok_c = (kc[None, :] >= cs[:, None]) & (kc[None, :] < cs[:, None] + NA_COLS)
        m = np.where(ok_r & ok_c, 0.0, NEG_INF).astype(np.float32)
        rr = np.full((NA_Q_ROWS, NA_HALO_ROWS), -1, np.int64)
        for a in range(NA_Q_ROWS):
            for b in range(NA_HALO_ROWS):
                r, krow = NA_Q_ROWS * i + a, r0 + b
                if rs[a * GRID_W] <= krow < rs[a * GRID_W] + NA_ROWS:
                    rr[a, b] = krow - r + NA_ROWS - 1
        if cls in seen:
            assert np.array_equal(seen[cls][0], m) and np.array_equal(seen[cls][1], rr)
        seen[cls] = (m, rr)
        masks[cls], ridx[cls] = m, rr
    return masks, ridx


def _swa_static_masks(seq):
    nblk = seq // Q_BLK
    masks = np.zeros((3, Q_BLK, SWA_KEYS), np.float32)
    for cls, n in ((0, 0), (1, 1), (2, nblk - 1)):
        ks = int(np.clip(Q_BLK * (n - 1), 0, seq - SWA_KEYS))
        q = np.arange(Q_BLK) + n * Q_BLK
        k = np.arange(SWA_KEYS) + ks
        masks[cls] = np.where(np.abs(q[:, None] - k[None, :]) <= SWA_WINDOW, 0.0, NEG_INF)
    return masks


def _rope_tables(seq):
    half = ROPE_DIM // 2
    inv_freq = ROPE_THETA ** (-jnp.arange(0, ROPE_DIM, 2, dtype=jnp.float32) / ROPE_DIM)
    ang = jnp.arange(seq, dtype=jnp.int32).astype(jnp.float32)[:, None] * inv_freq[None, :]
    cos, sin = jnp.cos(ang), jnp.sin(ang)
    ones = jnp.ones((seq, HEAD_DIM - ROPE_DIM), jnp.float32)
    zeros = jnp.zeros((seq, HEAD_DIM - ROPE_DIM), jnp.float32)
    zh = jnp.zeros((seq, half), jnp.float32)
    c = jnp.concatenate([cos, cos, ones], axis=1)
    sa = jnp.concatenate([-sin, zh, zeros], axis=1)
    sb = jnp.concatenate([zh, sin, zeros], axis=1)
    rep = LANES // HEAD_DIM
    return jnp.tile(c, (1, rep)), jnp.tile(sa, (1, rep)), jnp.tile(sb, (1, rep))


def _rms(x, g):
    return x * lax.rsqrt(jnp.mean(x * x, axis=-1, keepdims=True) + RMS_EPS) * g


def _rope_tile(a, c, sa, sb):
    return a * c + pltpu.roll(a, LANES - ROPE_DIM // 2, axis=1) * sa + pltpu.roll(a, ROPE_DIM // 2, axis=1) * sb


def _in_proj_kernel(x_ref, g_ref, w_ref, c_ref, sa_ref, sb_ref, o_ref):
    h = _rms(x_ref[...], g_ref[...]).astype(jnp.bfloat16)
    c, sa, sb = c_ref[...], sa_ref[...], sb_ref[...]

    def proj(col, width):
        return jnp.dot(h, w_ref[:, col:col + width], preferred_element_type=jnp.float32)

    o_ref[:, COL_QA:COL_QA + A_WIDTH] = (proj(COL_QA, A_WIDTH) * Q_SCALE).astype(o_ref.dtype)
    o_ref[:, COL_KA:COL_KA + A_WIDTH] = proj(COL_KA, A_WIDTH).astype(o_ref.dtype)
    o_ref[:, COL_VA:COL_VA + A_WIDTH] = proj(COL_VA, A_WIDTH).astype(o_ref.dtype)
    qb = proj(COL_QB, B_WIDTH)
    for t in range(B_WIDTH // LANES):
        lo = t * LANES
        tile = _rope_tile(qb[:, lo:lo + LANES], c, sa, sb) * Q_SCALE
        o_ref[:, COL_QB + lo:COL_QB + lo + LANES] = tile.astype(o_ref.dtype)
    kv = proj(COL_KB, 2 * B_KV_WIDTH)
    kb = _rope_tile(kv[:, :B_KV_WIDTH], c, sa, sb)
    vb = kv[:, B_KV_WIDTH:]
    o_ref[:, COL_KB:COL_KB + B_KV_WIDTH] = kb.astype(o_ref.dtype)
    o_ref[:, COL_VB:COL_VB + B_KV_WIDTH] = vb.astype(o_ref.dtype)
    o_ref[:, COL_KB_SW:COL_KB_SW + B_KV_WIDTH] = pltpu.roll(kb, HEAD_DIM, axis=1).astype(o_ref.dtype)
    o_ref[:, COL_VB_SW:COL_VB_SW + B_KV_WIDTH] = pltpu.roll(vb, HEAD_DIM, axis=1).astype(o_ref.dtype)


def _in_proj(x2d, g, w_bf16, rope, seq, tm):
    n_tok = x2d.shape[0]
    pos_blocks = seq // tm
    const = lambda i: (0, 0)
    pos = lambda i: (i % pos_blocks, 0)
    vmem = 2 * tm * D_MODEL * 4 + 2 * D_MODEL * IN_WIDTH * 2 + 2 * tm * PROJ_WIDTH * 2 + 3 * tm * B_WIDTH * 4 * 2
    return pl.pallas_call(
        _in_proj_kernel,
        out_shape=jax.ShapeDtypeStruct((n_tok, PROJ_WIDTH), jnp.bfloat16),
        grid=(n_tok // tm,),
        in_specs=[
            pl.BlockSpec((tm, D_MODEL), lambda i: (i, 0)),
            pl.BlockSpec((1, D_MODEL), const),
            pl.BlockSpec((D_MODEL, IN_WIDTH), const),
            pl.BlockSpec((tm, LANES), pos),
            pl.BlockSpec((tm, LANES), pos),
            pl.BlockSpec((tm, LANES), pos),
        ],
        out_specs=pl.BlockSpec((tm, PROJ_WIDTH), lambda i: (i, 0)),
        compiler_params=pltpu.CompilerParams(
            dimension_semantics=("arbitrary",), vmem_limit_bytes=_vmem_limit(vmem + (8 << 20))),
        name="in_proj",
    )(x2d, g, w_bf16, *rope)


def _build_na_bias(rpb_ref, mask_ref, t_scr, hp, ridx):
    n_rrow, n_rcol = 2 * NA_ROWS - 1, 2 * NA_COLS - 1
    lane = lax.broadcasted_iota(jnp.int32, (GRID_W, LANES), 1)
    qc = lax.broadcasted_iota(jnp.int32, (GRID_W, LANES), 0)
    diff = (lane % GRID_W) - qc + (NA_COLS - 1)
    right = lane >= GRID_W
    used_rows = sorted({int(r) for r in ridx.reshape(-1) if r >= 0})
    for hh in range(2):
        h = 2 * hp + hh
        toeplitz = {}
        for rho in used_rows:
            acc = jnp.zeros((GRID_W, LANES), jnp.float32)
            for d in range(n_rcol):
                acc = jnp.where(diff == d, rpb_ref[(h * n_rrow + rho) * n_rcol + d] * LOG2E, acc)
            toeplitz[rho] = acc
        for cls in range(NA_CLASSES):
            for a in range(NA_Q_ROWS):
                for j in range(NA_HALO_ROWS // 2):
                    r1, r2 = int(ridx[cls, a, 2 * j]), int(ridx[cls, a, 2 * j + 1])
                    zero = jnp.zeros((GRID_W, LANES), jnp.float32)
                    left = toeplitz[r1] if r1 >= 0 else zero
                    rght = toeplitz[r2] if r2 >= 0 else zero
                    blk = jnp.where(right, rght, left)
                    rs, cs = a * GRID_W, j * LANES
                    t_scr[cls, hh, rs:rs + GRID_W, cs:cs + LANES] = blk + mask_ref[cls, rs:rs + GRID_W, cs:cs + LANES]


def _na_kernel(rpb_ref, q_ref, k_ref, v_ref, mask_ref, o_ref, t_scr, *, ridx, seq):
    hp = pl.program_id(0)
    nblk = seq // Q_BLK
    rows = seq // GRID_W

    @pl.when(pl.program_id(1) == 0)
    def _():
        _build_na_bias(rpb_ref, mask_ref, t_scr, hp, ridx)

    lane = lax.broadcasted_iota(jnp.int32, (Q_BLK, LANES), 1)
    left = lane < HEAD_DIM

    def trip(g, carry):
        blocks, scores = [], []
        for u in range(ATTN_UNROLL):
            i = g * ATTN_UNROLL + u
            q0 = pl.multiple_of(i * Q_BLK, Q_BLK)
            r0 = jnp.clip(NA_Q_ROWS * i - NA_ROWS // 2, 0, rows - NA_HALO_ROWS)
            k0 = pl.multiple_of(r0 * GRID_W, GRID_W)
            cls = jnp.where(i < 2, i, jnp.where(i >= nblk - 2, NA_CLASSES - (nblk - i), 2))
            q2 = q_ref[pl.ds(q0, Q_BLK), :]
            k2 = k_ref[pl.ds(k0, NA_KEYS), :]
            zero = jnp.zeros_like(q2)
            blocks.append((q0, k0))
            for hh, qm in enumerate((jnp.where(left, q2, zero), jnp.where(left, zero, q2))):
                s = lax.dot_general(qm, k2, (((1,), (1,)), ((), ())), preferred_element_type=jnp.float32)
                scores.append(s + t_scr[cls, hh])
        probs = []
        for s in scores:
            p = jnp.exp2(s - jnp.max(s, axis=-1, keepdims=True))
            probs.append((p.astype(jnp.bfloat16), jnp.sum(p, axis=-1, keepdims=True)))
        for u, (q0, k0) in enumerate(blocks):
            v2 = v_ref[pl.ds(k0, NA_KEYS), :]
            outs = [jnp.dot(p, v2, preferred_element_type=jnp.float32) / den for p, den in probs[2 * u:2 * u + 2]]
            o_ref[pl.ds(q0, Q_BLK), :] = jnp.where(left, outs[0], outs[1]).astype(o_ref.dtype)
        return carry

    lax.fori_loop(0, nblk // ATTN_UNROLL, trip, 0)


def _na_attn(proj, rpb_flat, seq, batch):
    masks, ridx = _na_static_tables(seq)
    n_tok = proj.shape[0]
    n_pairs = A_WIDTH // LANES
    col = lambda base: (lambda hp, b: (b, base // LANES + hp))
    vmem = 2 * 4 * seq * LANES * 2 + 2 * masks.size * 4 + NA_CLASSES * 2 * Q_BLK * NA_KEYS * 4
    return pl.pallas_call(
        functools.partial(_na_kernel, ridx=ridx, seq=seq),
        out_shape=jax.ShapeDtypeStruct((n_tok, A_WIDTH), jnp.bfloat16),
        grid=(n_pairs, batch),
        in_specs=[
            pl.BlockSpec(memory_space=pltpu.SMEM),
            pl.BlockSpec((seq, LANES), col(COL_QA)),
            pl.BlockSpec((seq, LANES), col(COL_KA)),
            pl.BlockSpec((seq, LANES), col(COL_VA)),
            pl.BlockSpec(masks.shape, lambda hp, b: (0, 0, 0)),
        ],
        out_specs=pl.BlockSpec((seq, LANES), lambda hp, b: (b, hp)),
        scratch_shapes=[pltpu.VMEM((NA_CLASSES, 2, Q_BLK, NA_KEYS), jnp.float32)],
        compiler_params=pltpu.CompilerParams(
            dimension_semantics=("arbitrary", "arbitrary"), vmem_limit_bytes=_vmem_limit(vmem + (16 << 20))),
        name="na_attn",
    )(rpb_flat, proj, proj, proj, jnp.asarray(masks))


def _swa_kernel(sink_ref, q_ref, ka_ref, kb_ref, va_ref, vb_ref, mask_ref, o_ref, *, seq):
    t = pl.program_id(1)
    nblk = seq // Q_BLK
    lane = lax.broadcasted_iota(jnp.int32, (Q_BLK, LANES), 1)
    left = lane < HEAD_DIM
    sinks = (sink_ref[2 * t] * LOG2E, sink_ref[2 * t + 1] * LOG2E)

    def trip(g, carry):
        blocks, scores = [], []
        for u in range(ATTN_UNROLL):
            n = g * ATTN_UNROLL + u
            q0 = pl.multiple_of(n * Q_BLK, Q_BLK)
            k0 = pl.multiple_of(jnp.clip(Q_BLK * (n - 1), 0, seq - SWA_KEYS), Q_BLK)
            cls = jnp.where(n == 0, 0, jnp.where(n == nblk - 1, 2, 1))
            q2 = q_ref[pl.ds(q0, Q_BLK), :]
            zero = jnp.zeros_like(q2)
            mask = mask_ref[cls]
            blocks.append((q0, k0))
            for qm, k_ref in ((jnp.where(left, q2, zero), ka_ref), (jnp.where(left, zero, q2), kb_ref)):
                k2 = k_ref[pl.ds(k0, SWA_KEYS), :]
                s = lax.dot_general(qm, k2, (((1,), (1,)), ((), ())), preferred_element_type=jnp.float32)
                scores.append(s + mask)
        probs = []
        for c, s in enumerate(scores):
            sink = sinks[c % 2]
            m = jnp.maximum(jnp.max(s, axis=-1, keepdims=True), sink)
            p = jnp.exp2(s - m)
            probs.append((p.astype(jnp.bfloat16), jnp.sum(p, axis=-1, keepdims=True) + jnp.exp2(sink - m)))
        for u, (q0, k0) in enumerate(blocks):
            outs = []
            for (p, den), v_ref in zip(probs[2 * u:2 * u + 2], (va_ref, vb_ref)):
                v2 = v_ref[pl.ds(k0, SWA_KEYS), :]
                outs.append(jnp.dot(p, v2, preferred_element_type=jnp.float32) / den)
            o_ref[pl.ds(q0, Q_BLK), :] = jnp.where(left, outs[0], outs[1]).astype(o_ref.dtype)
        return carry

    lax.fori_loop(0, nblk // ATTN_UNROLL, trip, 0)


def _swa_attn(proj, sink, seq, batch):
    masks = _swa_static_masks(seq)
    n_tok = proj.shape[0]
    n_tiles = B_WIDTH // LANES
    tiles_per_kv = n_tiles // SWA_KV_HEADS
    kb, vb, kbs, vbs = (c // LANES for c in (COL_KB, COL_VB, COL_KB_SW, COL_VB_SW))
    left_kv = lambda plain, swapped: (lambda b, t: (b, plain + (swapped - plain) * (t // tiles_per_kv)))
    right_kv = lambda plain, swapped: (lambda b, t: (b, swapped + (plain - swapped) * (t // tiles_per_kv)))
    vmem = 2 * 6 * seq * LANES * 2 + 2 * masks.size * 4
    return pl.pallas_call(
        functools.partial(_swa_kernel, seq=seq),
        out_shape=jax.ShapeDtypeStruct((n_tok, B_WIDTH), jnp.bfloat16),
        grid=(batch, n_tiles),
        in_specs=[
            pl.BlockSpec(memory_space=pltpu.SMEM),
            pl.BlockSpec((seq, LANES), lambda b, t: (b, COL_QB // LANES + t)),
            pl.BlockSpec((seq, LANES), left_kv(kb, kbs)),
            pl.BlockSpec((seq, LANES), right_kv(kb, kbs)),
            pl.BlockSpec((seq, LANES), left_kv(vb, vbs)),
            pl.BlockSpec((seq, LANES), right_kv(vb, vbs)),
            pl.BlockSpec(masks.shape, lambda b, t: (0, 0, 0)),
        ],
        out_specs=pl.BlockSpec((seq, LANES), lambda b, t: (b, t)),
        compiler_params=pltpu.CompilerParams(
            dimension_semantics=("arbitrary", "arbitrary"), vmem_limit_bytes=_vmem_limit(vmem + (16 << 20))),
        name="swa_attn",
    )(sink, proj, proj, proj, proj, proj, jnp.asarray(masks))


def _out_ffn_kernel(x_ref, oa_ref, ob_ref, gna_ref, gswa_ref, wo_ref, gffn_ref, wg_ref, wu_ref, wd_ref,
                    gfin_ref, o_ref, *, ff_chunk):
    na = _rms(oa_ref[...].astype(jnp.float32), gna_ref[...]).astype(jnp.bfloat16)
    nb = _rms(ob_ref[...].astype(jnp.float32), gswa_ref[...]).astype(jnp.bfloat16)
    x1 = (x_ref[...]
          + jnp.dot(na, wo_ref[:A_WIDTH, :], preferred_element_type=jnp.float32)
          + jnp.dot(nb, wo_ref[A_WIDTH:, :], preferred_element_type=jnp.float32))
    h = _rms(x1, gffn_ref[...]).astype(jnp.bfloat16)
    d_ff = wg_ref.shape[1]
    acc = jnp.zeros(x1.shape, jnp.float32)
    for c0 in range(0, d_ff, ff_chunk):
        gate = jnp.dot(h, wg_ref[:, c0:c0 + ff_chunk], preferred_element_type=jnp.float32)
        up = jnp.dot(h, wu_ref[:, c0:c0 + ff_chunk], preferred_element_type=jnp.float32)
        act = (jax.nn.silu(gate) * up).astype(jnp.bfloat16)
        acc = acc + jnp.dot(act, wd_ref[c0:c0 + ff_chunk, :], preferred_element_type=jnp.float32)
    o_ref[...] = _rms(x1 + acc, gfin_ref[...])


def _out_ffn(x2d, o_na, o_swa, g_na, g_swa, wo, g_ffn, wg, wu, wd, g_fin, tm, ff_chunk):
    n_tok = x2d.shape[0]
    d_ff = wg.shape[1]
    const = lambda i: (0, 0)
    row = lambda i: (i, 0)
    resident = lambda shape: pl.BlockSpec(shape, const, pipeline_mode=pl.Buffered(1))
    weights = (D_MODEL * D_MODEL + 3 * D_MODEL * d_ff) * 2
    vmem = weights + 2 * 2 * tm * D_MODEL * 4 + 2 * 2 * tm * A_WIDTH * 2 + 6 * tm * D_MODEL * 4 + 4 * tm * ff_chunk * 4
    return pl.pallas_call(
        functools.partial(_out_ffn_kernel, ff_chunk=ff_chunk),
        out_shape=jax.ShapeDtypeStruct((n_tok, D_MODEL), jnp.float32),
        grid=(n_tok // tm,),
        in_specs=[
            pl.BlockSpec((tm, D_MODEL), row),
            pl.BlockSpec((tm, A_WIDTH), row),
            pl.BlockSpec((tm, B_WIDTH), row),
            pl.BlockSpec((1, A_WIDTH), const),
            pl.BlockSpec((1, B_WIDTH), const),
            resident((D_MODEL, D_MODEL)),
            pl.BlockSpec((1, D_MODEL), const),
            resident((D_MODEL, d_ff)),
            resident((D_MODEL, d_ff)),
            resident((d_ff, D_MODEL)),
            pl.BlockSpec((1, D_MODEL), const),
        ],
        out_specs=pl.BlockSpec((tm, D_MODEL), row),
        compiler_params=pltpu.CompilerParams(
            dimension_semantics=("arbitrary",), vmem_limit_bytes=_vmem_limit(vmem + (8 << 20))),
        name="out_ffn",
    )(x2d, o_na, o_swa, g_na, g_swa, wo, g_ffn, wg, wu, wd, g_fin)


def kernel(x, g_norm_mix, w_in, na_rpb, swa_sink, g_out_na, g_out_swa, w_out, g_norm_ffn, w_gate, w_up, w_down,
           g_final):
    batch, seq, d_model = x.shape
    assert w_in.shape[0] == 1, "the final RMSNorm is fused into the single layer's FFN call"
    assert d_model == D_MODEL and seq % (Q_BLK * ATTN_UNROLL) == 0 and seq // GRID_W >= NA_HALO_ROWS and seq >= SWA_KEYS
    bf16 = jnp.bfloat16
    rope = _rope_tables(seq)
    tm = 512
    x2d = x.reshape(batch * seq, d_model)
    proj = _in_proj(x2d, g_norm_mix[0][None, :], w_in[0].astype(bf16), rope, seq, tm)
    o_na = _na_attn(proj, na_rpb[0].reshape(-1), seq, batch)
    o_swa = _swa_attn(proj, swa_sink[0], seq, batch)
    out = _out_ffn(x2d, o_na, o_swa, g_out_na[0][None, :], g_out_swa[0][None, :], w_out[0].astype(bf16),
                   g_norm_ffn[0][None, :], w_gate[0].astype(bf16), w_up[0].astype(bf16),
                   w_down[0].astype(bf16), g_final[None, :], tm, 256)
    return out.reshape(batch, seq, d_model)
```

```python
import functools

import numpy as np
import jax
import jax.numpy as jnp
from jax import lax
from jax.experimental import pallas as pl
from jax.experimental.pallas import tpu as pltpu

D_MODEL = 1024
HEAD_DIM = 64
NA_HEADS = 8
SWA_HEADS = 8
SWA_KV_HEADS = 2
A_WIDTH = NA_HEADS * HEAD_DIM
B_WIDTH = SWA_HEADS * HEAD_DIM
B_KV_WIDTH = SWA_KV_HEADS * HEAD_DIM
IN_WIDTH = 3 * A_WIDTH + B_WIDTH + 2 * B_KV_WIDTH
GRID_W = 64
NA_ROWS = 8
NA_COLS = 16
SWA_WINDOW = 128
ROPE_THETA = 500000.0
ROPE_DIM = HEAD_DIM // 4
RMS_EPS = 1e-6
NEG_INF = -1e30
LOG2E = 1.4426950408889634
Q_SCALE = HEAD_DIM ** -0.5 * LOG2E

LANES = 128
VMEM_CAPACITY = 64 * 1024 * 1024

PROJ_WIDTH = IN_WIDTH + 2 * B_KV_WIDTH
COL_QA, COL_KA, COL_VA = 0, A_WIDTH, 2 * A_WIDTH
COL_QB = 3 * A_WIDTH
COL_KB = COL_QB + B_WIDTH
COL_VB = COL_KB + B_KV_WIDTH
COL_KB_SW = COL_VB + B_KV_WIDTH
COL_VB_SW = COL_KB_SW + B_KV_WIDTH

Q_BLK = 128
NA_Q_ROWS = Q_BLK // GRID_W
NA_HALO_ROWS = NA_ROWS + NA_Q_ROWS
NA_KEYS = NA_HALO_ROWS * GRID_W
SWA_KEYS = 3 * Q_BLK
NA_CLASSES = 5
ATTN_UNROLL = 8
IN_PROJ_ROWS = 512
FFN_ROWS = 1024
FFN_SUB_ROWS = 512
FFN_CHUNK = 256


def _vmem_limit(nbytes):
    return int(min(VMEM_CAPACITY - (2 << 20), max(nbytes, 16 << 20)))


def _na_class_of_block(i, nblk):
    if i < 2:
        return i
    if i >= nblk - 2:
        return NA_CLASSES - (nblk - i)
    return 2


def _na_key_row0(i, rows):
    return int(np.clip(NA_Q_ROWS * i - NA_ROWS // 2, 0, rows - NA_HALO_ROWS))


def _na_static_tables(seq):
    rows = seq // GRID_W
    nblk = seq // Q_BLK
    masks = np.zeros((NA_CLASSES, Q_BLK, NA_KEYS), np.float32)
    ridx = np.full((NA_CLASSES, NA_Q_ROWS, NA_HALO_ROWS), -1, np.int64)
    seen = {}
    for i in range(nblk):
        cls = _na_class_of_block(i, nblk)
        r0 = _na_key_row0(i, rows)
        q = np.arange(Q_BLK) + i * Q_BLK
        qr, qc = q // GRID_W, q % GRID_W
        k = np.arange(NA_KEYS) + r0 * GRID_W
        kr, kc = k // GRID_W, k % GRID_W
        rs = np.clip(qr - NA_ROWS // 2, 0, rows - NA_ROWS)
        cs = np.clip(qc - NA_COLS // 2, 0, GRID_W - NA_COLS)
        ok_r = (kr[None, :] >= rs[:, None]) & (kr[None, :] < rs[:, None] + NA_ROWS)
        ok_c = (kc[None, :] >= cs[:, None]) & (kc[None, :] < cs[:, None] + NA_COLS)
        m = np.where(ok_r & ok_c, 0.0, NEG_INF).astype(np.float32)
        rr = np.full((NA_Q_ROWS, NA_HALO_ROWS), -1, np.int64)
        for a in range(NA_Q_ROWS):
            for b in range(NA_HALO_ROWS):
                r, krow = NA_Q_ROWS * i + a, r0 + b
                if rs[a * GRID_W] <= krow < rs[a * GRID_W] + NA_ROWS:
                    rr[a, b] = krow - r + NA_ROWS - 1
        if cls in seen:
            assert np.array_equal(seen[cls][0], m) and np.array_equal(seen[cls][1], rr)
        seen[cls] = (m, rr)
        masks[cls], ridx[cls] = m, rr
    return masks, ridx


def _swa_static_masks(seq):
    nblk = seq // Q_BLK
    masks = np.zeros((3, Q_BLK, SWA_KEYS), np.float32)
    for cls, n in ((0, 0), (1, 1), (2, nblk - 1)):
        ks = int(np.clip(Q_BLK * (n - 1), 0, seq - SWA_KEYS))
        q = np.arange(Q_BLK) + n * Q_BLK
        k = np.arange(SWA_KEYS) + ks
        masks[cls] = np.where(np.abs(q[:, None] - k[None, :]) <= SWA_WINDOW, 0.0, NEG_INF)
    return masks


def _rope_tables(seq):
    half = ROPE_DIM // 2
    inv_freq = ROPE_THETA ** (-jnp.arange(0, ROPE_DIM, 2, dtype=jnp.float32) / ROPE_DIM)
    ang = jnp.arange(seq, dtype=jnp.int32).astype(jnp.float32)[:, None] * inv_freq[None, :]
    cos, sin = jnp.cos(ang), jnp.sin(ang)
    ones = jnp.ones((seq, HEAD_DIM - ROPE_DIM), jnp.float32)
    zeros = jnp.zeros((seq, HEAD_DIM - ROPE_DIM), jnp.float32)
    zh = jnp.zeros((seq, half), jnp.float32)
    c = jnp.concatenate([cos, cos, ones], axis=1)
    sa = jnp.concatenate([-sin, zh, zeros], axis=1)
    sb = jnp.concatenate([zh, sin, zeros], axis=1)
    rep = LANES // HEAD_DIM
    return jnp.tile(c, (1, rep)), jnp.tile(sa, (1, rep)), jnp.tile(sb, (1, rep))


def _rms(x, g):
    return x * lax.rsqrt(jnp.mean(x * x, axis=-1, keepdims=True) + RMS_EPS) * g


def _rope_tile(a, c, sa, sb):
    return a * c + pltpu.roll(a, LANES - ROPE_DIM // 2, axis=1) * sa + pltpu.roll(a, ROPE_DIM // 2, axis=1) * sb


def _in_proj_kernel(x_ref, g_ref, w_ref, c_ref, sa_ref, sb_ref, o_ref):
    h = _rms(x_ref[...], g_ref[...]).astype(jnp.bfloat16)
    c, sa, sb = c_ref[...], sa_ref[...], sb_ref[...]

    def proj(col, width):
        return jnp.dot(h, w_ref[:, col:col + width], preferred_element_type=jnp.float32)

    kv = proj(COL_KB, 2 * B_KV_WIDTH)
    kb = _rope_tile(kv[:, :B_KV_WIDTH], c, sa, sb)
    vb = kv[:, B_KV_WIDTH:]
    o_ref[:, COL_KB:COL_KB + B_KV_WIDTH] = kb.astype(o_ref.dtype)
    o_ref[:, COL_VB:COL_VB + B_KV_WIDTH] = vb.astype(o_ref.dtype)
    o_ref[:, COL_KB_SW:COL_KB_SW + B_KV_WIDTH] = pltpu.roll(kb, HEAD_DIM, axis=1).astype(o_ref.dtype)
    o_ref[:, COL_VB_SW:COL_VB_SW + B_KV_WIDTH] = pltpu.roll(vb, HEAD_DIM, axis=1).astype(o_ref.dtype)
    qb = proj(COL_QB, B_WIDTH)
    for t in range(B_WIDTH // LANES):
        lo = t * LANES
        tile = _rope_tile(qb[:, lo:lo + LANES], c, sa, sb) * Q_SCALE
        o_ref[:, COL_QB + lo:COL_QB + lo + LANES] = tile.astype(o_ref.dtype)
    o_ref[:, COL_QA:COL_QA + A_WIDTH] = (proj(COL_QA, A_WIDTH) * Q_SCALE).astype(o_ref.dtype)
    o_ref[:, COL_KA:COL_KA + A_WIDTH] = proj(COL_KA, A_WIDTH).astype(o_ref.dtype)
    o_ref[:, COL_VA:COL_VA + A_WIDTH] = proj(COL_VA, A_WIDTH).astype(o_ref.dtype)


def _in_proj(x2d, g, w_bf16, rope, seq):
    n_tok = x2d.shape[0]
    tm = IN_PROJ_ROWS
    pos_blocks = seq // tm
    const = lambda i: (0, 0)
    pos = lambda i: (i % pos_blocks, 0)
    vmem = 2 * tm * D_MODEL * 4 + 2 * D_MODEL * IN_WIDTH * 2 + 2 * tm * PROJ_WIDTH * 2 + 3 * tm * B_WIDTH * 4 * 2
    return pl.pallas_call(
        _in_proj_kernel,
        out_shape=jax.ShapeDtypeStruct((n_tok, PROJ_WIDTH), jnp.bfloat16),
        grid=(n_tok // tm,),
        in_specs=[
            pl.BlockSpec((tm, D_MODEL), lambda i: (i, 0)),
            pl.BlockSpec((1, D_MODEL), const),
            pl.BlockSpec((D_MODEL, IN_WIDTH), const),
            pl.BlockSpec((tm, LANES), pos),
            pl.BlockSpec((tm, LANES), pos),
            pl.BlockSpec((tm, LANES), pos),
        ],
        out_specs=pl.BlockSpec((tm, PROJ_WIDTH), lambda i: (i, 0)),
        compiler_params=pltpu.CompilerParams(
            dimension_semantics=("arbitrary",), vmem_limit_bytes=_vmem_limit(vmem + (8 << 20))),
        name="in_proj",
    )(x2d, g, w_bf16, *rope)


def _build_na_bias(rpb_ref, mask_ref, t_scr, hp, ridx):
    n_rrow, n_rcol = 2 * NA_ROWS - 1, 2 * NA_COLS - 1
    lane = lax.broadcasted_iota(jnp.int32, (GRID_W, LANES), 1)
    qc = lax.broadcasted_iota(jnp.int32, (GRID_W, LANES), 0)
    diff = (lane % GRID_W) - qc + (NA_COLS - 1)
    right = lane >= GRID_W
    used_rows = sorted({int(r) for r in ridx.reshape(-1) if r >= 0})
    for hh in range(2):
        h = 2 * hp + hh
        toeplitz = {}
        for rho in used_rows:
            acc = jnp.zeros((GRID_W, LANES), jnp.float32)
            for d in range(n_rcol):
                acc = jnp.where(diff == d, rpb_ref[(h * n_rrow + rho) * n_rcol + d] * LOG2E, acc)
            toeplitz[rho] = acc
        for cls in range(NA_CLASSES):
            for a in range(NA_Q_ROWS):
                for j in range(NA_HALO_ROWS // 2):
                    r1, r2 = int(ridx[cls, a, 2 * j]), int(ridx[cls, a, 2 * j + 1])
                    zero = jnp.zeros((GRID_W, LANES), jnp.float32)
                    left = toeplitz[r1] if r1 >= 0 else zero
                    rght = toeplitz[r2] if r2 >= 0 else zero
                    blk = jnp.where(right, rght, left)
                    rs, cs = a * GRID_W, j * LANES
                    t_scr[cls, hh, rs:rs + GRID_W, cs:cs + LANES] = blk + mask_ref[cls, rs:rs + GRID_W, cs:cs + LANES]


def _na_kernel(rpb_ref, q_ref, k_ref, v_ref, mask_ref, o_ref, t_scr, *, ridx, seq):
    hp = pl.program_id(0)
    nblk = seq // Q_BLK
    rows = seq // GRID_W

    @pl.when(pl.program_id(1) == 0)
    def _():
        _build_na_bias(rpb_ref, mask_ref, t_scr, hp, ridx)

    lane = lax.broadcasted_iota(jnp.int32, (Q_BLK, LANES), 1)
    left = lane < HEAD_DIM

    def trip(g, carry):
        blocks, scores = [], []
        for u in range(ATTN_UNROLL):
            i = g * ATTN_UNROLL + u
            q0 = pl.multiple_of(i * Q_BLK, Q_BLK)
            r0 = jnp.clip(NA_Q_ROWS * i - NA_ROWS // 2, 0, rows - NA_HALO_ROWS)
            k0 = pl.multiple_of(r0 * GRID_W, GRID_W)
            cls = jnp.where(i < 2, i, jnp.where(i >= nblk - 2, NA_CLASSES - (nblk - i), 2))
            q2 = q_ref[pl.ds(q0, Q_BLK), :]
            k2 = k_ref[pl.ds(k0, NA_KEYS), :]
            zero = jnp.zeros_like(q2)
            blocks.append((q0, k0))
            for hh, qm in enumerate((jnp.where(left, q2, zero), jnp.where(left, zero, q2))):
                s = lax.dot_general(qm, k2, (((1,), (1,)), ((), ())), preferred_element_type=jnp.float32)
                scores.append(s + t_scr[cls, hh])
        probs = []
        for s in scores:
            p = jnp.exp2(s - jnp.max(s, axis=-1, keepdims=True))
            probs.append((p.astype(jnp.bfloat16), jnp.sum(p, axis=-1, keepdims=True)))
        for u, (q0, k0) in enumerate(blocks):
            v2 = v_ref[pl.ds(k0, NA_KEYS), :]
            outs = [jnp.dot(p, v2, preferred_element_type=jnp.float32) / den for p, den in probs[2 * u:2 * u + 2]]
            o_ref[pl.ds(q0, Q_BLK), :] = jnp.where(left, outs[0], outs[1]).astype(o_ref.dtype)
        return carry

    lax.fori_loop(0, nblk // ATTN_UNROLL, trip, 0)


def _na_attn(proj, rpb_flat, seq, batch):
    masks, ridx = _na_static_tables(seq)
    n_tok = proj.shape[0]
    n_pairs = A_WIDTH // LANES
    col = lambda base: (lambda hp, b: (b, base // LANES + hp))
    vmem = 2 * 4 * seq * LANES * 2 + 2 * masks.size * 4 + NA_CLASSES * 2 * Q_BLK * NA_KEYS * 4
    return pl.pallas_call(
        functools.partial(_na_kernel, ridx=ridx, seq=seq),
        out_shape=jax.ShapeDtypeStruct((n_tok, A_WIDTH), jnp.bfloat16),
        grid=(n_pairs, batch),
        in_specs=[
            pl.BlockSpec(memory_space=pltpu.SMEM),
            pl.BlockSpec((seq, LANES), col(COL_QA)),
            pl.BlockSpec((seq, LANES), col(COL_KA)),
            pl.BlockSpec((seq, LANES), col(COL_VA)),
            pl.BlockSpec(masks.shape, lambda hp, b: (0, 0, 0)),
        ],
        out_specs=pl.BlockSpec((seq, LANES), lambda hp, b: (b, hp)),
        scratch_shapes=[pltpu.VMEM((NA_CLASSES, 2, Q_BLK, NA_KEYS), jnp.float32)],
        compiler_params=pltpu.CompilerParams(
            dimension_semantics=("arbitrary", "arbitrary"), vmem_limit_bytes=_vmem_limit(vmem + (16 << 20))),
        name="na_attn",
    )(rpb_flat, proj, proj, proj, jnp.asarray(masks))


def _swa_kernel(sink_ref, q_ref, ka_ref, kb_ref, va_ref, vb_ref, mask_ref, o_ref, *, seq):
    t = pl.program_id(1)
    nblk = seq // Q_BLK
    lane = lax.broadcasted_iota(jnp.int32, (Q_BLK, LANES), 1)
    left = lane < HEAD_DIM
    sinks = (sink_ref[2 * t] * LOG2E, sink_ref[2 * t + 1] * LOG2E)

    def trip(g, carry):
        blocks, scores = [], []
        for u in range(ATTN_UNROLL):
            n = g * ATTN_UNROLL + u
            q0 = pl.multiple_of(n * Q_BLK, Q_BLK)
            k0 = pl.multiple_of(jnp.clip(Q_BLK * (n - 1), 0, seq - SWA_KEYS), Q_BLK)
            cls = jnp.where(n == 0, 0, jnp.where(n == nblk - 1, 2, 1))
            q2 = q_ref[pl.ds(q0, Q_BLK), :]
            zero = jnp.zeros_like(q2)
            mask = mask_ref[cls]
            blocks.append((q0, k0))
            for qm, k_ref in ((jnp.where(left, q2, zero), ka_ref), (jnp.where(left, zero, q2), kb_ref)):
                k2 = k_ref[pl.ds(k0, SWA_KEYS), :]
                s = lax.dot_general(qm, k2, (((1,), (1,)), ((), ())), preferred_element_type=jnp.float32)
                scores.append(s + mask)
        probs = []
        for c, s in enumerate(scores):
            sink = sinks[c % 2]
            m = jnp.maximum(jnp.max(s, axis=-1, keepdims=True), sink)
            p = jnp.exp2(s - m)
            probs.append((p.astype(jnp.bfloat16), jnp.sum(p, axis=-1, keepdims=True) + jnp.exp2(sink - m)))
        for u, (q0, k0) in enumerate(blocks):
            outs = []
            for (p, den), v_ref in zip(probs[2 * u:2 * u + 2], (va_ref, vb_ref)):
                v2 = v_ref[pl.ds(k0, SWA_KEYS), :]
                outs.append(jnp.dot(p, v2, preferred_element_type=jnp.float32) / den)
            o_ref[pl.ds(q0, Q_BLK), :] = jnp.where(left, outs[0], outs[1]).astype(o_ref.dtype)
        return carry

    lax.fori_loop(0, nblk // ATTN_UNROLL, trip, 0)


def _swa_attn(proj, sink, seq, batch):
    masks = _swa_static_masks(seq)
    n_tok = proj.shape[0]
    n_tiles = B_WIDTH // LANES
    tiles_per_kv = n_tiles // SWA_KV_HEADS
    kb, vb, kbs, vbs = (c // LANES for c in (COL_KB, COL_VB, COL_KB_SW, COL_VB_SW))
    left_kv = lambda plain, swapped: (lambda b, t: (b, plain + (swapped - plain) * (t // tiles_per_kv)))
    right_kv = lambda plain, swapped: (lambda b, t: (b, swapped + (plain - swapped) * (t // tiles_per_kv)))
    vmem = 2 * 6 * seq * LANES * 2 + 2 * masks.size * 4
    return pl.pallas_call(
        functools.partial(_swa_kernel, seq=seq),
        out_shape=jax.ShapeDtypeStruct((n_tok, B_WIDTH), jnp.bfloat16),
        grid=(batch, n_tiles),
        in_specs=[
            pl.BlockSpec(memory_space=pltpu.SMEM),
            pl.BlockSpec((seq, LANES), lambda b, t: (b, COL_QB // LANES + t)),
            pl.BlockSpec((seq, LANES), left_kv(kb, kbs)),
            pl.BlockSpec((seq, LANES), right_kv(kb, kbs)),
            pl.BlockSpec((seq, LANES), left_kv(vb, vbs)),
            pl.BlockSpec((seq, LANES), right_kv(vb, vbs)),
            pl.BlockSpec(masks.shape, lambda b, t: (0, 0, 0)),
        ],
        out_specs=pl.BlockSpec((seq, LANES), lambda b, t: (b, t)),
        compiler_params=pltpu.CompilerParams(
            dimension_semantics=("arbitrary", "arbitrary"), vmem_limit_bytes=_vmem_limit(vmem + (16 << 20))),
        name="swa_attn",
    )(sink, proj, proj, proj, proj, proj, jnp.asarray(masks))


def _out_ffn_kernel(x_ref, oa_ref, ob_ref, gna_ref, gswa_ref, wo_ref, gffn_ref, wg_ref, wu_ref, wd_ref,
                    gfin_ref, o_ref, *, ff_chunk, sub_rows):
    d_ff = wg_ref.shape[1]
    row_slices = [slice(r0, r0 + sub_rows) for r0 in range(0, x_ref.shape[0], sub_rows)]
    mixed = []
    for rows in row_slices:
        na = _rms(oa_ref[rows, :].astype(jnp.float32), gna_ref[...]).astype(jnp.bfloat16)
        nb = _rms(ob_ref[rows, :].astype(jnp.float32), gswa_ref[...]).astype(jnp.bfloat16)
        x1 = (x_ref[rows, :]
              + jnp.dot(na, wo_ref[:A_WIDTH, :], preferred_element_type=jnp.float32)
              + jnp.dot(nb, wo_ref[A_WIDTH:, :], preferred_element_type=jnp.float32))
        mixed.append((x1, _rms(x1, gffn_ref[...]).astype(jnp.bfloat16)))
    for rows, (x1, h) in zip(row_slices, mixed):
        acc = jnp.zeros(x1.shape, jnp.float32)
        for c0 in range(0, d_ff, ff_chunk):
            gate = jnp.dot(h, wg_ref[:, c0:c0 + ff_chunk], preferred_element_type=jnp.float32)
            up = jnp.dot(h, wu_ref[:, c0:c0 + ff_chunk], preferred_element_type=jnp.float32)
            act = (jax.nn.silu(gate) * up).astype(jnp.bfloat16)
            acc = acc + jnp.dot(act, wd_ref[c0:c0 + ff_chunk, :], preferred_element_type=jnp.float32)
        o_ref[rows, :] = _rms(x1 + acc, gfin_ref[...])


def _out_ffn(x2d, o_na, o_swa, g_na, g_swa, wo, g_ffn, wg, wu, wd, g_fin):
    n_tok = x2d.shape[0]
    tm, sub_rows, ff_chunk = FFN_ROWS, FFN_SUB_ROWS, FFN_CHUNK
    d_ff = wg.shape[1]
    const = lambda i: (0, 0)
    row = lambda i: (i, 0)
    resident = lambda shape: pl.BlockSpec(shape, const, pipeline_mode=pl.Buffered(1))
    weights = (D_MODEL * D_MODEL + 3 * D_MODEL * d_ff) * 2
    tiles = 2 * 2 * tm * D_MODEL * 4 + 2 * 2 * tm * A_WIDTH * 2
    vmem = weights + tiles + 2 * (6 * sub_rows * D_MODEL * 4 + 4 * sub_rows * ff_chunk * 4)
    return pl.pallas_call(
        functools.partial(_out_ffn_kernel, ff_chunk=ff_chunk, sub_rows=sub_rows),
        out_shape=jax.ShapeDtypeStruct((n_tok, D_MODEL), jnp.float32),
        grid=(n_tok // tm,),
        in_specs=[
            pl.BlockSpec((tm, D_MODEL), row),
            pl.BlockSpec((tm, A_WIDTH), row),
            pl.BlockSpec((tm, B_WIDTH), row),
            pl.BlockSpec((1, A_WIDTH), const),
            pl.BlockSpec((1, B_WIDTH), const),
            resident((D_MODEL, D_MODEL)),
            pl.BlockSpec((1, D_MODEL), const),
            resident((D_MODEL, d_ff)),
            resident((D_MODEL, d_ff)),
            resident((d_ff, D_MODEL)),
            pl.BlockSpec((1, D_MODEL), const),
        ],
        out_specs=pl.BlockSpec((tm, D_MODEL), row),
        compiler_params=pltpu.CompilerParams(
            dimension_semantics=("arbitrary",), vmem_limit_bytes=_vmem_limit(vmem + (8 << 20))),
        name="out_ffn",
    )(x2d, o_na, o_swa, g_na, g_swa, wo, g_ffn, wg, wu, wd, g_fin)


def kernel(x, g_norm_mix, w_in, na_rpb, swa_sink, g_out_na, g_out_swa, w_out, g_norm_ffn, w_gate, w_up, w_down,
           g_final):
    batch, seq, d_model = x.shape
    assert w_in.shape[0] == 1, "the final RMSNorm is fused into the single layer's FFN call"
    assert d_model == D_MODEL and seq % (Q_BLK * ATTN_UNROLL) == 0 and seq // GRID_W >= NA_HALO_ROWS and seq >= SWA_KEYS
    bf16 = jnp.bfloat16
    rope = _rope_tables(seq)
    x2d = x.reshape(batch * seq, d_model)
    proj = _in_proj(x2d, g_norm_mix[0][None, :], w_in[0].astype(bf16), rope, seq)
    o_na = _na_attn(proj, na_rpb[0].reshape(-1), seq, batch)
    o_swa = _swa_attn(proj, swa_sink[0], seq, batch)
    out = _out_ffn(x2d, o_na, o_swa, g_out_na[0][None, :], g_out_swa[0][None, :], w_out[0].astype(bf16),
                   g_norm_ffn[0][None, :], w_gate[0].astype(bf16), w_up[0].astype(bf16),
                   w_down[0].astype(bf16), g_final[None, :])
    return out.reshape(batch, seq, d_model)
```

```python
import functools

import numpy as np
import jax
import jax.numpy as jnp
from jax import lax
from jax.experimental import pallas as pl
from jax.experimental.pallas import tpu as pltpu

D_MODEL = 1024
HEAD_DIM = 64
NA_HEADS = 8
SWA_HEADS = 8
SWA_KV_HEADS = 2
A_WIDTH = NA_HEADS * HEAD_DIM
B_WIDTH = SWA_HEADS * HEAD_DIM
B_KV_WIDTH = SWA_KV_HEADS * HEAD_DIM
IN_WIDTH = 3 * A_WIDTH + B_WIDTH + 2 * B_KV_WIDTH
GRID_W = 64
NA_ROWS = 8
NA_COLS = 16
SWA_WINDOW = 128
ROPE_THETA = 500000.0
ROPE_DIM = HEAD_DIM // 4
RMS_EPS = 1e-6
NEG_INF = -1e30
LOG2E = 1.4426950408889634
Q_SCALE = HEAD_DIM ** -0.5 * LOG2E

LANES = 128
VMEM_CAPACITY = 64 * 1024 * 1024

PROJ_WIDTH = 3 * A_WIDTH + B_WIDTH + 2 * SWA_KV_HEADS * LANES
COL_QA, COL_KA, COL_VA = 0, A_WIDTH, 2 * A_WIDTH
COL_QB = 3 * A_WIDTH
COL_KD = COL_QB + B_WIDTH
COL_VD = COL_KD + SWA_KV_HEADS * LANES
W_COL_KB = COL_QB + B_WIDTH

Q_BLK = 128
NA_Q_ROWS = Q_BLK // GRID_W
NA_HALO_ROWS = NA_ROWS + NA_Q_ROWS
NA_KEYS = NA_HALO_ROWS * GRID_W
SWA_KEYS = 3 * Q_BLK
NA_CLASSES = 5
ATTN_UNROLL = 16
ATTN_SKEW = 3
IN_PROJ_ROWS = 512
FFN_ROWS = 1024
FFN_SUB_ROWS = 512
FFN_CHUNK = 256


def _vmem_limit(nbytes):
    return int(min(VMEM_CAPACITY - (2 << 20), max(nbytes, 16 << 20)))


def _na_class_of_block(i, nblk):
    if i < 2:
        return i
    if i >= nblk - 2:
        return NA_CLASSES - (nblk - i)
    return 2


def _na_key_row0(i, rows):
    return int(np.clip(NA_Q_ROWS * i - NA_ROWS // 2, 0, rows - NA_HALO_ROWS))


def _na_static_tables(seq):
    rows = seq // GRID_W
    nblk = seq // Q_BLK
    masks = np.zeros((NA_CLASSES, Q_BLK, NA_KEYS), np.float32)
    ridx = np.full((NA_CLASSES, NA_Q_ROWS, NA_HALO_ROWS), -1, np.int64)
    seen = {}
    for i in range(nblk):
        cls = _na_class_of_block(i, nblk)
        r0 = _na_key_row0(i, rows)
        q = np.arange(Q_BLK) + i * Q_BLK
        qr, qc = q // GRID_W, q % GRID_W
        k = np.arange(NA_KEYS) + r0 * GRID_W
        kr, kc = k // GRID_W, k % GRID_W
        rs = np.clip(qr - NA_ROWS // 2, 0, rows - NA_ROWS)
        cs = np.clip(qc - NA_COLS // 2, 0, GRID_W - NA_COLS)
        ok_r = (kr[None, :] >= rs[:, None]) & (kr[None, :] < rs[:, None] + NA_ROWS)
        ok_c = (kc[None, :] >= cs[:, None]) & (kc[None, :] < cs[:, None] + NA_COLS)
        m = np.where(ok_r & ok_c, 0.0, NEG_INF).astype(np.float32)
        rr = np.full((NA_Q_ROWS, NA_HALO_ROWS), -1, np.int64)
        for a in range(NA_Q_ROWS):
            for b in range(NA_HALO_ROWS):
                r, krow = NA_Q_ROWS * i + a, r0 + b
                if rs[a * GRID_W] <= krow < rs[a * GRID_W] + NA_ROWS:
                    rr[a, b] = krow - r + NA_ROWS - 1
        if cls in seen:
            assert np.array_equal(seen[cls][0], m) and np.array_equal(seen[cls][1], rr)
        seen[cls] = (m, rr)
        masks[cls], ridx[cls] = m, rr
    return masks, ridx


def _swa_static_masks(seq):
    nblk = seq // Q_BLK
    masks = np.zeros((3, Q_BLK, SWA_KEYS), np.float32)
    for cls, n in ((0, 0), (1, 1), (2, nblk - 1)):
        ks = int(np.clip(Q_BLK * (n - 1), 0, seq - SWA_KEYS))
        q = np.arange(Q_BLK) + n * Q_BLK
        k = np.arange(SWA_KEYS) + ks
        masks[cls] = np.where(np.abs(q[:, None] - k[None, :]) <= SWA_WINDOW, 0.0, NEG_INF)
    return masks


def _rope_tables(seq):
    half = ROPE_DIM // 2
    inv_freq = ROPE_THETA ** (-jnp.arange(0, ROPE_DIM, 2, dtype=jnp.float32) / ROPE_DIM)
    ang = jnp.arange(seq, dtype=jnp.int32).astype(jnp.float32)[:, None] * inv_freq[None, :]
    cos, sin = jnp.cos(ang), jnp.sin(ang)
    ones = jnp.ones((seq, HEAD_DIM - ROPE_DIM), jnp.float32)
    zeros = jnp.zeros((seq, HEAD_DIM - ROPE_DIM), jnp.float32)
    zh = jnp.zeros((seq, half), jnp.float32)
    c = jnp.concatenate([cos, cos, ones], axis=1)
    sa = jnp.concatenate([-sin, zh, zeros], axis=1)
    sb = jnp.concatenate([zh, sin, zeros], axis=1)
    rep = LANES // HEAD_DIM
    return jnp.tile(c, (1, rep)), jnp.tile(sa, (1, rep)), jnp.tile(sb, (1, rep))


def _rms(x, g):
    return x * lax.rsqrt(jnp.mean(x * x, axis=-1, keepdims=True) + RMS_EPS) * g


def _rope_tile(a, c, sa, sb):
    return a * c + pltpu.roll(a, LANES - ROPE_DIM // 2, axis=1) * sa + pltpu.roll(a, ROPE_DIM // 2, axis=1) * sb


def _in_proj_kernel(x_ref, g_ref, w_ref, c_ref, sa_ref, sb_ref, o_ref):
    h = _rms(x_ref[...], g_ref[...]).astype(jnp.bfloat16)
    c, sa, sb = c_ref[...], sa_ref[...], sb_ref[...]

    def proj(col, width):
        return jnp.dot(h, w_ref[:, col:col + width], preferred_element_type=jnp.float32)

    kv = proj(W_COL_KB, 2 * B_KV_WIDTH)
    left = lax.broadcasted_iota(jnp.int32, (h.shape[0], LANES), 1) < HEAD_DIM
    for col, tile in ((COL_KD, _rope_tile(kv[:, :B_KV_WIDTH], c, sa, sb)), (COL_VD, kv[:, B_KV_WIDTH:])):
        swapped = pltpu.roll(tile, HEAD_DIM, axis=1)
        o_ref[:, col:col + LANES] = jnp.where(left, tile, swapped).astype(o_ref.dtype)
        o_ref[:, col + LANES:col + 2 * LANES] = jnp.where(left, swapped, tile).astype(o_ref.dtype)
    qb = proj(COL_QB, B_WIDTH)
    for t in range(B_WIDTH // LANES):
        lo = t * LANES
        tile = _rope_tile(qb[:, lo:lo + LANES], c, sa, sb) * Q_SCALE
        o_ref[:, COL_QB + lo:COL_QB + lo + LANES] = tile.astype(o_ref.dtype)
    o_ref[:, COL_QA:COL_QA + A_WIDTH] = (proj(COL_QA, A_WIDTH) * Q_SCALE).astype(o_ref.dtype)
    o_ref[:, COL_KA:COL_KA + A_WIDTH] = proj(COL_KA, A_WIDTH).astype(o_ref.dtype)
    o_ref[:, COL_VA:COL_VA + A_WIDTH] = proj(COL_VA, A_WIDTH).astype(o_ref.dtype)


def _in_proj(x2d, g, w_bf16, rope, seq):
    n_tok = x2d.shape[0]
    tm = IN_PROJ_ROWS
    pos_blocks = seq // tm
    const = lambda i: (0, 0)
    pos = lambda i: (i % pos_blocks, 0)
    vmem = 2 * tm * D_MODEL * 4 + 2 * D_MODEL * IN_WIDTH * 2 + 2 * tm * PROJ_WIDTH * 2 + 3 * tm * B_WIDTH * 4 * 2
    return pl.pallas_call(
        _in_proj_kernel,
        out_shape=jax.ShapeDtypeStruct((n_tok, PROJ_WIDTH), jnp.bfloat16),
        grid=(n_tok // tm,),
        in_specs=[
            pl.BlockSpec((tm, D_MODEL), lambda i: (i, 0)),
            pl.BlockSpec((1, D_MODEL), const),
            pl.BlockSpec((D_MODEL, IN_WIDTH), const),
            pl.BlockSpec((tm, LANES), pos),
            pl.BlockSpec((tm, LANES), pos),
            pl.BlockSpec((tm, LANES), pos),
        ],
        out_specs=pl.BlockSpec((tm, PROJ_WIDTH), lambda i: (i, 0)),
        compiler_params=pltpu.CompilerParams(
            dimension_semantics=("arbitrary",), vmem_limit_bytes=_vmem_limit(vmem + (8 << 20))),
        name="in_proj",
    )(x2d, g, w_bf16, *rope)


def _transpose_keys(v_ref, vt_ref):
    for j in range(v_ref.shape[0] // LANES):
        vt_ref[j] = v_ref[j * LANES:(j + 1) * LANES, :].T


def _attend(q_ref, k_ref, vt_ref, o_ref, blocks, sink_row=None):
    lane = lax.broadcasted_iota(jnp.int32, (Q_BLK, LANES), 1)
    left = lane < HEAD_DIM
    top = lax.broadcasted_iota(jnp.int32, (LANES, Q_BLK), 0) < HEAD_DIM

    def scores_of(q0, kt0, table):
        q2 = q_ref[pl.ds(q0, Q_BLK), :]
        zero = jnp.zeros_like(q2)
        qq = jnp.concatenate([jnp.where(left, q2, zero), jnp.where(left, zero, q2)], axis=0)
        k2 = k_ref[pl.ds(pl.multiple_of(kt0 * LANES, LANES), table.shape[0]), :]
        st = lax.dot_general(k2, qq, (((1,), (1,)), ((), ())), preferred_element_type=jnp.float32)
        return st + table

    def finish(q0, kt0, table, st):
        m = jnp.max(st, axis=0, keepdims=True)
        if sink_row is not None:
            m = jnp.maximum(m, sink_row)
        p = jnp.exp2(st - m)
        den = jnp.sum(p, axis=0, keepdims=True)
        if sink_row is not None:
            den = den + jnp.exp2(sink_row - m)
        vt = jnp.concatenate([vt_ref[kt0 + j] for j in range(table.shape[0] // LANES)], axis=1)
        ot = jnp.dot(vt, p.astype(jnp.bfloat16), preferred_element_type=jnp.float32) / den
        x = jnp.where(top, ot[:, :Q_BLK], ot[:, Q_BLK:])
        o_ref[pl.ds(q0, Q_BLK), :] = x.T.astype(o_ref.dtype)

    pending = [scores_of(*blk) for blk in blocks[:ATTN_SKEW]]
    for c, blk in enumerate(blocks):
        st = pending.pop(0)
        if c + ATTN_SKEW < len(blocks):
            pending.append(scores_of(*blocks[c + ATTN_SKEW]))
        finish(*blk, st)


def _build_na_bias(rpb_ref, mask_ref, t_scr, hp, ridx):
    n_rrow, n_rcol = 2 * NA_ROWS - 1, 2 * NA_COLS - 1
    kc = lax.broadcasted_iota(jnp.int32, (GRID_W, LANES), 0)
    lane = lax.broadcasted_iota(jnp.int32, (GRID_W, LANES), 1)
    diff = kc - (lane % GRID_W) + (NA_COLS - 1)
    second_row = lane >= GRID_W
    used_rows = sorted({int(r) for r in ridx.reshape(-1) if r >= 0})
    zero = jnp.zeros((GRID_W, LANES), jnp.float32)
    for hh in range(2):
        h = 2 * hp + hh
        toeplitz = {-1: zero}
        for rho in used_rows:
            acc = zero
            for d in range(n_rcol):
                acc = jnp.where(diff == d, rpb_ref[(h * n_rrow + rho) * n_rcol + d] * LOG2E, acc)
            toeplitz[rho] = acc
        for cls in range(NA_CLASSES):
            for b in range(NA_HALO_ROWS):
                blk = jnp.where(second_row, toeplitz[int(ridx[cls, 1, b])], toeplitz[int(ridx[cls, 0, b])])
                ks = slice(b * GRID_W, (b + 1) * GRID_W)
                t_scr[cls, ks, hh * Q_BLK:(hh + 1) * Q_BLK] = blk + mask_ref[cls, ks, :]


def _na_kernel(rpb_ref, q_ref, k_ref, v_ref, mask_ref, o_ref, t_scr, vt_scr, *, ridx, seq):
    nblk = seq // Q_BLK
    max_tile0 = (seq - NA_KEYS) // LANES

    @pl.when(pl.program_id(1) == 0)
    def _():
        _build_na_bias(rpb_ref, mask_ref, t_scr, pl.program_id(0), ridx)

    _transpose_keys(v_ref, vt_scr)

    def trip(g, carry):
        blocks = []
        for u in range(ATTN_UNROLL):
            i = g * ATTN_UNROLL + u
            cls = jnp.where(i < 2, i, jnp.where(i >= nblk - 2, NA_CLASSES - (nblk - i), 2))
            kt0 = jnp.clip(i - NA_ROWS // (2 * NA_Q_ROWS), 0, max_tile0)
            blocks.append((pl.multiple_of(i * Q_BLK, Q_BLK), kt0, t_scr[cls]))
        _attend(q_ref, k_ref, vt_scr, o_ref, blocks)
        return carry

    lax.fori_loop(0, nblk // ATTN_UNROLL, trip, 0)


def _na_attn(proj, rpb_flat, seq, batch):
    masks, ridx = _na_static_tables(seq)
    masks_t = np.ascontiguousarray(masks.transpose(0, 2, 1))
    n_tok = proj.shape[0]
    n_pairs = A_WIDTH // LANES
    col = lambda base: (lambda hp, b: (b, base // LANES + hp))
    table_bytes = NA_CLASSES * NA_KEYS * 2 * Q_BLK * 4
    vmem = 2 * 4 * seq * LANES * 2 + 2 * masks_t.size * 4 + table_bytes + seq * LANES * 2
    return pl.pallas_call(
        functools.partial(_na_kernel, ridx=ridx, seq=seq),
        out_shape=jax.ShapeDtypeStruct((n_tok, A_WIDTH), jnp.bfloat16),
        grid=(n_pairs, batch),
        in_specs=[
            pl.BlockSpec(memory_space=pltpu.SMEM),
            pl.BlockSpec((seq, LANES), col(COL_QA)),
            pl.BlockSpec((seq, LANES), col(COL_KA)),
            pl.BlockSpec((seq, LANES), col(COL_VA)),
            pl.BlockSpec(masks_t.shape, lambda hp, b: (0, 0, 0)),
        ],
        out_specs=pl.BlockSpec((seq, LANES), lambda hp, b: (b, hp)),
        scratch_shapes=[pltpu.VMEM((NA_CLASSES, NA_KEYS, 2 * Q_BLK), jnp.float32),
                        pltpu.VMEM((seq // LANES, LANES, LANES), jnp.bfloat16)],
        compiler_params=pltpu.CompilerParams(
            dimension_semantics=("arbitrary", "arbitrary"), vmem_limit_bytes=_vmem_limit(vmem + (16 << 20))),
        name="na_attn",
    )(rpb_flat, proj, proj, proj, jnp.asarray(masks_t))


def _swa_kernel(sink_ref, q_ref, k_ref, v_ref, mask_ref, o_ref, vt_scr, *, seq):
    t = pl.program_id(1)
    nblk = seq // Q_BLK
    max_tile0 = (seq - SWA_KEYS) // LANES
    tiles_per_kv = (B_WIDTH // LANES) // SWA_KV_HEADS

    @pl.when(t % tiles_per_kv == 0)
    def _():
        _transpose_keys(v_ref, vt_scr)

    lane = lax.broadcasted_iota(jnp.int32, (1, 2 * Q_BLK), 1)
    sink_row = jnp.where(lane < Q_BLK, sink_ref[2 * t], sink_ref[2 * t + 1]) * LOG2E

    def trip(g, carry):
        blocks = []
        for u in range(ATTN_UNROLL):
            n = g * ATTN_UNROLL + u
            cls = jnp.where(n == 0, 0, jnp.where(n == nblk - 1, 2, 1))
            blocks.append((pl.multiple_of(n * Q_BLK, Q_BLK), jnp.clip(n - 1, 0, max_tile0), mask_ref[cls]))
        _attend(q_ref, k_ref, vt_scr, o_ref, blocks, sink_row)
        return carry

    lax.fori_loop(0, nblk // ATTN_UNROLL, trip, 0)


def _swa_attn(proj, sink, seq, batch):
    masks = _swa_static_masks(seq)
    masks_t = np.ascontiguousarray(np.concatenate([masks.transpose(0, 2, 1)] * 2, axis=2))
    n_tok = proj.shape[0]
    n_tiles = B_WIDTH // LANES
    tiles_per_kv = n_tiles // SWA_KV_HEADS
    kv_tile = lambda base: (lambda b, t: (b, base // LANES + t // tiles_per_kv))
    vmem = 2 * 4 * seq * LANES * 2 + 2 * masks_t.size * 4 + seq * LANES * 2
    return pl.pallas_call(
        functools.partial(_swa_kernel, seq=seq),
        out_shape=jax.ShapeDtypeStruct((n_tok, B_WIDTH), jnp.bfloat16),
        grid=(batch, n_tiles),
        in_specs=[
            pl.BlockSpec(memory_space=pltpu.SMEM),
            pl.BlockSpec((seq, LANES), lambda b, t: (b, COL_QB // LANES + t)),
            pl.BlockSpec((seq, LANES), kv_tile(COL_KD)),
            pl.BlockSpec((seq, LANES), kv_tile(COL_VD)),
            pl.BlockSpec(masks_t.shape, lambda b, t: (0, 0, 0)),
        ],
        out_specs=pl.BlockSpec((seq, LANES), lambda b, t: (b, t)),
        scratch_shapes=[pltpu.VMEM((seq // LANES, LANES, LANES), jnp.bfloat16)],
        compiler_params=pltpu.CompilerParams(
            dimension_semantics=("arbitrary", "arbitrary"), vmem_limit_bytes=_vmem_limit(vmem + (16 << 20))),
        name="swa_attn",
    )(sink, proj, proj, proj, jnp.asarray(masks_t))


def _out_ffn_kernel(x_ref, oa_ref, ob_ref, gna_ref, gswa_ref, wo_ref, gffn_ref, wg_ref, wu_ref, wd_ref,
                    gfin_ref, o_ref, *, ff_chunk, sub_rows):
    d_ff = wg_ref.shape[1]
    row_slices = [slice(r0, r0 + sub_rows) for r0 in range(0, x_ref.shape[0], sub_rows)]
    mixed = []
    for rows in row_slices:
        na = _rms(oa_ref[rows, :].astype(jnp.float32), gna_ref[...]).astype(jnp.bfloat16)
        nb = _rms(ob_ref[rows, :].astype(jnp.float32), gswa_ref[...]).astype(jnp.bfloat16)
        x1 = (x_ref[rows, :]
              + jnp.dot(na, wo_ref[:A_WIDTH, :], preferred_element_type=jnp.float32)
              + jnp.dot(nb, wo_ref[A_WIDTH:, :], preferred_element_type=jnp.float32))
        mixed.append((x1, _rms(x1, gffn_ref[...]).astype(jnp.bfloat16)))
    for rows, (x1, h) in zip(row_slices, mixed):
        acc = jnp.zeros(x1.shape, jnp.float32)
        for c0 in range(0, d_ff, ff_chunk):
            gate = jnp.dot(h, wg_ref[:, c0:c0 + ff_chunk], preferred_element_type=jnp.float32)
            up = jnp.dot(h, wu_ref[:, c0:c0 + ff_chunk], preferred_element_type=jnp.float32)
            act = (jax.nn.silu(gate) * up).astype(jnp.bfloat16)
            acc = acc + jnp.dot(act, wd_ref[c0:c0 + ff_chunk, :], preferred_element_type=jnp.float32)
        o_ref[rows, :] = _rms(x1 + acc, gfin_ref[...])


def _out_ffn(x2d, o_na, o_swa, g_na, g_swa, wo, g_ffn, wg, wu, wd, g_fin):
    n_tok = x2d.shape[0]
    tm, sub_rows, ff_chunk = FFN_ROWS, FFN_SUB_ROWS, FFN_CHUNK
    d_ff = wg.shape[1]
    const = lambda i: (0, 0)
    row = lambda i: (i, 0)
    resident = lambda shape: pl.BlockSpec(shape, const, pipeline_mode=pl.Buffered(1))
    weights = (D_MODEL * D_MODEL + 3 * D_MODEL * d_ff) * 2
    tiles = 2 * 2 * tm * D_MODEL * 4 + 2 * 2 * tm * A_WIDTH * 2
    vmem = weights + tiles + 2 * (6 * sub_rows * D_MODEL * 4 + 4 * sub_rows * ff_chunk * 4)
    return pl.pallas_call(
        functools.partial(_out_ffn_kernel, ff_chunk=ff_chunk, sub_rows=sub_rows),
        out_shape=jax.ShapeDtypeStruct((n_tok, D_MODEL), jnp.float32),
        grid=(n_tok // tm,),
        in_specs=[
            pl.BlockSpec((tm, D_MODEL), row),
            pl.BlockSpec((tm, A_WIDTH), row),
            pl.BlockSpec((tm, B_WIDTH), row),
            pl.BlockSpec((1, A_WIDTH), const),
            pl.BlockSpec((1, B_WIDTH), const),
            resident((D_MODEL, D_MODEL)),
            pl.BlockSpec((1, D_MODEL), const),
            resident((D_MODEL, d_ff)),
            resident((D_MODEL, d_ff)),
            resident((d_ff, D_MODEL)),
            pl.BlockSpec((1, D_MODEL), const),
        ],
        out_specs=pl.BlockSpec((tm, D_MODEL), row),
        compiler_params=pltpu.CompilerParams(
            dimension_semantics=("arbitrary",), vmem_limit_bytes=_vmem_limit(vmem + (8 << 20))),
        name="out_ffn",
    )(x2d, o_na, o_swa, g_na, g_swa, wo, g_ffn, wg, wu, wd, g_fin)


def kernel(x, g_norm_mix, w_in, na_rpb, swa_sink, g_out_na, g_out_swa, w_out, g_norm_ffn, w_gate, w_up, w_down,
           g_final):
    batch, seq, d_model = x.shape
    assert w_in.shape[0] == 1, "the final RMSNorm is fused into the single layer's FFN call"
    assert d_model == D_MODEL and seq % (Q_BLK * ATTN_UNROLL) == 0 and seq // GRID_W >= NA_HALO_ROWS and seq >= SWA_KEYS
    assert B_KV_WIDTH == LANES, "the windowed k (and v) heads together fill one lane tile"
    bf16 = jnp.bfloat16
    rope = _rope_tables(seq)
    x2d = x.reshape(batch * seq, d_model)
    proj = _in_proj(x2d, g_norm_mix[0][None, :], w_in[0].astype(bf16), rope, seq)
    o_na = _na_attn(proj, na_rpb[0].reshape(-1), seq, batch)
    o_swa = _swa_attn(proj, swa_sink[0], seq, batch)
    out = _out_ffn(x2d, o_na, o_swa, g_out_na[0][None, :], g_out_swa[0][None, :], w_out[0].astype(bf16),
                   g_norm_ffn[0][None, :], w_gate[0].astype(bf16), w_up[0].astype(bf16),
                   w_down[0].astype(bf16), g_final[None, :])
    return out.reshape(batch, seq, d_model)
```

```python
import functools

import numpy as np
import jax
import jax.numpy as jnp
from jax import lax
from jax.experimental import pallas as pl
from jax.experimental.pallas import tpu as pltpu

D_MODEL = 1024
HEAD_DIM = 64
NA_HEADS = 8
SWA_HEADS = 8
SWA_KV_HEADS = 2
A_WIDTH = NA_HEADS * HEAD_DIM
B_WIDTH = SWA_HEADS * HEAD_DIM
B_KV_WIDTH = SWA_KV_HEADS * HEAD_DIM
IN_WIDTH = 3 * A_WIDTH + B_WIDTH + 2 * B_KV_WIDTH
GRID_W = 64
NA_ROWS = 8
NA_COLS = 16
SWA_WINDOW = 128
ROPE_THETA = 500000.0
ROPE_DIM = HEAD_DIM // 4
RMS_EPS = 1e-6
NEG_INF = -1e30
LOG2E = 1.4426950408889634
Q_SCALE = HEAD_DIM ** -0.5 * LOG2E

LANES = 128
VMEM_CAPACITY = 64 * 1024 * 1024

PROJ_WIDTH = 3 * A_WIDTH + B_WIDTH + 2 * SWA_KV_HEADS * LANES
COL_QA, COL_KA, COL_VA = 0, A_WIDTH, 2 * A_WIDTH
COL_QB = 3 * A_WIDTH
COL_KD = COL_QB + B_WIDTH
COL_VD = COL_KD + SWA_KV_HEADS * LANES
W_COL_KB = COL_QB + B_WIDTH

Q_BLK = 128
NA_Q_ROWS = Q_BLK // GRID_W
NA_MAX_KEY_ROWS = NA_ROWS + NA_Q_ROWS - 1
NA_MAX_KEYS = NA_MAX_KEY_ROWS * GRID_W
NA_CLASSES = 5
ATTN_SKEW = 3
IN_PROJ_ROWS = 512
FFN_ROWS = 1024
FFN_SUB_ROWS = 512
FFN_CHUNK = 256


def _vmem_limit(nbytes):
    return int(min(VMEM_CAPACITY - (2 << 20), max(nbytes, 16 << 20)))


def _na_class_of_block(i, nblk):
    if i < 2:
        return i
    if i >= nblk - 2:
        return NA_CLASSES - (nblk - i)
    return 2


def _na_key_rows(i, rows):
    first = NA_Q_ROWS * i
    lo = int(np.clip(first - NA_ROWS // 2, 0, rows - NA_ROWS))
    hi = int(np.clip(first + NA_Q_ROWS - 1 - NA_ROWS // 2, 0, rows - NA_ROWS)) + NA_ROWS
    return lo, hi


def _na_static_tables(seq):
    rows = seq // GRID_W
    nblk = seq // Q_BLK
    masks = np.full((NA_CLASSES, NA_MAX_KEYS, Q_BLK), NEG_INF, np.float32)
    ridx = np.full((NA_CLASSES, NA_Q_ROWS, NA_MAX_KEY_ROWS), -1, np.int64)
    seen = {}
    for i in range(nblk):
        cls = _na_class_of_block(i, nblk)
        lo, hi = _na_key_rows(i, rows)
        assert (lo * GRID_W) % LANES == 0 and hi - lo <= NA_MAX_KEY_ROWS
        q = np.arange(Q_BLK) + i * Q_BLK
        qr, qc = q // GRID_W, q % GRID_W
        k = np.arange((hi - lo) * GRID_W) + lo * GRID_W
        kr, kc = k // GRID_W, k % GRID_W
        rs = np.clip(qr - NA_ROWS // 2, 0, rows - NA_ROWS)
        cs = np.clip(qc - NA_COLS // 2, 0, GRID_W - NA_COLS)
        ok_r = (kr[:, None] >= rs[None, :]) & (kr[:, None] < rs[None, :] + NA_ROWS)
        ok_c = (kc[:, None] >= cs[None, :]) & (kc[:, None] < cs[None, :] + NA_COLS)
        m = np.full((NA_MAX_KEYS, Q_BLK), NEG_INF, np.float32)
        m[:k.size] = np.where(ok_r & ok_c, 0.0, NEG_INF)
        rr = np.full((NA_Q_ROWS, NA_MAX_KEY_ROWS), -1, np.int64)
        for a in range(NA_Q_ROWS):
            for b in range(hi - lo):
                r, krow = NA_Q_ROWS * i + a, lo + b
                if rs[a * GRID_W] <= krow < rs[a * GRID_W] + NA_ROWS:
                    rr[a, b] = krow - r + NA_ROWS - 1
        if cls in seen:
            assert np.array_equal(seen[cls][0], m) and np.array_equal(seen[cls][1], rr)
        seen[cls] = (m, rr)
        masks[cls], ridx[cls] = m, rr
    return masks, ridx


def _swa_edge_masks():
    assert SWA_WINDOW == Q_BLK
    key = np.arange(Q_BLK)[:, None]
    qry = np.arange(Q_BLK)[None, :]
    tri = np.stack([np.where(key >= qry, 0.0, NEG_INF), np.where(key <= qry, 0.0, NEG_INF)]).astype(np.float32)
    return np.ascontiguousarray(np.concatenate([tri, tri], axis=2))


def _rope_tables(seq):
    half = ROPE_DIM // 2
    inv_freq = ROPE_THETA ** (-jnp.arange(0, ROPE_DIM, 2, dtype=jnp.float32) / ROPE_DIM)
    ang = jnp.arange(seq, dtype=jnp.int32).astype(jnp.float32)[:, None] * inv_freq[None, :]
    cos, sin = jnp.cos(ang), jnp.sin(ang)
    ones = jnp.ones((seq, HEAD_DIM - ROPE_DIM), jnp.float32)
    zeros = jnp.zeros((seq, HEAD_DIM - ROPE_DIM), jnp.float32)
    zh = jnp.zeros((seq, half), jnp.float32)
    c = jnp.concatenate([cos, cos, ones], axis=1)
    sa = jnp.concatenate([-sin, zh, zeros], axis=1)
    sb = jnp.concatenate([zh, sin, zeros], axis=1)
    rep = LANES // HEAD_DIM
    return jnp.tile(c, (1, rep)), jnp.tile(sa, (1, rep)), jnp.tile(sb, (1, rep))


def _rms(x, g):
    return x * lax.rsqrt(jnp.mean(x * x, axis=-1, keepdims=True) + RMS_EPS) * g


def _rope_tile(a, c, sa, sb):
    return a * c + pltpu.roll(a, LANES - ROPE_DIM // 2, axis=1) * sa + pltpu.roll(a, ROPE_DIM // 2, axis=1) * sb


def _in_proj_kernel(x_ref, g_ref, w_ref, c_ref, sa_ref, sb_ref, o_ref):
    h = _rms(x_ref[...], g_ref[...]).astype(jnp.bfloat16)
    c, sa, sb = c_ref[...], sa_ref[...], sb_ref[...]

    def proj(col, width):
        return jnp.dot(h, w_ref[:, col:col + width], preferred_element_type=jnp.float32)

    kv = proj(W_COL_KB, 2 * B_KV_WIDTH)
    left = lax.broadcasted_iota(jnp.int32, (h.shape[0], LANES), 1) < HEAD_DIM
    for col, tile in ((COL_KD, _rope_tile(kv[:, :B_KV_WIDTH], c, sa, sb)), (COL_VD, kv[:, B_KV_WIDTH:])):
        swapped = pltpu.roll(tile, HEAD_DIM, axis=1)
        o_ref[:, col:col + LANES] = jnp.where(left, tile, swapped).astype(o_ref.dtype)
        o_ref[:, col + LANES:col + 2 * LANES] = jnp.where(left, swapped, tile).astype(o_ref.dtype)
    qb = proj(COL_QB, B_WIDTH)
    for t in range(B_WIDTH // LANES):
        lo = t * LANES
        tile = _rope_tile(qb[:, lo:lo + LANES], c, sa, sb) * Q_SCALE
        o_ref[:, COL_QB + lo:COL_QB + lo + LANES] = tile.astype(o_ref.dtype)
    o_ref[:, COL_QA:COL_QA + A_WIDTH] = (proj(COL_QA, A_WIDTH) * Q_SCALE).astype(o_ref.dtype)
    o_ref[:, COL_KA:COL_KA + A_WIDTH] = proj(COL_KA, A_WIDTH).astype(o_ref.dtype)
    o_ref[:, COL_VA:COL_VA + A_WIDTH] = proj(COL_VA, A_WIDTH).astype(o_ref.dtype)


def _in_proj(x2d, g, w_bf16, rope, seq):
    n_tok = x2d.shape[0]
    tm = IN_PROJ_ROWS
    pos_blocks = seq // tm
    const = lambda i: (0, 0)
    pos = lambda i: (i % pos_blocks, 0)
    vmem = 2 * tm * D_MODEL * 4 + 2 * D_MODEL * IN_WIDTH * 2 + 2 * tm * PROJ_WIDTH * 2 + 3 * tm * B_WIDTH * 4 * 2
    return pl.pallas_call(
        _in_proj_kernel,
        out_shape=jax.ShapeDtypeStruct((n_tok, PROJ_WIDTH), jnp.bfloat16),
        grid=(n_tok // tm,),
        in_specs=[
            pl.BlockSpec((tm, D_MODEL), lambda i: (i, 0)),
            pl.BlockSpec((1, D_MODEL), const),
            pl.BlockSpec((D_MODEL, IN_WIDTH), const),
            pl.BlockSpec((tm, LANES), pos),
            pl.BlockSpec((tm, LANES), pos),
            pl.BlockSpec((tm, LANES), pos),
        ],
        out_specs=pl.BlockSpec((tm, PROJ_WIDTH), lambda i: (i, 0)),
        compiler_params=pltpu.CompilerParams(
            dimension_semantics=("arbitrary",), vmem_limit_bytes=_vmem_limit(vmem + (8 << 20))),
        name="in_proj",
    )(x2d, g, w_bf16, *rope)


def _transpose_keys(v_ref, vt_ref, ones_below=False):
    top = lax.broadcasted_iota(jnp.int32, (LANES, LANES), 0) < HEAD_DIM
    for j in range(v_ref.shape[0] // LANES):
        tile = v_ref[j * LANES:(j + 1) * LANES, :].T
        vt_ref[j] = jnp.where(top, tile, jnp.ones_like(tile)) if ones_below else tile


def _attend(q_ref, k_ref, vt_ref, o_ref, blocks, sink_row=None, sums_below=False):
    lane = lax.broadcasted_iota(jnp.int32, (Q_BLK, LANES), 1)
    left = lane < HEAD_DIM
    top = lax.broadcasted_iota(jnp.int32, (LANES, Q_BLK), 0) < HEAD_DIM

    def scores_of(q0, k0, n_keys, add_bias):
        q2 = q_ref[q0:q0 + Q_BLK, :]
        zero = jnp.zeros_like(q2)
        qq = jnp.concatenate([jnp.where(left, q2, zero), jnp.where(left, zero, q2)], axis=0)
        st = lax.dot_general(k_ref[k0:k0 + n_keys, :], qq, (((1,), (1,)), ((), ())),
                             preferred_element_type=jnp.float32)
        return add_bias(st)

    def finish(q0, k0, n_keys, add_bias, st):
        m = jnp.max(st, axis=0, keepdims=True)
        if sink_row is not None:
            m = jnp.maximum(m, sink_row)
        p = jnp.exp2(st - m)
        tile0, full, part = k0 // LANES, n_keys // LANES, n_keys % LANES
        pieces = [vt_ref[tile0 + j] for j in range(full)]
        if part:
            pieces.append(vt_ref[tile0 + full][:, :part])
        ot = jnp.dot(jnp.concatenate(pieces, axis=1), p.astype(jnp.bfloat16), preferred_element_type=jnp.float32)
        den = ot[HEAD_DIM:HEAD_DIM + 1, :] if sums_below else jnp.sum(p, axis=0, keepdims=True)
        if sink_row is not None:
            den = den + jnp.exp2(sink_row - m)
        if sums_below:
            ot = ot[:HEAD_DIM, :] / den
            x = jnp.concatenate([ot[:, :Q_BLK], ot[:, Q_BLK:]], axis=0)
        else:
            ot = ot / den
            x = jnp.where(top, ot[:, :Q_BLK], ot[:, Q_BLK:])
        o_ref[q0:q0 + Q_BLK, :] = x.T.astype(o_ref.dtype)

    pending = [scores_of(*blk) for blk in blocks[:ATTN_SKEW]]
    for c, blk in enumerate(blocks):
        st = pending.pop(0)
        if c + ATTN_SKEW < len(blocks):
            pending.append(scores_of(*blocks[c + ATTN_SKEW]))
        finish(*blk, st)


def _build_na_bias(rpb_ref, mask_ref, t_scr, hp, ridx):
    n_rrow, n_rcol = 2 * NA_ROWS - 1, 2 * NA_COLS - 1
    kc = lax.broadcasted_iota(jnp.int32, (GRID_W, LANES), 0)
    lane = lax.broadcasted_iota(jnp.int32, (GRID_W, LANES), 1)
    diff = kc - (lane % GRID_W) + (NA_COLS - 1)
    second_row = lane >= GRID_W
    used_rows = sorted({int(r) for r in ridx.reshape(-1) if r >= 0})
    zero = jnp.zeros((GRID_W, LANES), jnp.float32)
    for hh in range(2):
        h = 2 * hp + hh
        toeplitz = {-1: zero}
        for rho in used_rows:
            acc = zero
            for d in range(n_rcol):
                acc = jnp.where(diff == d, rpb_ref[(h * n_rrow + rho) * n_rcol + d] * LOG2E, acc)
            toeplitz[rho] = acc
        for cls in range(NA_CLASSES):
            for b in range(NA_MAX_KEY_ROWS):
                blk = jnp.where(second_row, toeplitz[int(ridx[cls, 1, b])], toeplitz[int(ridx[cls, 0, b])])
                ks = slice(b * GRID_W, (b + 1) * GRID_W)
                t_scr[cls, ks, hh * Q_BLK:(hh + 1) * Q_BLK] = blk + mask_ref[cls, ks, :]


def _na_kernel(rpb_ref, q_ref, k_ref, v_ref, mask_ref, o_ref, t_scr, vt_scr, *, ridx, seq):
    nblk = seq // Q_BLK

    @pl.when(pl.program_id(1) == 0)
    def _():
        _build_na_bias(rpb_ref, mask_ref, t_scr, pl.program_id(0), ridx)

    _transpose_keys(v_ref, vt_scr)
    blocks = []
    for i in range(nblk):
        lo, hi = _na_key_rows(i, seq // GRID_W)
        n_keys = (hi - lo) * GRID_W
        add_bias = functools.partial(lambda st, cls, n: st + t_scr[cls, :n, :], cls=_na_class_of_block(i, nblk), n=n_keys)
        blocks.append((i * Q_BLK, lo * GRID_W, n_keys, add_bias))
    _attend(q_ref, k_ref, vt_scr, o_ref, blocks)


def _na_attn(proj, rpb_flat, seq, batch):
    masks, ridx = _na_static_tables(seq)
    n_tok = proj.shape[0]
    n_pairs = A_WIDTH // LANES
    col = lambda base: (lambda hp, b: (b, base // LANES + hp))
    table_bytes = NA_CLASSES * NA_MAX_KEYS * 2 * Q_BLK * 4
    vmem = 2 * 4 * seq * LANES * 2 + 2 * masks.size * 4 + table_bytes + seq * LANES * 2
    return pl.pallas_call(
        functools.partial(_na_kernel, ridx=ridx, seq=seq),
        out_shape=jax.ShapeDtypeStruct((n_tok, A_WIDTH), jnp.bfloat16),
        grid=(n_pairs, batch),
        in_specs=[
            pl.BlockSpec(memory_space=pltpu.SMEM),
            pl.BlockSpec((seq, LANES), col(COL_QA)),
            pl.BlockSpec((seq, LANES), col(COL_KA)),
            pl.BlockSpec((seq, LANES), col(COL_VA)),
            pl.BlockSpec(masks.shape, lambda hp, b: (0, 0, 0)),
        ],
        out_specs=pl.BlockSpec((seq, LANES), lambda hp, b: (b, hp)),
        scratch_shapes=[pltpu.VMEM((NA_CLASSES, NA_MAX_KEYS, 2 * Q_BLK), jnp.float32),
                        pltpu.VMEM((seq // LANES, LANES, LANES), jnp.bfloat16)],
        compiler_params=pltpu.CompilerParams(
            dimension_semantics=("arbitrary", "arbitrary"), vmem_limit_bytes=_vmem_limit(vmem + (16 << 20))),
        name="na_attn",
    )(rpb_flat, proj, proj, proj, jnp.asarray(masks))


def _swa_add_mask(st, edge_ref, first_tile, n):
    parts = []
    for j in range(st.shape[0] // Q_BLK):
        part = st[j * Q_BLK:(j + 1) * Q_BLK, :]
        if first_tile + j != n:
            part = part + edge_ref[0 if first_tile + j < n else 1]
        parts.append(part)
    return jnp.concatenate(parts, axis=0)


def _swa_kernel(sink_ref, q_ref, k_ref, v_ref, edge_ref, o_ref, vt_scr, *, seq):
    t = pl.program_id(1)
    nblk = seq // Q_BLK
    tiles_per_kv = (B_WIDTH // LANES) // SWA_KV_HEADS

    @pl.when(t % tiles_per_kv == 0)
    def _():
        _transpose_keys(v_ref, vt_scr, ones_below=True)

    lane = lax.broadcasted_iota(jnp.int32, (1, 2 * Q_BLK), 1)
    sink_row = jnp.where(lane < Q_BLK, sink_ref[2 * t], sink_ref[2 * t + 1]) * LOG2E
    blocks = []
    for n in range(nblk):
        first, last = max(n - 1, 0), min(n + 1, nblk - 1)
        add_mask = functools.partial(_swa_add_mask, edge_ref=edge_ref, first_tile=first, n=n)
        blocks.append((n * Q_BLK, first * Q_BLK, (last - first + 1) * Q_BLK, add_mask))
    _attend(q_ref, k_ref, vt_scr, o_ref, blocks, sink_row, sums_below=True)


def _swa_attn(proj, sink, seq, batch):
    edges = _swa_edge_masks()
    n_tok = proj.shape[0]
    n_tiles = B_WIDTH // LANES
    tiles_per_kv = n_tiles // SWA_KV_HEADS
    kv_tile = lambda base: (lambda b, t: (b, base // LANES + t // tiles_per_kv))
    vmem = 2 * 4 * seq * LANES * 2 + 2 * edges.size * 4 + seq * LANES * 2
    return pl.pallas_call(
        functools.partial(_swa_kernel, seq=seq),
        out_shape=jax.ShapeDtypeStruct((n_tok, B_WIDTH), jnp.bfloat16),
        grid=(batch, n_tiles),
        in_specs=[
            pl.BlockSpec(memory_space=pltpu.SMEM),
            pl.BlockSpec((seq, LANES), lambda b, t: (b, COL_QB // LANES + t)),
            pl.BlockSpec((seq, LANES), kv_tile(COL_KD)),
            pl.BlockSpec((seq, LANES), kv_tile(COL_VD)),
            pl.BlockSpec(edges.shape, lambda b, t: (0, 0, 0)),
        ],
        out_specs=pl.BlockSpec((seq, LANES), lambda b, t: (b, t)),
        scratch_shapes=[pltpu.VMEM((seq // LANES, LANES, LANES), jnp.bfloat16)],
        compiler_params=pltpu.CompilerParams(
            dimension_semantics=("arbitrary", "arbitrary"), vmem_limit_bytes=_vmem_limit(vmem + (16 << 20))),
        name="swa_attn",
    )(sink, proj, proj, proj, jnp.asarray(edges))


def _out_ffn_kernel(x_ref, oa_ref, ob_ref, gna_ref, gswa_ref, wo_ref, gffn_ref, wg_ref, wu_ref, wd_ref,
                    gfin_ref, o_ref, *, ff_chunk, sub_rows):
    d_ff = wg_ref.shape[1]
    row_slices = [slice(r0, r0 + sub_rows) for r0 in range(0, x_ref.shape[0], sub_rows)]
    mixed = []
    for rows in row_slices:
        na = _rms(oa_ref[rows, :].astype(jnp.float32), gna_ref[...]).astype(jnp.bfloat16)
        nb = _rms(ob_ref[rows, :].astype(jnp.float32), gswa_ref[...]).astype(jnp.bfloat16)
        x1 = (x_ref[rows, :]
              + jnp.dot(na, wo_ref[:A_WIDTH, :], preferred_element_type=jnp.float32)
              + jnp.dot(nb, wo_ref[A_WIDTH:, :], preferred_element_type=jnp.float32))
        mixed.append((x1, _rms(x1, gffn_ref[...]).astype(jnp.bfloat16)))
    for rows, (x1, h) in zip(row_slices, mixed):
        acc = jnp.zeros(x1.shape, jnp.float32)
        for c0 in range(0, d_ff, ff_chunk):
            gate = jnp.dot(h, wg_ref[:, c0:c0 + ff_chunk], preferred_element_type=jnp.float32)
            up = jnp.dot(h, wu_ref[:, c0:c0 + ff_chunk], preferred_element_type=jnp.float32)
            act = (jax.nn.silu(gate) * up).astype(jnp.bfloat16)
            acc = acc + jnp.dot(act, wd_ref[c0:c0 + ff_chunk, :], preferred_element_type=jnp.float32)
        o_ref[rows, :] = _rms(x1 + acc, gfin_ref[...])


def _out_ffn(x2d, o_na, o_swa, g_na, g_swa, wo, g_ffn, wg, wu, wd, g_fin):
    n_tok = x2d.shape[0]
    tm, sub_rows, ff_chunk = FFN_ROWS, FFN_SUB_ROWS, FFN_CHUNK
    d_ff = wg.shape[1]
    const = lambda i: (0, 0)
    row = lambda i: (i, 0)
    resident = lambda shape: pl.BlockSpec(shape, const, pipeline_mode=pl.Buffered(1))
    weights = (D_MODEL * D_MODEL + 3 * D_MODEL * d_ff) * 2
    tiles = 2 * 2 * tm * D_MODEL * 4 + 2 * 2 * tm * A_WIDTH * 2
    vmem = weights + tiles + 2 * (6 * sub_rows * D_MODEL * 4 + 4 * sub_rows * ff_chunk * 4)
    return pl.pallas_call(
        functools.partial(_out_ffn_kernel, ff_chunk=ff_chunk, sub_rows=sub_rows),
        out_shape=jax.ShapeDtypeStruct((n_tok, D_MODEL), jnp.float32),
        grid=(n_tok // tm,),
        in_specs=[
            pl.BlockSpec((tm, D_MODEL), row),
            pl.BlockSpec((tm, A_WIDTH), row),
            pl.BlockSpec((tm, B_WIDTH), row),
            pl.BlockSpec((1, A_WIDTH), const),
            pl.BlockSpec((1, B_WIDTH), const),
            resident((D_MODEL, D_MODEL)),
            pl.BlockSpec((1, D_MODEL), const),
            resident((D_MODEL, d_ff)),
            resident((D_MODEL, d_ff)),
            resident((d_ff, D_MODEL)),
            pl.BlockSpec((1, D_MODEL), const),
        ],
        out_specs=pl.BlockSpec((tm, D_MODEL), row),
        compiler_params=pltpu.CompilerParams(
            dimension_semantics=("arbitrary",), vmem_limit_bytes=_vmem_limit(vmem + (8 << 20))),
        name="out_ffn",
    )(x2d, o_na, o_swa, g_na, g_swa, wo, g_ffn, wg, wu, wd, g_fin)


def kernel(x, g_norm_mix, w_in, na_rpb, swa_sink, g_out_na, g_out_swa, w_out, g_norm_ffn, w_gate, w_up, w_down,
           g_final):
    batch, seq, d_model = x.shape
    assert w_in.shape[0] == 1, "the final RMSNorm is fused into the single layer's FFN call"
    assert d_model == D_MODEL and seq % Q_BLK == 0 and seq // GRID_W >= NA_MAX_KEY_ROWS
    assert B_KV_WIDTH == LANES, "the windowed k (and v) heads together fill one lane tile"
    bf16 = jnp.bfloat16
    rope = _rope_tables(seq)
    x2d = x.reshape(batch * seq, d_model)
    proj = _in_proj(x2d, g_norm_mix[0][None, :], w_in[0].astype(bf16), rope, seq)
    o_na = _na_attn(proj, na_rpb[0].reshape(-1), seq, batch)
    o_swa = _swa_attn(proj, swa_sink[0], seq, batch)
    out = _out_ffn(x2d, o_na, o_swa, g_out_na[0][None, :], g_out_swa[0][None, :], w_out[0].astype(bf16),
                   g_norm_ffn[0][None, :], w_gate[0].astype(bf16), w_up[0].astype(bf16),
                   w_down[0].astype(bf16), g_final[None, :])
    return out.reshape(batch, seq, d_model)
```

```python
import functools

import numpy as np
import jax
import jax.numpy as jnp
from jax import lax
from jax.experimental import pallas as pl
from jax.experimental.pallas import tpu as pltpu

D_MODEL = 1024
HEAD_DIM = 64
NA_HEADS = 8
SWA_HEADS = 8
SWA_KV_HEADS = 2
A_WIDTH = NA_HEADS * HEAD_DIM
B_WIDTH = SWA_HEADS * HEAD_DIM
B_KV_WIDTH = SWA_KV_HEADS * HEAD_DIM
IN_WIDTH = 3 * A_WIDTH + B_WIDTH + 2 * B_KV_WIDTH
GRID_W = 64
NA_ROWS = 8
NA_COLS = 16
SWA_WINDOW = 128
ROPE_THETA = 500000.0
ROPE_DIM = HEAD_DIM // 4
RMS_EPS = 1e-6
NEG_INF = -1e30
LOG2E = 1.4426950408889634
Q_SCALE = HEAD_DIM ** -0.5 * LOG2E

LANES = 128
VMEM_CAPACITY = 64 * 1024 * 1024

PROJ_WIDTH = 3 * A_WIDTH + B_WIDTH + 2 * SWA_KV_HEADS * LANES
COL_QA, COL_KA, COL_VA = 0, A_WIDTH, 2 * A_WIDTH
COL_QB = 3 * A_WIDTH
COL_KD = COL_QB + B_WIDTH
COL_VD = COL_KD + SWA_KV_HEADS * LANES
W_COL_KB = COL_QB + B_WIDTH

Q_BLK = 128
NA_Q_ROWS = Q_BLK // GRID_W
NA_MAX_KEY_ROWS = NA_ROWS + NA_Q_ROWS - 1
NA_MAX_KEYS = NA_MAX_KEY_ROWS * GRID_W
NA_CLASSES = 5
ATTN_SKEW = 3
IN_PROJ_ROWS = 1024
FFN_ROWS = 1024
FFN_SUB_ROWS = 512
FFN_CHUNK = 256


def _vmem_limit(nbytes):
    return int(min(VMEM_CAPACITY - (2 << 20), max(nbytes, 16 << 20)))


def _na_class_of_block(i, nblk):
    if i < 2:
        return i
    if i >= nblk - 2:
        return NA_CLASSES - (nblk - i)
    return 2


def _na_key_rows(i, rows):
    first = NA_Q_ROWS * i
    lo = int(np.clip(first - NA_ROWS // 2, 0, rows - NA_ROWS))
    hi = int(np.clip(first + NA_Q_ROWS - 1 - NA_ROWS // 2, 0, rows - NA_ROWS)) + NA_ROWS
    return lo, hi


def _na_static_tables(seq):
    rows = seq // GRID_W
    nblk = seq // Q_BLK
    masks = np.full((NA_CLASSES, NA_MAX_KEYS, Q_BLK), NEG_INF, np.float32)
    ridx = np.full((NA_CLASSES, NA_Q_ROWS, NA_MAX_KEY_ROWS), -1, np.int64)
    seen = {}
    for i in range(nblk):
        cls = _na_class_of_block(i, nblk)
        lo, hi = _na_key_rows(i, rows)
        assert (lo * GRID_W) % LANES == 0 and hi - lo <= NA_MAX_KEY_ROWS
        q = np.arange(Q_BLK) + i * Q_BLK
        qr, qc = q // GRID_W, q % GRID_W
        k = np.arange((hi - lo) * GRID_W) + lo * GRID_W
        kr, kc = k // GRID_W, k % GRID_W
        rs = np.clip(qr - NA_ROWS // 2, 0, rows - NA_ROWS)
        cs = np.clip(qc - NA_COLS // 2, 0, GRID_W - NA_COLS)
        ok_r = (kr[:, None] >= rs[None, :]) & (kr[:, None] < rs[None, :] + NA_ROWS)
        ok_c = (kc[:, None] >= cs[None, :]) & (kc[:, None] < cs[None, :] + NA_COLS)
        m = np.full((NA_MAX_KEYS, Q_BLK), NEG_INF, np.float32)
        m[:k.size] = np.where(ok_r & ok_c, 0.0, NEG_INF)
        rr = np.full((NA_Q_ROWS, NA_MAX_KEY_ROWS), -1, np.int64)
        for a in range(NA_Q_ROWS):
            for b in range(hi - lo):
                r, krow = NA_Q_ROWS * i + a, lo + b
                if rs[a * GRID_W] <= krow < rs[a * GRID_W] + NA_ROWS:
                    rr[a, b] = krow - r + NA_ROWS - 1
        if cls in seen:
            assert np.array_equal(seen[cls][0], m) and np.array_equal(seen[cls][1], rr)
        seen[cls] = (m, rr)
        masks[cls], ridx[cls] = m, rr
    return masks, ridx


def _swa_edge_masks():
    assert SWA_WINDOW == Q_BLK
    key = np.arange(Q_BLK)[:, None]
    qry = np.arange(Q_BLK)[None, :]
    tri = np.stack([np.where(key >= qry, 0.0, NEG_INF), np.where(key <= qry, 0.0, NEG_INF)]).astype(np.float32)
    return np.ascontiguousarray(np.concatenate([tri, tri], axis=2))


def _rope_tables(seq):
    half = ROPE_DIM // 2
    inv_freq = ROPE_THETA ** (-jnp.arange(0, ROPE_DIM, 2, dtype=jnp.float32) / ROPE_DIM)
    ang = jnp.arange(seq, dtype=jnp.int32).astype(jnp.float32)[:, None] * inv_freq[None, :]
    cos, sin = jnp.cos(ang), jnp.sin(ang)
    ones = jnp.ones((seq, HEAD_DIM - ROPE_DIM), jnp.float32)
    zeros = jnp.zeros((seq, HEAD_DIM - ROPE_DIM), jnp.float32)
    zh = jnp.zeros((seq, half), jnp.float32)
    c = jnp.concatenate([cos, cos, ones], axis=1)
    sa = jnp.concatenate([-sin, zh, zeros], axis=1)
    sb = jnp.concatenate([zh, sin, zeros], axis=1)
    rep = LANES // HEAD_DIM
    return jnp.tile(c, (1, rep)), jnp.tile(sa, (1, rep)), jnp.tile(sb, (1, rep))


def _rms(x, g):
    return x * lax.rsqrt(jnp.mean(x * x, axis=-1, keepdims=True) + RMS_EPS) * g


def _rope_tile(a, c, sa, sb):
    return a * c + pltpu.roll(a, LANES - ROPE_DIM // 2, axis=1) * sa + pltpu.roll(a, ROPE_DIM // 2, axis=1) * sb


def _in_proj_kernel(x_ref, g_ref, w_ref, c_ref, sa_ref, sb_ref, *rest):
    n_extra = (len(rest) - 2) // 2
    extra_in, o_ref, extra_out, w_scr = rest[:n_extra], rest[n_extra], rest[n_extra + 1:-1], rest[-1]

    @pl.when(pl.program_id(0) == 0)
    def _():
        for c0 in range(0, w_ref.shape[1], LANES):
            w_scr[:, c0:c0 + LANES] = w_ref[:, c0:c0 + LANES].astype(w_scr.dtype)

    for src, dst in zip(extra_in, extra_out):
        dst[...] = src[...].astype(dst.dtype)

    h = _rms(x_ref[...], g_ref[...]).astype(jnp.bfloat16)
    c, sa, sb = c_ref[...], sa_ref[...], sb_ref[...]

    def proj(col, width):
        return jnp.dot(h, w_scr[:, col:col + width], preferred_element_type=jnp.float32)

    kv = proj(W_COL_KB, 2 * B_KV_WIDTH)
    left = lax.broadcasted_iota(jnp.int32, (h.shape[0], LANES), 1) < HEAD_DIM
    for col, tile in ((COL_KD, _rope_tile(kv[:, :B_KV_WIDTH], c, sa, sb)), (COL_VD, kv[:, B_KV_WIDTH:])):
        swapped = pltpu.roll(tile, HEAD_DIM, axis=1)
        o_ref[:, col:col + LANES] = jnp.where(left, tile, swapped).astype(o_ref.dtype)
        o_ref[:, col + LANES:col + 2 * LANES] = jnp.where(left, swapped, tile).astype(o_ref.dtype)
    qb = proj(COL_QB, B_WIDTH)
    for t in range(B_WIDTH // LANES):
        lo = t * LANES
        tile = _rope_tile(qb[:, lo:lo + LANES], c, sa, sb) * Q_SCALE
        o_ref[:, COL_QB + lo:COL_QB + lo + LANES] = tile.astype(o_ref.dtype)
    o_ref[:, COL_QA:COL_QA + A_WIDTH] = (proj(COL_QA, A_WIDTH) * Q_SCALE).astype(o_ref.dtype)
    o_ref[:, COL_KA:COL_KA + A_WIDTH] = proj(COL_KA, A_WIDTH).astype(o_ref.dtype)
    o_ref[:, COL_VA:COL_VA + A_WIDTH] = proj(COL_VA, A_WIDTH).astype(o_ref.dtype)


def _in_proj(x2d, g, w_in, rope, seq, later_weights):
    n_tok = x2d.shape[0]
    tm = IN_PROJ_ROWS
    steps = n_tok // tm
    pos_blocks = seq // tm
    const = lambda i: (0, 0)
    row = lambda i: (i, 0)
    pos = lambda i: (i % pos_blocks, 0)
    slices = [(w.shape[0] // steps, w.shape[1]) for w in later_weights]
    assert all(w.shape[0] % steps == 0 and rows % 16 == 0 for w, (rows, _) in zip(later_weights, slices))
    slice_bytes = sum(rows * cols for rows, cols in slices)
    vmem = (2 * tm * D_MODEL * 4 + D_MODEL * IN_WIDTH * (4 + 2) + 2 * tm * PROJ_WIDTH * 2 + 3 * tm * B_WIDTH * 4 * 2
            + 2 * slice_bytes * (4 + 2) + 3 * 2 * tm * LANES * 4)
    outs = pl.pallas_call(
        _in_proj_kernel,
        out_shape=[jax.ShapeDtypeStruct((n_tok, PROJ_WIDTH), jnp.bfloat16)]
        + [jax.ShapeDtypeStruct(w.shape, jnp.bfloat16) for w in later_weights],
        grid=(steps,),
        in_specs=[
            pl.BlockSpec((tm, D_MODEL), row),
            pl.BlockSpec((1, D_MODEL), const),
            pl.BlockSpec((D_MODEL, IN_WIDTH), const, pipeline_mode=pl.Buffered(1)),
            pl.BlockSpec((tm, LANES), pos),
            pl.BlockSpec((tm, LANES), pos),
            pl.BlockSpec((tm, LANES), pos),
        ] + [pl.BlockSpec(blk, row) for blk in slices],
        out_specs=[pl.BlockSpec((tm, PROJ_WIDTH), row)] + [pl.BlockSpec(blk, row) for blk in slices],
        scratch_shapes=[pltpu.VMEM((D_MODEL, IN_WIDTH), jnp.bfloat16)],
        compiler_params=pltpu.CompilerParams(
            dimension_semantics=("arbitrary",), vmem_limit_bytes=_vmem_limit(vmem + (8 << 20))),
        name="in_proj",
    )(x2d, g, w_in, *rope, *later_weights)
    return outs[0], outs[1:]


def _transpose_keys(v_ref, vt_ref, ones_below=False):
    top = lax.broadcasted_iota(jnp.int32, (LANES, LANES), 0) < HEAD_DIM
    for j in range(v_ref.shape[0] // LANES):
        tile = v_ref[j * LANES:(j + 1) * LANES, :].T
        vt_ref[j] = jnp.where(top, tile, jnp.ones_like(tile)) if ones_below else tile


def _attend(q_ref, k_ref, vt_ref, o_ref, blocks, sink_row=None, sums_below=False):
    lane = lax.broadcasted_iota(jnp.int32, (Q_BLK, LANES), 1)
    left = lane < HEAD_DIM
    top = lax.broadcasted_iota(jnp.int32, (LANES, Q_BLK), 0) < HEAD_DIM

    def scores_of(q0, k0, n_keys, add_bias):
        q2 = q_ref[q0:q0 + Q_BLK, :]
        zero = jnp.zeros_like(q2)
        qq = jnp.concatenate([jnp.where(left, q2, zero), jnp.where(left, zero, q2)], axis=0)
        st = lax.dot_general(k_ref[k0:k0 + n_keys, :], qq, (((1,), (1,)), ((), ())),
                             preferred_element_type=jnp.float32)
        return add_bias(st)

    def finish(q0, k0, n_keys, add_bias, st):
        m = jnp.max(st, axis=0, keepdims=True)
        if sink_row is not None:
            m = jnp.maximum(m, sink_row)
        p = jnp.exp2(st - m)
        tile0, full, part = k0 // LANES, n_keys // LANES, n_keys % LANES
        pieces = [vt_ref[tile0 + j] for j in range(full)]
        if part:
            pieces.append(vt_ref[tile0 + full][:, :part])
        ot = jnp.dot(jnp.concatenate(pieces, axis=1), p.astype(jnp.bfloat16), preferred_element_type=jnp.float32)
        den = ot[HEAD_DIM:HEAD_DIM + 1, :] if sums_below else jnp.sum(p, axis=0, keepdims=True)
        if sink_row is not None:
            den = den + jnp.exp2(sink_row - m)
        if sums_below:
            ot = ot[:HEAD_DIM, :] / den
            x = jnp.concatenate([ot[:, :Q_BLK], ot[:, Q_BLK:]], axis=0)
        else:
            ot = ot / den
            x = jnp.where(top, ot[:, :Q_BLK], ot[:, Q_BLK:])
        o_ref[q0:q0 + Q_BLK, :] = x.T.astype(o_ref.dtype)

    pending = [scores_of(*blk) for blk in blocks[:ATTN_SKEW]]
    for c, blk in enumerate(blocks):
        st = pending.pop(0)
        if c + ATTN_SKEW < len(blocks):
            pending.append(scores_of(*blocks[c + ATTN_SKEW]))
        finish(*blk, st)


def _build_na_bias(rpb_ref, mask_ref, t_scr, hp, ridx):
    n_rrow, n_rcol = 2 * NA_ROWS - 1, 2 * NA_COLS - 1
    kc = lax.broadcasted_iota(jnp.int32, (GRID_W, LANES), 0)
    lane = lax.broadcasted_iota(jnp.int32, (GRID_W, LANES), 1)
    diff = kc - (lane % GRID_W) + (NA_COLS - 1)
    second_row = lane >= GRID_W
    used_rows = sorted({int(r) for r in ridx.reshape(-1) if r >= 0})
    zero = jnp.zeros((GRID_W, LANES), jnp.float32)
    for hh in range(2):
        h = 2 * hp + hh
        toeplitz = {-1: zero}
        for rho in used_rows:
            acc = zero
            for d in range(n_rcol):
                acc = jnp.where(diff == d, rpb_ref[(h * n_rrow + rho) * n_rcol + d] * LOG2E, acc)
            toeplitz[rho] = acc
        for cls in range(NA_CLASSES):
            for b in range(NA_MAX_KEY_ROWS):
                blk = jnp.where(second_row, toeplitz[int(ridx[cls, 1, b])], toeplitz[int(ridx[cls, 0, b])])
                ks = slice(b * GRID_W, (b + 1) * GRID_W)
                t_scr[cls, ks, hh * Q_BLK:(hh + 1) * Q_BLK] = blk + mask_ref[cls, ks, :]


def _na_kernel(rpb_ref, q_ref, k_ref, v_ref, mask_ref, o_ref, t_scr, vt_scr, *, ridx, seq):
    nblk = seq // Q_BLK

    @pl.when(pl.program_id(1) == 0)
    def _():
        _build_na_bias(rpb_ref, mask_ref, t_scr, pl.program_id(0), ridx)

    _transpose_keys(v_ref, vt_scr)
    blocks = []
    for i in range(nblk):
        lo, hi = _na_key_rows(i, seq // GRID_W)
        n_keys = (hi - lo) * GRID_W
        add_bias = functools.partial(lambda st, cls, n: st + t_scr[cls, :n, :], cls=_na_class_of_block(i, nblk), n=n_keys)
        blocks.append((i * Q_BLK, lo * GRID_W, n_keys, add_bias))
    _attend(q_ref, k_ref, vt_scr, o_ref, blocks)


def _na_attn(proj, rpb_flat, seq, batch):
    masks, ridx = _na_static_tables(seq)
    n_tok = proj.shape[0]
    n_pairs = A_WIDTH // LANES
    col = lambda base: (lambda hp, b: (b, base // LANES + hp))
    table_bytes = NA_CLASSES * NA_MAX_KEYS * 2 * Q_BLK * 4
    vmem = 2 * 4 * seq * LANES * 2 + 2 * masks.size * 4 + table_bytes + seq * LANES * 2
    return pl.pallas_call(
        functools.partial(_na_kernel, ridx=ridx, seq=seq),
        out_shape=jax.ShapeDtypeStruct((n_tok, A_WIDTH), jnp.bfloat16),
        grid=(n_pairs, batch),
        in_specs=[
            pl.BlockSpec(memory_space=pltpu.SMEM),
            pl.BlockSpec((seq, LANES), col(COL_QA)),
            pl.BlockSpec((seq, LANES), col(COL_KA)),
            pl.BlockSpec((seq, LANES), col(COL_VA)),
            pl.BlockSpec(masks.shape, lambda hp, b: (0, 0, 0)),
        ],
        out_specs=pl.BlockSpec((seq, LANES), lambda hp, b: (b, hp)),
        scratch_shapes=[pltpu.VMEM((NA_CLASSES, NA_MAX_KEYS, 2 * Q_BLK), jnp.float32),
                        pltpu.VMEM((seq // LANES, LANES, LANES), jnp.bfloat16)],
        compiler_params=pltpu.CompilerParams(
            dimension_semantics=("arbitrary", "arbitrary"), vmem_limit_bytes=_vmem_limit(vmem + (16 << 20))),
        name="na_attn",
    )(rpb_flat, proj, proj, proj, jnp.asarray(masks))


def _swa_add_mask(st, edge_ref, first_tile, n):
    parts = []
    for j in range(st.shape[0] // Q_BLK):
        part = st[j * Q_BLK:(j + 1) * Q_BLK, :]
        if first_tile + j != n:
            part = part + edge_ref[0 if first_tile + j < n else 1]
        parts.append(part)
    return jnp.concatenate(parts, axis=0)


def _swa_kernel(sink_ref, q_ref, k_ref, v_ref, edge_ref, o_ref, vt_scr, *, seq):
    t = pl.program_id(1)
    nblk = seq // Q_BLK
    tiles_per_kv = (B_WIDTH // LANES) // SWA_KV_HEADS

    @pl.when(t % tiles_per_kv == 0)
    def _():
        _transpose_keys(v_ref, vt_scr, ones_below=True)

    lane = lax.broadcasted_iota(jnp.int32, (1, 2 * Q_BLK), 1)
    sink_row = jnp.where(lane < Q_BLK, sink_ref[2 * t], sink_ref[2 * t + 1]) * LOG2E
    blocks = []
    for n in range(nblk):
        first, last = max(n - 1, 0), min(n + 1, nblk - 1)
        add_mask = functools.partial(_swa_add_mask, edge_ref=edge_ref, first_tile=first, n=n)
        blocks.append((n * Q_BLK, first * Q_BLK, (last - first + 1) * Q_BLK, add_mask))
    _attend(q_ref, k_ref, vt_scr, o_ref, blocks, sink_row, sums_below=True)


def _swa_attn(proj, sink, seq, batch):
    edges = _swa_edge_masks()
    n_tok = proj.shape[0]
    n_tiles = B_WIDTH // LANES
    tiles_per_kv = n_tiles // SWA_KV_HEADS
    kv_tile = lambda base: (lambda b, t: (b, base // LANES + t // tiles_per_kv))
    vmem = 2 * 4 * seq * LANES * 2 + 2 * edges.size * 4 + seq * LANES * 2
    return pl.pallas_call(
        functools.partial(_swa_kernel, seq=seq),
        out_shape=jax.ShapeDtypeStruct((n_tok, B_WIDTH), jnp.bfloat16),
        grid=(batch, n_tiles),
        in_specs=[
            pl.BlockSpec(memory_space=pltpu.SMEM),
            pl.BlockSpec((seq, LANES), lambda b, t: (b, COL_QB // LANES + t)),
            pl.BlockSpec((seq, LANES), kv_tile(COL_KD)),
            pl.BlockSpec((seq, LANES), kv_tile(COL_VD)),
            pl.BlockSpec(edges.shape, lambda b, t: (0, 0, 0)),
        ],
        out_specs=pl.BlockSpec((seq, LANES), lambda b, t: (b, t)),
        scratch_shapes=[pltpu.VMEM((seq // LANES, LANES, LANES), jnp.bfloat16)],
        compiler_params=pltpu.CompilerParams(
            dimension_semantics=("arbitrary", "arbitrary"), vmem_limit_bytes=_vmem_limit(vmem + (16 << 20))),
        name="swa_attn",
    )(sink, proj, proj, proj, jnp.asarray(edges))


def _out_ffn_kernel(x_ref, oa_ref, ob_ref, gna_ref, gswa_ref, wo_ref, gffn_ref, wg_ref, wu_ref, wd_ref,
                    gfin_ref, o_ref, *, ff_chunk, sub_rows):
    d_ff = wg_ref.shape[1]
    row_slices = [slice(r0, r0 + sub_rows) for r0 in range(0, x_ref.shape[0], sub_rows)]
    mixed = []
    for rows in row_slices:
        na = _rms(oa_ref[rows, :].astype(jnp.float32), gna_ref[...]).astype(jnp.bfloat16)
        nb = _rms(ob_ref[rows, :].astype(jnp.float32), gswa_ref[...]).astype(jnp.bfloat16)
        x1 = (x_ref[rows, :]
              + jnp.dot(na, wo_ref[:A_WIDTH, :], preferred_element_type=jnp.float32)
              + jnp.dot(nb, wo_ref[A_WIDTH:, :], preferred_element_type=jnp.float32))
        mixed.append((x1, _rms(x1, gffn_ref[...]).astype(jnp.bfloat16)))
    for rows, (x1, h) in zip(row_slices, mixed):
        acc = jnp.zeros(x1.shape, jnp.float32)
        for c0 in range(0, d_ff, ff_chunk):
            gate = jnp.dot(h, wg_ref[:, c0:c0 + ff_chunk], preferred_element_type=jnp.float32)
            up = jnp.dot(h, wu_ref[:, c0:c0 + ff_chunk], preferred_element_type=jnp.float32)
            act = (jax.nn.silu(gate) * up).astype(jnp.bfloat16)
            acc = acc + jnp.dot(act, wd_ref[c0:c0 + ff_chunk, :], preferred_element_type=jnp.float32)
        o_ref[rows, :] = _rms(x1 + acc, gfin_ref[...])


def _out_ffn(x2d, o_na, o_swa, g_na, g_swa, wo, g_ffn, wg, wu, wd, g_fin):
    n_tok = x2d.shape[0]
    tm, sub_rows, ff_chunk = FFN_ROWS, FFN_SUB_ROWS, FFN_CHUNK
    d_ff = wg.shape[1]
    const = lambda i: (0, 0)
    row = lambda i: (i, 0)
    resident = lambda shape: pl.BlockSpec(shape, const, pipeline_mode=pl.Buffered(1))
    weights = (D_MODEL * D_MODEL + 3 * D_MODEL * d_ff) * 2
    tiles = 2 * 2 * tm * D_MODEL * 4 + 2 * 2 * tm * A_WIDTH * 2
    vmem = weights + tiles + 2 * (6 * sub_rows * D_MODEL * 4 + 4 * sub_rows * ff_chunk * 4)
    return pl.pallas_call(
        functools.partial(_out_ffn_kernel, ff_chunk=ff_chunk, sub_rows=sub_rows),
        out_shape=jax.ShapeDtypeStruct((n_tok, D_MODEL), jnp.float32),
        grid=(n_tok // tm,),
        in_specs=[
            pl.BlockSpec((tm, D_MODEL), row),
            pl.BlockSpec((tm, A_WIDTH), row),
            pl.BlockSpec((tm, B_WIDTH), row),
            pl.BlockSpec((1, A_WIDTH), const),
            pl.BlockSpec((1, B_WIDTH), const),
            resident((D_MODEL, D_MODEL)),
            pl.BlockSpec((1, D_MODEL), const),
            resident((D_MODEL, d_ff)),
            resident((D_MODEL, d_ff)),
            resident((d_ff, D_MODEL)),
            pl.BlockSpec((1, D_MODEL), const),
        ],
        out_specs=pl.BlockSpec((tm, D_MODEL), row),
        compiler_params=pltpu.CompilerParams(
            dimension_semantics=("arbitrary",), vmem_limit_bytes=_vmem_limit(vmem + (8 << 20))),
        name="out_ffn",
    )(x2d, o_na, o_swa, g_na, g_swa, wo, g_ffn, wg, wu, wd, g_fin)


def kernel(x, g_norm_mix, w_in, na_rpb, swa_sink, g_out_na, g_out_swa, w_out, g_norm_ffn, w_gate, w_up, w_down,
           g_final):
    batch, seq, d_model = x.shape
    assert w_in.shape[0] == 1, "the final RMSNorm is fused into the single layer's FFN call"
    assert d_model == D_MODEL and seq % Q_BLK == 0 and seq // GRID_W >= NA_MAX_KEY_ROWS
    assert B_KV_WIDTH == LANES, "the windowed k (and v) heads together fill one lane tile"
    rope = _rope_tables(seq)
    x2d = x.reshape(batch * seq, d_model)
    proj, (wo, wg, wu, wd) = _in_proj(x2d, g_norm_mix[0][None, :], w_in[0], rope, seq,
                                      (w_out[0], w_gate[0], w_up[0], w_down[0]))
    o_na = _na_attn(proj, na_rpb[0].reshape(-1), seq, batch)
    o_swa = _swa_attn(proj, swa_sink[0], seq, batch)
    out = _out_ffn(x2d, o_na, o_swa, g_out_na[0][None, :], g_out_swa[0][None, :], wo,
                   g_norm_ffn[0][None, :], wg, wu, wd, g_final[None, :])
    return out.reshape(batch, seq, d_model)
```

```python
import functools

import numpy as np
import jax
import jax.numpy as jnp
from jax import lax
from jax.experimental import pallas as pl
from jax.experimental.pallas import tpu as pltpu

D_MODEL = 1024
HEAD_DIM = 64
NA_HEADS = 8
SWA_HEADS = 8
SWA_KV_HEADS = 2
A_WIDTH = NA_HEADS * HEAD_DIM
B_WIDTH = SWA_HEADS * HEAD_DIM
B_KV_WIDTH = SWA_KV_HEADS * HEAD_DIM
IN_WIDTH = 3 * A_WIDTH + B_WIDTH + 2 * B_KV_WIDTH
GRID_W = 64
NA_ROWS = 8
NA_COLS = 16
SWA_WINDOW = 128
ROPE_THETA = 500000.0
ROPE_DIM = HEAD_DIM // 4
RMS_EPS = 1e-6
NEG_INF = -1e30
LOG2E = 1.4426950408889634
Q_SCALE = HEAD_DIM ** -0.5 * LOG2E

LANES = 128
VMEM_CAPACITY = 64 * 1024 * 1024

PROJ_WIDTH = 3 * A_WIDTH + B_WIDTH + 2 * SWA_KV_HEADS * LANES
COL_QA, COL_KA, COL_VA = 0, A_WIDTH, 2 * A_WIDTH
COL_QB = 3 * A_WIDTH
COL_KD = COL_QB + B_WIDTH
COL_VD = COL_KD + SWA_KV_HEADS * LANES
W_COL_KB = COL_QB + B_WIDTH

Q_BLK = 128
NA_Q_ROWS = Q_BLK // GRID_W
NA_MAX_KEY_ROWS = NA_ROWS + NA_Q_ROWS - 1
NA_MAX_KEYS = NA_MAX_KEY_ROWS * GRID_W
NA_CLASSES = 5
ATTN_SKEW = 3
IN_PROJ_ROWS = 1024
FFN_ROWS = 1024
FFN_SUB_ROWS = 512
FFN_CHUNK = 256


def _vmem_limit(nbytes):
    return int(min(VMEM_CAPACITY - (2 << 20), max(nbytes, 16 << 20)))


def _attn_vmem_limit(nbytes):
    assert nbytes + (16 << 20) <= VMEM_CAPACITY - (2 << 20)
    return VMEM_CAPACITY - (2 << 20)


def _na_class_of_block(i, nblk):
    if i < 2:
        return i
    if i >= nblk - 2:
        return NA_CLASSES - (nblk - i)
    return 2


def _na_key_rows(i, rows):
    first = NA_Q_ROWS * i
    lo = int(np.clip(first - NA_ROWS // 2, 0, rows - NA_ROWS))
    hi = int(np.clip(first + NA_Q_ROWS - 1 - NA_ROWS // 2, 0, rows - NA_ROWS)) + NA_ROWS
    return lo, hi


def _na_static_tables(seq):
    rows = seq // GRID_W
    nblk = seq // Q_BLK
    masks = np.full((NA_CLASSES, NA_MAX_KEYS, Q_BLK), NEG_INF, np.float32)
    ridx = np.full((NA_CLASSES, NA_Q_ROWS, NA_MAX_KEY_ROWS), -1, np.int64)
    seen = {}
    for i in range(nblk):
        cls = _na_class_of_block(i, nblk)
        lo, hi = _na_key_rows(i, rows)
        assert (lo * GRID_W) % LANES == 0 and hi - lo <= NA_MAX_KEY_ROWS
        q = np.arange(Q_BLK) + i * Q_BLK
        qr, qc = q // GRID_W, q % GRID_W
        k = np.arange((hi - lo) * GRID_W) + lo * GRID_W
        kr, kc = k // GRID_W, k % GRID_W
        rs = np.clip(qr - NA_ROWS // 2, 0, rows - NA_ROWS)
        cs = np.clip(qc - NA_COLS // 2, 0, GRID_W - NA_COLS)
        ok_r = (kr[:, None] >= rs[None, :]) & (kr[:, None] < rs[None, :] + NA_ROWS)
        ok_c = (kc[:, None] >= cs[None, :]) & (kc[:, None] < cs[None, :] + NA_COLS)
        m = np.full((NA_MAX_KEYS, Q_BLK), NEG_INF, np.float32)
        m[:k.size] = np.where(ok_r & ok_c, 0.0, NEG_INF)
        rr = np.full((NA_Q_ROWS, NA_MAX_KEY_ROWS), -1, np.int64)
        for a in range(NA_Q_ROWS):
            for b in range(hi - lo):
                r, krow = NA_Q_ROWS * i + a, lo + b
                if rs[a * GRID_W] <= krow < rs[a * GRID_W] + NA_ROWS:
                    rr[a, b] = krow - r + NA_ROWS - 1
        if cls in seen:
            assert np.array_equal(seen[cls][0], m) and np.array_equal(seen[cls][1], rr)
        seen[cls] = (m, rr)
        masks[cls], ridx[cls] = m, rr
    return masks, ridx


def _swa_edge_masks():
    assert SWA_WINDOW == Q_BLK
    key = np.arange(Q_BLK)[:, None]
    qry = np.arange(Q_BLK)[None, :]
    tri = np.stack([np.where(key >= qry, 0.0, NEG_INF), np.where(key <= qry, 0.0, NEG_INF)]).astype(np.float32)
    return np.ascontiguousarray(np.concatenate([tri, tri], axis=2))


def _rope_tables(seq):
    half = ROPE_DIM // 2
    inv_freq = np.float32(ROPE_THETA) ** (-np.arange(0, ROPE_DIM, 2, dtype=np.float32) / np.float32(ROPE_DIM))
    ang = np.arange(seq, dtype=np.float32)[:, None] * inv_freq[None, :].astype(np.float32)
    cos, sin = np.cos(ang).astype(np.float32), np.sin(ang).astype(np.float32)
    ones = np.ones((seq, HEAD_DIM - ROPE_DIM), np.float32)
    zeros = np.zeros((seq, HEAD_DIM - ROPE_DIM), np.float32)
    zh = np.zeros((seq, half), np.float32)
    c = np.concatenate([cos, cos, ones], axis=1)
    sa = np.concatenate([-sin, zh, zeros], axis=1)
    sb = np.concatenate([zh, sin, zeros], axis=1)
    rep = LANES // HEAD_DIM
    return tuple(jnp.asarray(np.tile(t, (1, rep))) for t in (c, sa, sb))


def _rms(x, g):
    return x * lax.rsqrt(jnp.mean(x * x, axis=-1, keepdims=True) + RMS_EPS) * g


def _rope_tile(a, c, sa, sb):
    return a * c + pltpu.roll(a, LANES - ROPE_DIM // 2, axis=1) * sa + pltpu.roll(a, ROPE_DIM // 2, axis=1) * sb


def _in_proj_kernel(x_ref, g_ref, w_ref, c_ref, sa_ref, sb_ref, *rest):
    n_extra = (len(rest) - 2) // 2
    extra_in, o_ref, extra_out, w_scr = rest[:n_extra], rest[n_extra], rest[n_extra + 1:-1], rest[-1]

    @pl.when(pl.program_id(0) == 0)
    def _():
        for c0 in range(0, w_ref.shape[1], LANES):
            w_scr[:, c0:c0 + LANES] = w_ref[:, c0:c0 + LANES].astype(w_scr.dtype)

    for src, dst in zip(extra_in, extra_out):
        dst[...] = src[...].astype(dst.dtype)

    h = _rms(x_ref[...], g_ref[...]).astype(jnp.bfloat16)
    c, sa, sb = c_ref[...], sa_ref[...], sb_ref[...]

    def proj(col, width):
        return jnp.dot(h, w_scr[:, col:col + width], preferred_element_type=jnp.float32)

    kv = proj(W_COL_KB, 2 * B_KV_WIDTH)
    left = lax.broadcasted_iota(jnp.int32, (h.shape[0], LANES), 1) < HEAD_DIM
    for col, tile in ((COL_KD, _rope_tile(kv[:, :B_KV_WIDTH], c, sa, sb)), (COL_VD, kv[:, B_KV_WIDTH:])):
        swapped = pltpu.roll(tile, HEAD_DIM, axis=1)
        o_ref[:, col:col + LANES] = jnp.where(left, tile, swapped).astype(o_ref.dtype)
        o_ref[:, col + LANES:col + 2 * LANES] = jnp.where(left, swapped, tile).astype(o_ref.dtype)
    qb = proj(COL_QB, B_WIDTH)
    for t in range(B_WIDTH // LANES):
        lo = t * LANES
        tile = _rope_tile(qb[:, lo:lo + LANES], c, sa, sb) * Q_SCALE
        o_ref[:, COL_QB + lo:COL_QB + lo + LANES] = tile.astype(o_ref.dtype)
    o_ref[:, COL_QA:COL_QA + A_WIDTH] = (proj(COL_QA, A_WIDTH) * Q_SCALE).astype(o_ref.dtype)
    o_ref[:, COL_KA:COL_KA + A_WIDTH] = proj(COL_KA, A_WIDTH).astype(o_ref.dtype)
    o_ref[:, COL_VA:COL_VA + A_WIDTH] = proj(COL_VA, A_WIDTH).astype(o_ref.dtype)


def _in_proj(x2d, g, w_in, rope, seq, later_weights):
    n_tok = x2d.shape[0]
    tm = IN_PROJ_ROWS
    steps = n_tok // tm
    pos_blocks = seq // tm
    const = lambda i: (0, 0)
    row = lambda i: (i, 0)
    pos = lambda i: (i % pos_blocks, 0)
    slices = [(w.shape[0] // steps, w.shape[1]) for w in later_weights]
    assert all(w.shape[0] % steps == 0 and rows % 16 == 0 for w, (rows, _) in zip(later_weights, slices))
    slice_bytes = sum(rows * cols for rows, cols in slices)
    vmem = (2 * tm * D_MODEL * 4 + D_MODEL * IN_WIDTH * (4 + 2) + 2 * tm * PROJ_WIDTH * 2 + 3 * tm * B_WIDTH * 4 * 2
            + 2 * slice_bytes * (4 + 2) + 3 * 2 * tm * LANES * 4)
    outs = pl.pallas_call(
        _in_proj_kernel,
        out_shape=[jax.ShapeDtypeStruct((n_tok, PROJ_WIDTH), jnp.bfloat16)]
        + [jax.ShapeDtypeStruct(w.shape, jnp.bfloat16) for w in later_weights],
        grid=(steps,),
        in_specs=[
            pl.BlockSpec((tm, D_MODEL), row),
            pl.BlockSpec((1, D_MODEL), const),
            pl.BlockSpec((D_MODEL, IN_WIDTH), const, pipeline_mode=pl.Buffered(1)),
            pl.BlockSpec((tm, LANES), pos),
            pl.BlockSpec((tm, LANES), pos),
            pl.BlockSpec((tm, LANES), pos),
        ] + [pl.BlockSpec(blk, row) for blk in slices],
        out_specs=[pl.BlockSpec((tm, PROJ_WIDTH), row)] + [pl.BlockSpec(blk, row) for blk in slices],
        scratch_shapes=[pltpu.VMEM((D_MODEL, IN_WIDTH), jnp.bfloat16)],
        compiler_params=pltpu.CompilerParams(
            dimension_semantics=("arbitrary",), vmem_limit_bytes=_vmem_limit(vmem + (8 << 20))),
        name="in_proj",
    )(x2d, g, w_in, *rope, *later_weights)
    return outs[0], outs[1:]


def _transpose_keys(v_ref, vt_ref, ones_below=False):
    top = lax.broadcasted_iota(jnp.int32, (LANES, LANES), 0) < HEAD_DIM
    for j in range(v_ref.shape[0] // LANES):
        tile = v_ref[j * LANES:(j + 1) * LANES, :].T
        vt_ref[j] = jnp.where(top, tile, jnp.ones_like(tile)) if ones_below else tile


def _attend(q_ref, k_ref, vt_ref, o_ref, blocks, sink_row=None, sums_below=False):
    lane = lax.broadcasted_iota(jnp.int32, (Q_BLK, LANES), 1)
    left = lane < HEAD_DIM
    top = lax.broadcasted_iota(jnp.int32, (LANES, Q_BLK), 0) < HEAD_DIM

    def scores_of(q0, k0, n_keys, add_bias):
        q2 = q_ref[q0:q0 + Q_BLK, :]
        zero = jnp.zeros_like(q2)
        qq = jnp.concatenate([jnp.where(left, q2, zero), jnp.where(left, zero, q2)], axis=0)
        st = lax.dot_general(k_ref[k0:k0 + n_keys, :], qq, (((1,), (1,)), ((), ())),
                             preferred_element_type=jnp.float32)
        return add_bias(st)

    def finish(q0, k0, n_keys, add_bias, st):
        m = jnp.max(st, axis=0, keepdims=True)
        if sink_row is not None:
            m = jnp.maximum(m, sink_row)
        p = jnp.exp2(st - m)
        tile0, full, part = k0 // LANES, n_keys // LANES, n_keys % LANES
        pieces = [vt_ref[tile0 + j] for j in range(full)]
        if part:
            pieces.append(vt_ref[tile0 + full][:, :part])
        ot = jnp.dot(jnp.concatenate(pieces, axis=1), p.astype(jnp.bfloat16), preferred_element_type=jnp.float32)
        den = ot[HEAD_DIM:HEAD_DIM + 1, :] if sums_below else jnp.sum(p, axis=0, keepdims=True)
        if sink_row is not None:
            den = den + jnp.exp2(sink_row - m)
        if sums_below:
            ot = ot[:HEAD_DIM, :] / den
            x = jnp.concatenate([ot[:, :Q_BLK], ot[:, Q_BLK:]], axis=0)
        else:
            ot = ot / den
            x = jnp.where(top, ot[:, :Q_BLK], ot[:, Q_BLK:])
        o_ref[q0:q0 + Q_BLK, :] = x.T.astype(o_ref.dtype)

    pending = [scores_of(*blk) for blk in blocks[:ATTN_SKEW]]
    for c, blk in enumerate(blocks):
        st = pending.pop(0)
        if c + ATTN_SKEW < len(blocks):
            pending.append(scores_of(*blocks[c + ATTN_SKEW]))
        finish(*blk, st)


def _build_na_bias(rpb_ref, mask_ref, t_scr, hp, ridx):
    n_rrow, n_rcol = 2 * NA_ROWS - 1, 2 * NA_COLS - 1
    kc = lax.broadcasted_iota(jnp.int32, (GRID_W, LANES), 0)
    lane = lax.broadcasted_iota(jnp.int32, (GRID_W, LANES), 1)
    diff = kc - (lane % GRID_W) + (NA_COLS - 1)
    second_row = lane >= GRID_W
    used_rows = sorted({int(r) for r in ridx.reshape(-1) if r >= 0})
    zero = jnp.zeros((GRID_W, LANES), jnp.float32)
    for hh in range(2):
        h = 2 * hp + hh
        toeplitz = {-1: zero}
        for rho in used_rows:
            acc = zero
            for d in range(n_rcol):
                acc = jnp.where(diff == d, rpb_ref[(h * n_rrow + rho) * n_rcol + d] * LOG2E, acc)
            toeplitz[rho] = acc
        for cls in range(NA_CLASSES):
            for b in range(NA_MAX_KEY_ROWS):
                blk = jnp.where(second_row, toeplitz[int(ridx[cls, 1, b])], toeplitz[int(ridx[cls, 0, b])])
                ks = slice(b * GRID_W, (b + 1) * GRID_W)
                t_scr[cls, ks, hh * Q_BLK:(hh + 1) * Q_BLK] = blk + mask_ref[cls, ks, :]


def _na_kernel(rpb_ref, q_ref, k_ref, v_ref, mask_ref, o_ref, t_scr, vt_scr, *, ridx, seq):
    nblk = seq // Q_BLK

    @pl.when(pl.program_id(1) == 0)
    def _():
        _build_na_bias(rpb_ref, mask_ref, t_scr, pl.program_id(0), ridx)

    _transpose_keys(v_ref, vt_scr)
    blocks = []
    for i in range(nblk):
        lo, hi = _na_key_rows(i, seq // GRID_W)
        n_keys = (hi - lo) * GRID_W
        add_bias = functools.partial(lambda st, cls, n: st + t_scr[cls, :n, :], cls=_na_class_of_block(i, nblk), n=n_keys)
        blocks.append((i * Q_BLK, lo * GRID_W, n_keys, add_bias))
    _attend(q_ref, k_ref, vt_scr, o_ref, blocks)


def _na_attn(proj, rpb_flat, seq, batch):
    masks, ridx = _na_static_tables(seq)
    n_tok = proj.shape[0]
    n_pairs = A_WIDTH // LANES
    col = lambda base: (lambda hp, b: (b, base // LANES + hp))
    table_bytes = NA_CLASSES * NA_MAX_KEYS * 2 * Q_BLK * 4
    vmem = 2 * 4 * seq * LANES * 2 + 2 * masks.size * 4 + table_bytes + seq * LANES * 2
    return pl.pallas_call(
        functools.partial(_na_kernel, ridx=ridx, seq=seq),
        out_shape=jax.ShapeDtypeStruct((n_tok, A_WIDTH), jnp.bfloat16),
        grid=(n_pairs, batch),
        in_specs=[
            pl.BlockSpec(memory_space=pltpu.SMEM),
            pl.BlockSpec((seq, LANES), col(COL_QA)),
            pl.BlockSpec((seq, LANES), col(COL_KA)),
            pl.BlockSpec((seq, LANES), col(COL_VA)),
            pl.BlockSpec(masks.shape, lambda hp, b: (0, 0, 0)),
        ],
        out_specs=pl.BlockSpec((seq, LANES), lambda hp, b: (b, hp)),
        scratch_shapes=[pltpu.VMEM((NA_CLASSES, NA_MAX_KEYS, 2 * Q_BLK), jnp.float32),
                        pltpu.VMEM((seq // LANES, LANES, LANES), jnp.bfloat16)],
        compiler_params=pltpu.CompilerParams(
            dimension_semantics=("arbitrary", "arbitrary"), vmem_limit_bytes=_attn_vmem_limit(vmem)),
        name="na_attn",
    )(rpb_flat, proj, proj, proj, jnp.asarray(masks))


def _swa_add_mask(st, edge_ref, first_tile, n):
    parts = []
    for j in range(st.shape[0] // Q_BLK):
        part = st[j * Q_BLK:(j + 1) * Q_BLK, :]
        if first_tile + j != n:
            part = part + edge_ref[0 if first_tile + j < n else 1]
        parts.append(part)
    return jnp.concatenate(parts, axis=0)


def _swa_kernel(sink_ref, q_ref, k_ref, v_ref, edge_ref, o_ref, vt_scr, *, seq):
    t = pl.program_id(1)
    nblk = seq // Q_BLK
    tiles_per_kv = (B_WIDTH // LANES) // SWA_KV_HEADS

    @pl.when(t % tiles_per_kv == 0)
    def _():
        _transpose_keys(v_ref, vt_scr, ones_below=True)

    lane = lax.broadcasted_iota(jnp.int32, (1, 2 * Q_BLK), 1)
    sink_row = jnp.where(lane < Q_BLK, sink_ref[2 * t], sink_ref[2 * t + 1]) * LOG2E
    blocks = []
    for n in range(nblk):
        first, last = max(n - 1, 0), min(n + 1, nblk - 1)
        add_mask = functools.partial(_swa_add_mask, edge_ref=edge_ref, first_tile=first, n=n)
        blocks.append((n * Q_BLK, first * Q_BLK, (last - first + 1) * Q_BLK, add_mask))
    _attend(q_ref, k_ref, vt_scr, o_ref, blocks, sink_row, sums_below=True)


def _swa_attn(proj, sink, seq, batch):
    edges = _swa_edge_masks()
    n_tok = proj.shape[0]
    n_tiles = B_WIDTH // LANES
    tiles_per_kv = n_tiles // SWA_KV_HEADS
    kv_tile = lambda base: (lambda b, t: (b, base // LANES + t // tiles_per_kv))
    vmem = 2 * 4 * seq * LANES * 2 + 2 * edges.size * 4 + seq * LANES * 2
    return pl.pallas_call(
        functools.partial(_swa_kernel, seq=seq),
        out_shape=jax.ShapeDtypeStruct((n_tok, B_WIDTH), jnp.bfloat16),
        grid=(batch, n_tiles),
        in_specs=[
            pl.BlockSpec(memory_space=pltpu.SMEM),
            pl.BlockSpec((seq, LANES), lambda b, t: (b, COL_QB // LANES + t)),
            pl.BlockSpec((seq, LANES), kv_tile(COL_KD)),
            pl.BlockSpec((seq, LANES), kv_tile(COL_VD)),
            pl.BlockSpec(edges.shape, lambda b, t: (0, 0, 0)),
        ],
        out_specs=pl.BlockSpec((seq, LANES), lambda b, t: (b, t)),
        scratch_shapes=[pltpu.VMEM((seq // LANES, LANES, LANES), jnp.bfloat16)],
        compiler_params=pltpu.CompilerParams(
            dimension_semantics=("arbitrary", "arbitrary"), vmem_limit_bytes=_attn_vmem_limit(vmem)),
        name="swa_attn",
    )(sink, proj, proj, proj, jnp.asarray(edges))


def _out_ffn_kernel(x_ref, oa_ref, ob_ref, gna_ref, gswa_ref, wo_ref, gffn_ref, wg_ref, wu_ref, wd_ref,
                    gfin_ref, o_ref, *, ff_chunk, sub_rows):
    d_ff = wg_ref.shape[1]
    row_slices = [slice(r0, r0 + sub_rows) for r0 in range(0, x_ref.shape[0], sub_rows)]
    mixed = []
    for rows in row_slices:
        na = _rms(oa_ref[rows, :].astype(jnp.float32), gna_ref[...]).astype(jnp.bfloat16)
        nb = _rms(ob_ref[rows, :].astype(jnp.float32), gswa_ref[...]).astype(jnp.bfloat16)
        x1 = (x_ref[rows, :]
              + jnp.dot(na, wo_ref[:A_WIDTH, :], preferred_element_type=jnp.float32)
              + jnp.dot(nb, wo_ref[A_WIDTH:, :], preferred_element_type=jnp.float32))
        mixed.append((x1, _rms(x1, gffn_ref[...]).astype(jnp.bfloat16)))
    for rows, (x1, h) in zip(row_slices, mixed):
        acc = jnp.zeros(x1.shape, jnp.float32)
        for c0 in range(0, d_ff, ff_chunk):
            gate = jnp.dot(h, wg_ref[:, c0:c0 + ff_chunk], preferred_element_type=jnp.float32)
            up = jnp.dot(h, wu_ref[:, c0:c0 + ff_chunk], preferred_element_type=jnp.float32)
            act = (jax.nn.silu(gate) * up).astype(jnp.bfloat16)
            acc = acc + jnp.dot(act, wd_ref[c0:c0 + ff_chunk, :], preferred_element_type=jnp.float32)
        o_ref[rows, :] = _rms(x1 + acc, gfin_ref[...])


def _out_ffn(x2d, o_na, o_swa, g_na, g_swa, wo, g_ffn, wg, wu, wd, g_fin):
    n_tok = x2d.shape[0]
    tm, sub_rows, ff_chunk = FFN_ROWS, FFN_SUB_ROWS, FFN_CHUNK
    d_ff = wg.shape[1]
    const = lambda i: (0, 0)
    row = lambda i: (i, 0)
    resident = lambda shape: pl.BlockSpec(shape, const, pipeline_mode=pl.Buffered(1))
    weights = (D_MODEL * D_MODEL + 3 * D_MODEL * d_ff) * 2
    tiles = 2 * 2 * tm * D_MODEL * 4 + 2 * 2 * tm * A_WIDTH * 2
    vmem = weights + tiles + 2 * (6 * sub_rows * D_MODEL * 4 + 4 * sub_rows * ff_chunk * 4)
    return pl.pallas_call(
        functools.partial(_out_ffn_kernel, ff_chunk=ff_chunk, sub_rows=sub_rows),
        out_shape=jax.ShapeDtypeStruct((n_tok, D_MODEL), jnp.float32),
        grid=(n_tok // tm,),
        in_specs=[
            pl.BlockSpec((tm, D_MODEL), row),
            pl.BlockSpec((tm, A_WIDTH), row),
            pl.BlockSpec((tm, B_WIDTH), row),
            pl.BlockSpec((1, A_WIDTH), const),
            pl.BlockSpec((1, B_WIDTH), const),
            resident((D_MODEL, D_MODEL)),
            pl.BlockSpec((1, D_MODEL), const),
            resident((D_MODEL, d_ff)),
            resident((D_MODEL, d_ff)),
            resident((d_ff, D_MODEL)),
            pl.BlockSpec((1, D_MODEL), const),
        ],
        out_specs=pl.BlockSpec((tm, D_MODEL), row),
        compiler_params=pltpu.CompilerParams(
            dimension_semantics=("arbitrary",), vmem_limit_bytes=_vmem_limit(vmem + (8 << 20))),
        name="out_ffn",
    )(x2d, o_na, o_swa, g_na, g_swa, wo, g_ffn, wg, wu, wd, g_fin)


def kernel(x, g_norm_mix, w_in, na_rpb, swa_sink, g_out_na, g_out_swa, w_out, g_norm_ffn, w_gate, w_up, w_down,
           g_final):
    batch, seq, d_model = x.shape
    assert w_in.shape[0] == 1, "the final RMSNorm is fused into the single layer's FFN call"
    assert d_model == D_MODEL and seq % Q_BLK == 0 and seq // GRID_W >= NA_MAX_KEY_ROWS
    assert B_KV_WIDTH == LANES, "the windowed k (and v) heads together fill one lane tile"
    rope = _rope_tables(seq)
    x2d = x.reshape(batch * seq, d_model)
    proj, (wo, wg, wu, wd) = _in_proj(x2d, g_norm_mix[0][None, :], w_in[0], rope, seq,
                                      (w_out[0], w_gate[0], w_up[0], w_down[0]))
    o_na = _na_attn(proj, na_rpb[0].reshape(-1), seq, batch)
    o_swa = _swa_attn(proj, swa_sink[0], seq, batch)
    out = _out_ffn(x2d, o_na, o_swa, g_out_na[0][None, :], g_out_swa[0][None, :], wo,
                   g_norm_ffn[0][None, :], wg, wu, wd, g_final[None, :])
    return out.reshape(batch, seq, d_model)
```

```python
import functools

import numpy as np
import jax
import jax.numpy as jnp
from jax import lax
from jax.experimental import pallas as pl
from jax.experimental.pallas import tpu as pltpu

D_MODEL = 1024
HEAD_DIM = 64
NA_HEADS = 8
SWA_HEADS = 8
SWA_KV_HEADS = 2
A_WIDTH = NA_HEADS * HEAD_DIM
B_WIDTH = SWA_HEADS * HEAD_DIM
B_KV_WIDTH = SWA_KV_HEADS * HEAD_DIM
IN_WIDTH = 3 * A_WIDTH + B_WIDTH + 2 * B_KV_WIDTH
GRID_W = 64
NA_ROWS = 8
NA_COLS = 16
SWA_WINDOW = 128
ROPE_THETA = 500000.0
ROPE_DIM = HEAD_DIM // 4
RMS_EPS = 1e-6
NEG_INF = -1e30
LOG2E = 1.4426950408889634
Q_SCALE = HEAD_DIM ** -0.5 * LOG2E

LANES = 128
VMEM_CAPACITY = 64 * 1024 * 1024

PROJ_WIDTH = 3 * A_WIDTH + B_WIDTH + 2 * SWA_KV_HEADS * LANES
COL_QA, COL_KA, COL_VA = 0, A_WIDTH, 2 * A_WIDTH
COL_QB = 3 * A_WIDTH
COL_KD = COL_QB + B_WIDTH
COL_VD = COL_KD + SWA_KV_HEADS * LANES
W_COL_KB = COL_QB + B_WIDTH

Q_BLK = 128
NA_Q_ROWS = Q_BLK // GRID_W
NA_MAX_KEY_ROWS = NA_ROWS + NA_Q_ROWS - 1
NA_MAX_KEYS = NA_MAX_KEY_ROWS * GRID_W
NA_CLASSES = 5
NA_SKEW, SWA_SKEW = 4, 6
IN_PROJ_ROWS = 1024
FFN_ROWS = 1024
FFN_SUB_ROWS = 512
FFN_CHUNK = 256


def _vmem_limit(nbytes):
    return int(min(VMEM_CAPACITY - (2 << 20), max(nbytes, 16 << 20)))


def _attn_vmem_limit(nbytes):
    assert nbytes + (16 << 20) <= VMEM_CAPACITY - (2 << 20)
    return VMEM_CAPACITY - (2 << 20)


def _na_class_of_block(i, nblk):
    if i < 2:
        return i
    if i >= nblk - 2:
        return NA_CLASSES - (nblk - i)
    return 2


def _na_key_rows(i, rows):
    first = NA_Q_ROWS * i
    lo = int(np.clip(first - NA_ROWS // 2, 0, rows - NA_ROWS))
    hi = int(np.clip(first + NA_Q_ROWS - 1 - NA_ROWS // 2, 0, rows - NA_ROWS)) + NA_ROWS
    return lo, hi


def _na_static_tables(seq):
    rows = seq // GRID_W
    nblk = seq // Q_BLK
    masks = np.full((NA_CLASSES, NA_MAX_KEYS, Q_BLK), NEG_INF, np.float32)
    ridx = np.full((NA_CLASSES, NA_Q_ROWS, NA_MAX_KEY_ROWS), -1, np.int64)
    seen = {}
    for i in range(nblk):
        cls = _na_class_of_block(i, nblk)
        lo, hi = _na_key_rows(i, rows)
        assert (lo * GRID_W) % LANES == 0 and hi - lo <= NA_MAX_KEY_ROWS
        q = np.arange(Q_BLK) + i * Q_BLK
        qr, qc = q // GRID_W, q % GRID_W
        k = np.arange((hi - lo) * GRID_W) + lo * GRID_W
        kr, kc = k // GRID_W, k % GRID_W
        rs = np.clip(qr - NA_ROWS // 2, 0, rows - NA_ROWS)
        cs = np.clip(qc - NA_COLS // 2, 0, GRID_W - NA_COLS)
        ok_r = (kr[:, None] >= rs[None, :]) & (kr[:, None] < rs[None, :] + NA_ROWS)
        ok_c = (kc[:, None] >= cs[None, :]) & (kc[:, None] < cs[None, :] + NA_COLS)
        m = np.full((NA_MAX_KEYS, Q_BLK), NEG_INF, np.float32)
        m[:k.size] = np.where(ok_r & ok_c, 0.0, NEG_INF)
        rr = np.full((NA_Q_ROWS, NA_MAX_KEY_ROWS), -1, np.int64)
        for a in range(NA_Q_ROWS):
            for b in range(hi - lo):
                r, krow = NA_Q_ROWS * i + a, lo + b
                if rs[a * GRID_W] <= krow < rs[a * GRID_W] + NA_ROWS:
                    rr[a, b] = krow - r + NA_ROWS - 1
        if cls in seen:
            assert np.array_equal(seen[cls][0], m) and np.array_equal(seen[cls][1], rr)
        seen[cls] = (m, rr)
        masks[cls], ridx[cls] = m, rr
    return masks, ridx


def _swa_edge_masks():
    assert SWA_WINDOW == Q_BLK
    key = np.arange(Q_BLK)[:, None]
    qry = np.arange(Q_BLK)[None, :]
    tri = np.stack([np.where(key >= qry, 0.0, NEG_INF), np.where(key <= qry, 0.0, NEG_INF)]).astype(np.float32)
    return np.ascontiguousarray(np.concatenate([tri, tri], axis=2))


def _rope_tables(seq):
    half = ROPE_DIM // 2
    inv_freq = np.float32(ROPE_THETA) ** (-np.arange(0, ROPE_DIM, 2, dtype=np.float32) / np.float32(ROPE_DIM))
    ang = np.arange(seq, dtype=np.float32)[:, None] * inv_freq[None, :].astype(np.float32)
    cos, sin = np.cos(ang).astype(np.float32), np.sin(ang).astype(np.float32)
    ones = np.ones((seq, HEAD_DIM - ROPE_DIM), np.float32)
    zeros = np.zeros((seq, HEAD_DIM - ROPE_DIM), np.float32)
    zh = np.zeros((seq, half), np.float32)
    c = np.concatenate([cos, cos, ones], axis=1)
    sa = np.concatenate([-sin, zh, zeros], axis=1)
    sb = np.concatenate([zh, sin, zeros], axis=1)
    rep = LANES // HEAD_DIM
    return tuple(jnp.asarray(np.tile(t, (1, rep))) for t in (c, sa, sb))


def _rms(x, g):
    return x * lax.rsqrt(jnp.mean(x * x, axis=-1, keepdims=True) + RMS_EPS) * g


def _rope_tile(a, c, sa, sb):
    return a * c + pltpu.roll(a, LANES - ROPE_DIM // 2, axis=1) * sa + pltpu.roll(a, ROPE_DIM // 2, axis=1) * sb


def _in_proj_kernel(x_ref, g_ref, w_ref, c_ref, sa_ref, sb_ref, *rest):
    n_extra = (len(rest) - 2) // 2
    extra_in, o_ref, extra_out, w_scr = rest[:n_extra], rest[n_extra], rest[n_extra + 1:-1], rest[-1]

    @pl.when(pl.program_id(0) == 0)
    def _():
        for c0 in range(0, w_ref.shape[1], LANES):
            w_scr[:, c0:c0 + LANES] = w_ref[:, c0:c0 + LANES].astype(w_scr.dtype)

    for src, dst in zip(extra_in, extra_out):
        dst[...] = src[...].astype(dst.dtype)

    h = _rms(x_ref[...], g_ref[...]).astype(jnp.bfloat16)
    c, sa, sb = c_ref[...], sa_ref[...], sb_ref[...]

    def proj(col, width):
        return jnp.dot(h, w_scr[:, col:col + width], preferred_element_type=jnp.float32)

    kv = proj(W_COL_KB, 2 * B_KV_WIDTH)
    left = lax.broadcasted_iota(jnp.int32, (h.shape[0], LANES), 1) < HEAD_DIM
    for col, tile in ((COL_KD, _rope_tile(kv[:, :B_KV_WIDTH], c, sa, sb)), (COL_VD, kv[:, B_KV_WIDTH:])):
        swapped = pltpu.roll(tile, HEAD_DIM, axis=1)
        o_ref[:, col:col + LANES] = jnp.where(left, tile, swapped).astype(o_ref.dtype)
        o_ref[:, col + LANES:col + 2 * LANES] = jnp.where(left, swapped, tile).astype(o_ref.dtype)
    qb = proj(COL_QB, B_WIDTH)
    for t in range(B_WIDTH // LANES):
        lo = t * LANES
        tile = _rope_tile(qb[:, lo:lo + LANES], c, sa, sb) * Q_SCALE
        o_ref[:, COL_QB + lo:COL_QB + lo + LANES] = tile.astype(o_ref.dtype)
    o_ref[:, COL_QA:COL_QA + A_WIDTH] = (proj(COL_QA, A_WIDTH) * Q_SCALE).astype(o_ref.dtype)
    o_ref[:, COL_KA:COL_KA + A_WIDTH] = proj(COL_KA, A_WIDTH).astype(o_ref.dtype)
    o_ref[:, COL_VA:COL_VA + A_WIDTH] = proj(COL_VA, A_WIDTH).astype(o_ref.dtype)


def _in_proj(x2d, g, w_in, rope, seq, later_weights):
    n_tok = x2d.shape[0]
    tm = IN_PROJ_ROWS
    steps = n_tok // tm
    pos_blocks = seq // tm
    const = lambda i: (0, 0)
    row = lambda i: (i, 0)
    pos = lambda i: (i % pos_blocks, 0)
    slices = [(w.shape[0] // steps, w.shape[1]) for w in later_weights]
    assert all(w.shape[0] % steps == 0 and rows % 16 == 0 for w, (rows, _) in zip(later_weights, slices))
    slice_bytes = sum(rows * cols for rows, cols in slices)
    vmem = (2 * tm * D_MODEL * 4 + D_MODEL * IN_WIDTH * (4 + 2) + 2 * tm * PROJ_WIDTH * 2 + 3 * tm * B_WIDTH * 4 * 2
            + 2 * slice_bytes * (4 + 2) + 3 * 2 * tm * LANES * 4)
    outs = pl.pallas_call(
        _in_proj_kernel,
        out_shape=[jax.ShapeDtypeStruct((n_tok, PROJ_WIDTH), jnp.bfloat16)]
        + [jax.ShapeDtypeStruct(w.shape, jnp.bfloat16) for w in later_weights],
        grid=(steps,),
        in_specs=[
            pl.BlockSpec((tm, D_MODEL), row),
            pl.BlockSpec((1, D_MODEL), const),
            pl.BlockSpec((D_MODEL, IN_WIDTH), const, pipeline_mode=pl.Buffered(1)),
            pl.BlockSpec((tm, LANES), pos),
            pl.BlockSpec((tm, LANES), pos),
            pl.BlockSpec((tm, LANES), pos),
        ] + [pl.BlockSpec(blk, row) for blk in slices],
        out_specs=[pl.BlockSpec((tm, PROJ_WIDTH), row)] + [pl.BlockSpec(blk, row) for blk in slices],
        scratch_shapes=[pltpu.VMEM((D_MODEL, IN_WIDTH), jnp.bfloat16)],
        compiler_params=pltpu.CompilerParams(
            dimension_semantics=("arbitrary",), vmem_limit_bytes=_vmem_limit(vmem + (8 << 20))),
        name="in_proj",
    )(x2d, g, w_in, *rope, *later_weights)
    return outs[0], outs[1:]


def _transpose_keys(v_ref, vt_ref, ones_below=False):
    top = lax.broadcasted_iota(jnp.int32, (LANES, LANES), 0) < HEAD_DIM
    for j in range(v_ref.shape[0] // LANES):
        tile = v_ref[j * LANES:(j + 1) * LANES, :].T
        vt_ref[j] = jnp.where(top, tile, jnp.ones_like(tile)) if ones_below else tile


def _attend(q_ref, k_ref, vt_ref, o_ref, blocks, skew, sink_row=None, sums_below=False):
    lane = lax.broadcasted_iota(jnp.int32, (Q_BLK, LANES), 1)
    left = lane < HEAD_DIM
    top = lax.broadcasted_iota(jnp.int32, (LANES, Q_BLK), 0) < HEAD_DIM

    def scores_of(q0, k0, n_keys, add_bias):
        q2 = q_ref[q0:q0 + Q_BLK, :]
        zero = jnp.zeros_like(q2)
        qq = jnp.concatenate([jnp.where(left, q2, zero), jnp.where(left, zero, q2)], axis=0)
        st = lax.dot_general(k_ref[k0:k0 + n_keys, :], qq, (((1,), (1,)), ((), ())),
                             preferred_element_type=jnp.float32)
        return add_bias(st)

    def finish(q0, k0, n_keys, add_bias, st):
        m = jnp.max(st, axis=0, keepdims=True)
        if sink_row is not None:
            m = jnp.maximum(m, sink_row)
        p = jnp.exp2(st - m)
        tile0, full, part = k0 // LANES, n_keys // LANES, n_keys % LANES
        pieces = [vt_ref[tile0 + j] for j in range(full)]
        if part:
            pieces.append(vt_ref[tile0 + full][:, :part])
        ot = jnp.dot(jnp.concatenate(pieces, axis=1), p.astype(jnp.bfloat16), preferred_element_type=jnp.float32)
        den = ot[HEAD_DIM:HEAD_DIM + 1, :] if sums_below else jnp.sum(p, axis=0, keepdims=True)
        if sink_row is not None:
            den = den + jnp.exp2(sink_row - m)
        if sums_below:
            ot = ot[:HEAD_DIM, :] / den
            x = jnp.concatenate([ot[:, :Q_BLK], ot[:, Q_BLK:]], axis=0)
        else:
            ot = ot / den
            x = jnp.where(top, ot[:, :Q_BLK], ot[:, Q_BLK:])
        o_ref[q0:q0 + Q_BLK, :] = x.T.astype(o_ref.dtype)

    pending = [scores_of(*blk) for blk in blocks[:skew]]
    for c, blk in enumerate(blocks):
        st = pending.pop(0)
        if c + skew < len(blocks):
            pending.append(scores_of(*blocks[c + skew]))
        finish(*blk, st)


def _build_na_bias(rpb_ref, mask_ref, t_scr, hp, ridx):
    n_rrow, n_rcol = 2 * NA_ROWS - 1, 2 * NA_COLS - 1
    kc = lax.broadcasted_iota(jnp.int32, (GRID_W, LANES), 0)
    lane = lax.broadcasted_iota(jnp.int32, (GRID_W, LANES), 1)
    diff = kc - (lane % GRID_W) + (NA_COLS - 1)
    second_row = lane >= GRID_W
    used_rows = sorted({int(r) for r in ridx.reshape(-1) if r >= 0})
    zero = jnp.zeros((GRID_W, LANES), jnp.float32)
    for hh in range(2):
        h = 2 * hp + hh
        toeplitz = {-1: zero}
        for rho in used_rows:
            acc = zero
            for d in range(n_rcol):
                acc = jnp.where(diff == d, rpb_ref[(h * n_rrow + rho) * n_rcol + d] * LOG2E, acc)
            toeplitz[rho] = acc
        for cls in range(NA_CLASSES):
            for b in range(NA_MAX_KEY_ROWS):
                blk = jnp.where(second_row, toeplitz[int(ridx[cls, 1, b])], toeplitz[int(ridx[cls, 0, b])])
                ks = slice(b * GRID_W, (b + 1) * GRID_W)
                t_scr[cls, ks, hh * Q_BLK:(hh + 1) * Q_BLK] = blk + mask_ref[cls, ks, :]


def _na_kernel(rpb_ref, q_ref, k_ref, v_ref, mask_ref, o_ref, t_scr, vt_scr, *, ridx, seq):
    nblk = seq // Q_BLK

    @pl.when(pl.program_id(1) == 0)
    def _():
        _build_na_bias(rpb_ref, mask_ref, t_scr, pl.program_id(0), ridx)

    _transpose_keys(v_ref, vt_scr)
    blocks = []
    for i in range(nblk):
        lo, hi = _na_key_rows(i, seq // GRID_W)
        n_keys = (hi - lo) * GRID_W
        add_bias = functools.partial(lambda st, cls, n: st + t_scr[cls, :n, :], cls=_na_class_of_block(i, nblk), n=n_keys)
        blocks.append((i * Q_BLK, lo * GRID_W, n_keys, add_bias))
    _attend(q_ref, k_ref, vt_scr, o_ref, blocks, NA_SKEW)


def _na_attn(proj, rpb_flat, seq, batch):
    masks, ridx = _na_static_tables(seq)
    n_tok = proj.shape[0]
    n_pairs = A_WIDTH // LANES
    col = lambda base: (lambda hp, b: (b, base // LANES + hp))
    table_bytes = NA_CLASSES * NA_MAX_KEYS * 2 * Q_BLK * 4
    vmem = 2 * 4 * seq * LANES * 2 + 2 * masks.size * 4 + table_bytes + seq * LANES * 2
    return pl.pallas_call(
        functools.partial(_na_kernel, ridx=ridx, seq=seq),
        out_shape=jax.ShapeDtypeStruct((n_tok, A_WIDTH), jnp.bfloat16),
        grid=(n_pairs, batch),
        in_specs=[
            pl.BlockSpec(memory_space=pltpu.SMEM),
            pl.BlockSpec((seq, LANES), col(COL_QA)),
            pl.BlockSpec((seq, LANES), col(COL_KA)),
            pl.BlockSpec((seq, LANES), col(COL_VA)),
            pl.BlockSpec(masks.shape, lambda hp, b: (0, 0, 0)),
        ],
        out_specs=pl.BlockSpec((seq, LANES), lambda hp, b: (b, hp)),
        scratch_shapes=[pltpu.VMEM((NA_CLASSES, NA_MAX_KEYS, 2 * Q_BLK), jnp.float32),
                        pltpu.VMEM((seq // LANES, LANES, LANES), jnp.bfloat16)],
        compiler_params=pltpu.CompilerParams(
            dimension_semantics=("arbitrary", "arbitrary"), vmem_limit_bytes=_attn_vmem_limit(vmem)),
        name="na_attn",
    )(rpb_flat, proj, proj, proj, jnp.asarray(masks))


def _swa_add_mask(st, edge_ref, first_tile, n):
    parts = []
    for j in range(st.shape[0] // Q_BLK):
        part = st[j * Q_BLK:(j + 1) * Q_BLK, :]
        if first_tile + j != n:
            part = part + edge_ref[0 if first_tile + j < n else 1]
        parts.append(part)
    return jnp.concatenate(parts, axis=0)


def _swa_kernel(sink_ref, q_ref, k_ref, v_ref, edge_ref, o_ref, vt_scr, *, seq):
    t = pl.program_id(1)
    nblk = seq // Q_BLK
    tiles_per_kv = (B_WIDTH // LANES) // SWA_KV_HEADS

    @pl.when(t % tiles_per_kv == 0)
    def _():
        _transpose_keys(v_ref, vt_scr, ones_below=True)

    lane = lax.broadcasted_iota(jnp.int32, (1, 2 * Q_BLK), 1)
    sink_row = jnp.where(lane < Q_BLK, sink_ref[2 * t], sink_ref[2 * t + 1]) * LOG2E
    blocks = []
    for n in range(nblk):
        first, last = max(n - 1, 0), min(n + 1, nblk - 1)
        add_mask = functools.partial(_swa_add_mask, edge_ref=edge_ref, first_tile=first, n=n)
        blocks.append((n * Q_BLK, first * Q_BLK, (last - first + 1) * Q_BLK, add_mask))
    _attend(q_ref, k_ref, vt_scr, o_ref, blocks, SWA_SKEW, sink_row, sums_below=True)


def _swa_attn(proj, sink, seq, batch):
    edges = _swa_edge_masks()
    n_tok = proj.shape[0]
    n_tiles = B_WIDTH // LANES
    tiles_per_kv = n_tiles // SWA_KV_HEADS
    kv_tile = lambda base: (lambda b, t: (b, base // LANES + t // tiles_per_kv))
    vmem = 2 * 4 * seq * LANES * 2 + 2 * edges.size * 4 + seq * LANES * 2
    return pl.pallas_call(
        functools.partial(_swa_kernel, seq=seq),
        out_shape=jax.ShapeDtypeStruct((n_tok, B_WIDTH), jnp.bfloat16),
        grid=(batch, n_tiles),
        in_specs=[
            pl.BlockSpec(memory_space=pltpu.SMEM),
            pl.BlockSpec((seq, LANES), lambda b, t: (b, COL_QB // LANES + t)),
            pl.BlockSpec((seq, LANES), kv_tile(COL_KD)),
            pl.BlockSpec((seq, LANES), kv_tile(COL_VD)),
            pl.BlockSpec(edges.shape, lambda b, t: (0, 0, 0)),
        ],
        out_specs=pl.BlockSpec((seq, LANES), lambda b, t: (b, t)),
        scratch_shapes=[pltpu.VMEM((seq // LANES, LANES, LANES), jnp.bfloat16)],
        compiler_params=pltpu.CompilerParams(
            dimension_semantics=("arbitrary", "arbitrary"), vmem_limit_bytes=_attn_vmem_limit(vmem)),
        name="swa_attn",
    )(sink, proj, proj, proj, jnp.asarray(edges))


def _out_ffn_kernel(x_ref, oa_ref, ob_ref, gna_ref, gswa_ref, wo_ref, gffn_ref, wg_ref, wu_ref, wd_ref,
                    gfin_ref, o_ref, *, ff_chunk, sub_rows):
    d_ff = wg_ref.shape[1]
    row_slices = [slice(r0, r0 + sub_rows) for r0 in range(0, x_ref.shape[0], sub_rows)]
    mixed = []
    for rows in row_slices:
        na = _rms(oa_ref[rows, :].astype(jnp.float32), gna_ref[...]).astype(jnp.bfloat16)
        nb = _rms(ob_ref[rows, :].astype(jnp.float32), gswa_ref[...]).astype(jnp.bfloat16)
        x1 = (x_ref[rows, :]
              + jnp.dot(na, wo_ref[:A_WIDTH, :], preferred_element_type=jnp.float32)
              + jnp.dot(nb, wo_ref[A_WIDTH:, :], preferred_element_type=jnp.float32))
        mixed.append((x1, _rms(x1, gffn_ref[...]).astype(jnp.bfloat16)))
    for rows, (x1, h) in zip(row_slices, mixed):
        acc = jnp.zeros(x1.shape, jnp.float32)
        for c0 in range(0, d_ff, ff_chunk):
            gate = jnp.dot(h, wg_ref[:, c0:c0 + ff_chunk], preferred_element_type=jnp.float32)
            up = jnp.dot(h, wu_ref[:, c0:c0 + ff_chunk], preferred_element_type=jnp.float32)
            act = (jax.nn.silu(gate) * up).astype(jnp.bfloat16)
            acc = acc + jnp.dot(act, wd_ref[c0:c0 + ff_chunk, :], preferred_element_type=jnp.float32)
        o_ref[rows, :] = _rms(x1 + acc, gfin_ref[...])


def _out_ffn(x2d, o_na, o_swa, g_na, g_swa, wo, g_ffn, wg, wu, wd, g_fin):
    n_tok = x2d.shape[0]
    tm, sub_rows, ff_chunk = FFN_ROWS, FFN_SUB_ROWS, FFN_CHUNK
    d_ff = wg.shape[1]
    const = lambda i: (0, 0)
    row = lambda i: (i, 0)
    resident = lambda shape: pl.BlockSpec(shape, const, pipeline_mode=pl.Buffered(1))
    weights = (D_MODEL * D_MODEL + 3 * D_MODEL * d_ff) * 2
    tiles = 2 * 2 * tm * D_MODEL * 4 + 2 * 2 * tm * A_WIDTH * 2
    vmem = weights + tiles + 2 * (6 * sub_rows * D_MODEL * 4 + 4 * sub_rows * ff_chunk * 4)
    return pl.pallas_call(
        functools.partial(_out_ffn_kernel, ff_chunk=ff_chunk, sub_rows=sub_rows),
        out_shape=jax.ShapeDtypeStruct((n_tok, D_MODEL), jnp.float32),
        grid=(n_tok // tm,),
        in_specs=[
            pl.BlockSpec((tm, D_MODEL), row),
            pl.BlockSpec((tm, A_WIDTH), row),
            pl.BlockSpec((tm, B_WIDTH), row),
            pl.BlockSpec((1, A_WIDTH), const),
            pl.BlockSpec((1, B_WIDTH), const),
            resident((D_MODEL, D_MODEL)),
            pl.BlockSpec((1, D_MODEL), const),
            resident((D_MODEL, d_ff)),
            resident((D_MODEL, d_ff)),
            resident((d_ff, D_MODEL)),
            pl.BlockSpec((1, D_MODEL), const),
        ],
        out_specs=pl.BlockSpec((tm, D_MODEL), row),
        compiler_params=pltpu.CompilerParams(
            dimension_semantics=("arbitrary",), vmem_limit_bytes=_vmem_limit(vmem + (8 << 20))),
        name="out_ffn",
    )(x2d, o_na, o_swa, g_na, g_swa, wo, g_ffn, wg, wu, wd, g_fin)


def kernel(x, g_norm_mix, w_in, na_rpb, swa_sink, g_out_na, g_out_swa, w_out, g_norm_ffn, w_gate, w_up, w_down,
           g_final):
    batch, seq, d_model = x.shape
    assert w_in.shape[0] == 1, "the final RMSNorm is fused into the single layer's FFN call"
    assert d_model == D_MODEL and seq % Q_BLK == 0 and seq // GRID_W >= NA_MAX_KEY_ROWS
    assert B_KV_WIDTH == LANES, "the windowed k (and v) heads together fill one lane tile"
    rope = _rope_tables(seq)
    x2d = x.reshape(batch * seq, d_model)
    proj, (wo, wg, wu, wd) = _in_proj(x2d, g_norm_mix[0][None, :], w_in[0], rope, seq,
                                      (w_out[0], w_gate[0], w_up[0], w_down[0]))
    o_na = _na_attn(proj, na_rpb[0].reshape(-1), seq, batch)
    o_swa = _swa_attn(proj, swa_sink[0], seq, batch)
    out = _out_ffn(x2d, o_na, o_swa, g_out_na[0][None, :], g_out_swa[0][None, :], wo,
                   g_norm_ffn[0][None, :], wg, wu, wd, g_final[None, :])
    return out.reshape(batch, seq, d_model)
```

```python
import functools
from typing import NamedTuple

import numpy as np
import jax
import jax.numpy as jnp
from jax import lax
from jax.experimental import pallas as pl
from jax.experimental.pallas import tpu as pltpu

D_MODEL = 1024
HEAD_DIM = 64
NA_HEADS = 8
SWA_HEADS = 8
SWA_KV_HEADS = 2
A_WIDTH = NA_HEADS * HEAD_DIM
B_WIDTH = SWA_HEADS * HEAD_DIM
B_KV_WIDTH = SWA_KV_HEADS * HEAD_DIM
IN_WIDTH = 3 * A_WIDTH + B_WIDTH + 2 * B_KV_WIDTH
GRID_W = 64
NA_ROWS = 8
NA_COLS = 16
SWA_WINDOW = 128
ROPE_THETA = 500000.0
ROPE_DIM = HEAD_DIM // 4
RMS_EPS = 1e-6
NEG_INF = -1e30
LOG2E = 1.4426950408889634
Q_SCALE = HEAD_DIM ** -0.5 * LOG2E

LANES = 128
VMEM_CAPACITY = 64 * 1024 * 1024

PROJ_WIDTH = 3 * A_WIDTH + B_WIDTH + 2 * SWA_KV_HEADS * LANES
COL_QA, COL_KA, COL_VA = 0, A_WIDTH, 2 * A_WIDTH
COL_QB = 3 * A_WIDTH
COL_KD = COL_QB + B_WIDTH
COL_VD = COL_KD + SWA_KV_HEADS * LANES
W_COL_KB = COL_QB + B_WIDTH

Q_BLK = 128
NA_Q_ROWS = Q_BLK // GRID_W
NA_MAX_KEY_ROWS = NA_ROWS + NA_Q_ROWS - 1
NA_MAX_KEYS = NA_MAX_KEY_ROWS * GRID_W
NA_CLASSES = 5
ATTN_SKEW = 6
IN_PROJ_ROWS = 1024
FFN_ROWS = 1024
FFN_SUB_ROWS = 512
FFN_CHUNK = 256


def _vmem_limit(nbytes):
    return int(min(VMEM_CAPACITY - (2 << 20), max(nbytes, 16 << 20)))


def _attn_vmem_limit(nbytes):
    assert nbytes + (16 << 20) <= VMEM_CAPACITY - (2 << 20)
    return VMEM_CAPACITY - (2 << 20)


def _na_class_of_block(i, nblk):
    if i < 2:
        return i
    if i >= nblk - 2:
        return NA_CLASSES - (nblk - i)
    return 2


def _na_key_rows(i, rows):
    first = NA_Q_ROWS * i
    lo = int(np.clip(first - NA_ROWS // 2, 0, rows - NA_ROWS))
    hi = int(np.clip(first + NA_Q_ROWS - 1 - NA_ROWS // 2, 0, rows - NA_ROWS)) + NA_ROWS
    return lo, hi


def _na_static_tables(seq):
    rows = seq // GRID_W
    nblk = seq // Q_BLK
    masks = np.full((NA_CLASSES, NA_MAX_KEYS, Q_BLK), NEG_INF, np.float32)
    ridx = np.full((NA_CLASSES, NA_Q_ROWS, NA_MAX_KEY_ROWS), -1, np.int64)
    seen = {}
    for i in range(nblk):
        cls = _na_class_of_block(i, nblk)
        lo, hi = _na_key_rows(i, rows)
        assert (lo * GRID_W) % LANES == 0 and hi - lo <= NA_MAX_KEY_ROWS
        q = np.arange(Q_BLK) + i * Q_BLK
        qr, qc = q // GRID_W, q % GRID_W
        k = np.arange((hi - lo) * GRID_W) + lo * GRID_W
        kr, kc = k // GRID_W, k % GRID_W
        rs = np.clip(qr - NA_ROWS // 2, 0, rows - NA_ROWS)
        cs = np.clip(qc - NA_COLS // 2, 0, GRID_W - NA_COLS)
        ok_r = (kr[:, None] >= rs[None, :]) & (kr[:, None] < rs[None, :] + NA_ROWS)
        ok_c = (kc[:, None] >= cs[None, :]) & (kc[:, None] < cs[None, :] + NA_COLS)
        m = np.full((NA_MAX_KEYS, Q_BLK), NEG_INF, np.float32)
        m[:k.size] = np.where(ok_r & ok_c, 0.0, NEG_INF)
        rr = np.full((NA_Q_ROWS, NA_MAX_KEY_ROWS), -1, np.int64)
        for a in range(NA_Q_ROWS):
            for b in range(hi - lo):
                r, krow = NA_Q_ROWS * i + a, lo + b
                if rs[a * GRID_W] <= krow < rs[a * GRID_W] + NA_ROWS:
                    rr[a, b] = krow - r + NA_ROWS - 1
        if cls in seen:
            assert np.array_equal(seen[cls][0], m) and np.array_equal(seen[cls][1], rr)
        seen[cls] = (m, rr)
        masks[cls], ridx[cls] = m, rr
    return masks, ridx


def _swa_edge_masks():
    assert SWA_WINDOW == Q_BLK
    key = np.arange(Q_BLK)[:, None]
    qry = np.arange(Q_BLK)[None, :]
    tri = np.stack([np.where(key >= qry, 0.0, NEG_INF), np.where(key <= qry, 0.0, NEG_INF)]).astype(np.float32)
    return np.ascontiguousarray(np.concatenate([tri, tri], axis=2))


def _rope_tables(seq):
    half = ROPE_DIM // 2
    inv_freq = np.float32(ROPE_THETA) ** (-np.arange(0, ROPE_DIM, 2, dtype=np.float32) / np.float32(ROPE_DIM))
    ang = np.arange(seq, dtype=np.float32)[:, None] * inv_freq[None, :].astype(np.float32)
    cos, sin = np.cos(ang).astype(np.float32), np.sin(ang).astype(np.float32)
    ones = np.ones((seq, HEAD_DIM - ROPE_DIM), np.float32)
    zeros = np.zeros((seq, HEAD_DIM - ROPE_DIM), np.float32)
    zh = np.zeros((seq, half), np.float32)
    c = np.concatenate([cos, cos, ones], axis=1)
    sa = np.concatenate([-sin, zh, zeros], axis=1)
    sb = np.concatenate([zh, sin, zeros], axis=1)
    rep = LANES // HEAD_DIM
    return tuple(jnp.asarray(np.tile(t, (1, rep))) for t in (c, sa, sb))


def _rms(x, g):
    return x * lax.rsqrt(jnp.mean(x * x, axis=-1, keepdims=True) + RMS_EPS) * g


def _rope_tile(a, c, sa, sb):
    return a * c + pltpu.roll(a, LANES - ROPE_DIM // 2, axis=1) * sa + pltpu.roll(a, ROPE_DIM // 2, axis=1) * sb


def _in_proj_kernel(x_ref, g_ref, w_ref, c_ref, sa_ref, sb_ref, *rest):
    n_extra = (len(rest) - 2) // 2
    extra_in, o_ref, extra_out, w_scr = rest[:n_extra], rest[n_extra], rest[n_extra + 1:-1], rest[-1]

    @pl.when(pl.program_id(0) == 0)
    def _():
        for c0 in range(0, w_ref.shape[1], LANES):
            w_scr[:, c0:c0 + LANES] = w_ref[:, c0:c0 + LANES].astype(w_scr.dtype)

    for src, dst in zip(extra_in, extra_out):
        dst[...] = src[...].astype(dst.dtype)

    h = _rms(x_ref[...], g_ref[...]).astype(jnp.bfloat16)
    c, sa, sb = c_ref[...], sa_ref[...], sb_ref[...]

    def proj(col, width):
        return jnp.dot(h, w_scr[:, col:col + width], preferred_element_type=jnp.float32)

    kv = proj(W_COL_KB, 2 * B_KV_WIDTH)
    left = lax.broadcasted_iota(jnp.int32, (h.shape[0], LANES), 1) < HEAD_DIM
    for col, tile in ((COL_KD, _rope_tile(kv[:, :B_KV_WIDTH], c, sa, sb)), (COL_VD, kv[:, B_KV_WIDTH:])):
        swapped = pltpu.roll(tile, HEAD_DIM, axis=1)
        o_ref[:, col:col + LANES] = jnp.where(left, tile, swapped).astype(o_ref.dtype)
        o_ref[:, col + LANES:col + 2 * LANES] = jnp.where(left, swapped, tile).astype(o_ref.dtype)
    qb = proj(COL_QB, B_WIDTH)
    for t in range(B_WIDTH // LANES):
        lo = t * LANES
        tile = _rope_tile(qb[:, lo:lo + LANES], c, sa, sb) * Q_SCALE
        o_ref[:, COL_QB + lo:COL_QB + lo + LANES] = tile.astype(o_ref.dtype)
    o_ref[:, COL_QA:COL_QA + A_WIDTH] = (proj(COL_QA, A_WIDTH) * Q_SCALE).astype(o_ref.dtype)
    o_ref[:, COL_KA:COL_KA + A_WIDTH] = proj(COL_KA, A_WIDTH).astype(o_ref.dtype)
    o_ref[:, COL_VA:COL_VA + A_WIDTH] = proj(COL_VA, A_WIDTH).astype(o_ref.dtype)


def _in_proj(x2d, g, w_in, rope, seq, later_weights):
    n_tok = x2d.shape[0]
    tm = IN_PROJ_ROWS
    steps = n_tok // tm
    pos_blocks = seq // tm
    const = lambda i: (0, 0)
    row = lambda i: (i, 0)
    pos = lambda i: (i % pos_blocks, 0)
    slices = [(w.shape[0] // steps, w.shape[1]) for w in later_weights]
    assert all(w.shape[0] % steps == 0 and rows % 16 == 0 for w, (rows, _) in zip(later_weights, slices))
    slice_bytes = sum(rows * cols for rows, cols in slices)
    vmem = (2 * tm * D_MODEL * 4 + D_MODEL * IN_WIDTH * (4 + 2) + 2 * tm * PROJ_WIDTH * 2 + 3 * tm * B_WIDTH * 4 * 2
            + 2 * slice_bytes * (4 + 2) + 3 * 2 * tm * LANES * 4)
    outs = pl.pallas_call(
        _in_proj_kernel,
        out_shape=[jax.ShapeDtypeStruct((n_tok, PROJ_WIDTH), jnp.bfloat16)]
        + [jax.ShapeDtypeStruct(w.shape, jnp.bfloat16) for w in later_weights],
        grid=(steps,),
        in_specs=[
            pl.BlockSpec((tm, D_MODEL), row),
            pl.BlockSpec((1, D_MODEL), const),
            pl.BlockSpec((D_MODEL, IN_WIDTH), const, pipeline_mode=pl.Buffered(1)),
            pl.BlockSpec((tm, LANES), pos),
            pl.BlockSpec((tm, LANES), pos),
            pl.BlockSpec((tm, LANES), pos),
        ] + [pl.BlockSpec(blk, row) for blk in slices],
        out_specs=[pl.BlockSpec((tm, PROJ_WIDTH), row)] + [pl.BlockSpec(blk, row) for blk in slices],
        scratch_shapes=[pltpu.VMEM((D_MODEL, IN_WIDTH), jnp.bfloat16)],
        compiler_params=pltpu.CompilerParams(
            dimension_semantics=("arbitrary",), vmem_limit_bytes=_vmem_limit(vmem + (8 << 20))),
        name="in_proj",
    )(x2d, g, w_in, *rope, *later_weights)
    return outs[0], outs[1:]


def _transpose_keys(v_ref, vt_ref, ones_below=False):
    top = lax.broadcasted_iota(jnp.int32, (LANES, LANES), 0) < HEAD_DIM
    for j in range(v_ref.shape[0] // LANES):
        tile = v_ref[j * LANES:(j + 1) * LANES, :].T
        vt_ref[j] = jnp.where(top, tile, jnp.ones_like(tile)) if ones_below else tile


class _Stream(NamedTuple):
    q_ref: object
    k_ref: object
    vt_ref: object
    o_ref: object
    sink_row: object = None
    sums_below: bool = False


class _Block(NamedTuple):
    stream: _Stream
    q0: int
    k0: int
    n_keys: int
    add_bias: object


def _attend(blocks, skew):
    lane = lax.broadcasted_iota(jnp.int32, (Q_BLK, LANES), 1)
    left = lane < HEAD_DIM
    top = lax.broadcasted_iota(jnp.int32, (LANES, Q_BLK), 0) < HEAD_DIM

    def scores_of(blk):
        q2 = blk.stream.q_ref[blk.q0:blk.q0 + Q_BLK, :]
        zero = jnp.zeros_like(q2)
        qq = jnp.concatenate([jnp.where(left, q2, zero), jnp.where(left, zero, q2)], axis=0)
        st = lax.dot_general(blk.stream.k_ref[blk.k0:blk.k0 + blk.n_keys, :], qq, (((1,), (1,)), ((), ())),
                             preferred_element_type=jnp.float32)
        return blk.add_bias(st)

    def finish(blk, st):
        _, _, vt_ref, o_ref, sink_row, sums_below = blk.stream
        m = jnp.max(st, axis=0, keepdims=True)
        if sink_row is not None:
            m = jnp.maximum(m, sink_row)
        p = jnp.exp2(st - m)
        tile0, full, part = blk.k0 // LANES, blk.n_keys // LANES, blk.n_keys % LANES
        pieces = [vt_ref[tile0 + j] for j in range(full)]
        if part:
            pieces.append(vt_ref[tile0 + full][:, :part])
        ot = jnp.dot(jnp.concatenate(pieces, axis=1), p.astype(jnp.bfloat16), preferred_element_type=jnp.float32)
        den = ot[HEAD_DIM:HEAD_DIM + 1, :] if sums_below else jnp.sum(p, axis=0, keepdims=True)
        if sink_row is not None:
            den = den + jnp.exp2(sink_row - m)
        if sums_below:
            ot = ot[:HEAD_DIM, :] / den
            x = jnp.concatenate([ot[:, :Q_BLK], ot[:, Q_BLK:]], axis=0)
        else:
            ot = ot / den
            x = jnp.where(top, ot[:, :Q_BLK], ot[:, Q_BLK:])
        o_ref[blk.q0:blk.q0 + Q_BLK, :] = x.T.astype(o_ref.dtype)

    pending = [scores_of(blk) for blk in blocks[:skew]]
    for c, blk in enumerate(blocks):
        st = pending.pop(0)
        if c + skew < len(blocks):
            pending.append(scores_of(blocks[c + skew]))
        finish(blk, st)


def _build_na_bias(rpb_ref, mask_ref, t_scr, hp, ridx):
    n_rrow, n_rcol = 2 * NA_ROWS - 1, 2 * NA_COLS - 1
    kc = lax.broadcasted_iota(jnp.int32, (GRID_W, LANES), 0)
    lane = lax.broadcasted_iota(jnp.int32, (GRID_W, LANES), 1)
    diff = kc - (lane % GRID_W) + (NA_COLS - 1)
    second_row = lane >= GRID_W
    used_rows = sorted({int(r) for r in ridx.reshape(-1) if r >= 0})
    zero = jnp.zeros((GRID_W, LANES), jnp.float32)
    for hh in range(2):
        h = 2 * hp + hh
        toeplitz = {-1: zero}
        for rho in used_rows:
            acc = zero
            for d in range(n_rcol):
                acc = jnp.where(diff == d, rpb_ref[(h * n_rrow + rho) * n_rcol + d] * LOG2E, acc)
            toeplitz[rho] = acc
        for cls in range(NA_CLASSES):
            for b in range(NA_MAX_KEY_ROWS):
                blk = jnp.where(second_row, toeplitz[int(ridx[cls, 1, b])], toeplitz[int(ridx[cls, 0, b])])
                ks = slice(b * GRID_W, (b + 1) * GRID_W)
                t_scr[cls, ks, hh * Q_BLK:(hh + 1) * Q_BLK] = blk + mask_ref[cls, ks, :]


def _swa_add_mask(st, edge_ref, first_tile, n):
    parts = []
    for j in range(st.shape[0] // Q_BLK):
        part = st[j * Q_BLK:(j + 1) * Q_BLK, :]
        if first_tile + j != n:
            part = part + edge_ref[0 if first_tile + j < n else 1]
        parts.append(part)
    return jnp.concatenate(parts, axis=0)


def _attn_kernel(rpb_ref, sink_ref, qa_ref, ka_ref, va_ref, qb_ref, kd_ref, vd_ref, mask_ref, edge_ref,
                 oa_ref, ob_ref, t_scr, vta_scr, vtb_scr, *, ridx, seq):
    pair = pl.program_id(0)
    nblk = seq // Q_BLK

    @pl.when(pl.program_id(1) == 0)
    def _():
        _build_na_bias(rpb_ref, mask_ref, t_scr, pair, ridx)

    _transpose_keys(va_ref, vta_scr)
    _transpose_keys(vd_ref, vtb_scr, ones_below=True)
    lane = lax.broadcasted_iota(jnp.int32, (1, 2 * Q_BLK), 1)
    sink_row = jnp.where(lane < Q_BLK, sink_ref[2 * pair], sink_ref[2 * pair + 1]) * LOG2E
    na = _Stream(qa_ref, ka_ref, vta_scr, oa_ref)
    swa = _Stream(qb_ref, kd_ref, vtb_scr, ob_ref, sink_row, sums_below=True)
    blocks = []
    for i in range(nblk):
        lo, hi = _na_key_rows(i, seq // GRID_W)
        n_keys = (hi - lo) * GRID_W
        add_bias = functools.partial(lambda st, cls, n: st + t_scr[cls, :n, :], cls=_na_class_of_block(i, nblk), n=n_keys)
        blocks.append(_Block(na, i * Q_BLK, lo * GRID_W, n_keys, add_bias))
        first, last = max(i - 1, 0), min(i + 1, nblk - 1)
        add_mask = functools.partial(_swa_add_mask, edge_ref=edge_ref, first_tile=first, n=i)
        blocks.append(_Block(swa, i * Q_BLK, first * Q_BLK, (last - first + 1) * Q_BLK, add_mask))
    _attend(blocks, ATTN_SKEW)


def _attention(proj, rpb_flat, sink, seq, batch):
    masks, ridx = _na_static_tables(seq)
    edges = _swa_edge_masks()
    n_tok = proj.shape[0]
    n_pairs = A_WIDTH // LANES
    assert B_WIDTH // LANES == n_pairs
    pairs_per_kv = n_pairs // SWA_KV_HEADS
    col = lambda base: (lambda p, b: (b, base // LANES + p))
    kv_col = lambda base: (lambda p, b: (b, base // LANES + p // pairs_per_kv))
    tile = lambda index_map: pl.BlockSpec((seq, LANES), index_map)
    whole = lambda a: pl.BlockSpec(a.shape, lambda p, b: (0,) * a.ndim)
    table_bytes = NA_CLASSES * NA_MAX_KEYS * 2 * Q_BLK * 4
    vmem = 2 * 8 * seq * LANES * 2 + 2 * (masks.size + edges.size) * 4 + table_bytes + 2 * seq * LANES * 2
    out = jax.ShapeDtypeStruct((n_tok, A_WIDTH), jnp.bfloat16)
    return pl.pallas_call(
        functools.partial(_attn_kernel, ridx=ridx, seq=seq),
        out_shape=[out, out],
        grid=(n_pairs, batch),
        in_specs=[
            pl.BlockSpec(memory_space=pltpu.SMEM),
            pl.BlockSpec(memory_space=pltpu.SMEM),
            tile(col(COL_QA)), tile(col(COL_KA)), tile(col(COL_VA)),
            tile(col(COL_QB)), tile(kv_col(COL_KD)), tile(kv_col(COL_VD)),
            whole(masks), whole(edges),
        ],
        out_specs=[tile(lambda p, b: (b, p)), tile(lambda p, b: (b, p))],
        scratch_shapes=[pltpu.VMEM((NA_CLASSES, NA_MAX_KEYS, 2 * Q_BLK), jnp.float32),
                        pltpu.VMEM((seq // LANES, LANES, LANES), jnp.bfloat16),
                        pltpu.VMEM((seq // LANES, LANES, LANES), jnp.bfloat16)],
        compiler_params=pltpu.CompilerParams(
            dimension_semantics=("arbitrary", "arbitrary"), vmem_limit_bytes=_attn_vmem_limit(vmem)),
        name="attention",
    )(rpb_flat, sink, proj, proj, proj, proj, proj, proj, jnp.asarray(masks), jnp.asarray(edges))


def _out_ffn_kernel(x_ref, oa_ref, ob_ref, gna_ref, gswa_ref, wo_ref, gffn_ref, wg_ref, wu_ref, wd_ref,
                    gfin_ref, o_ref, *, ff_chunk, sub_rows):
    d_ff = wg_ref.shape[1]
    row_slices = [slice(r0, r0 + sub_rows) for r0 in range(0, x_ref.shape[0], sub_rows)]
    mixed = []
    for rows in row_slices:
        na = _rms(oa_ref[rows, :].astype(jnp.float32), gna_ref[...]).astype(jnp.bfloat16)
        nb = _rms(ob_ref[rows, :].astype(jnp.float32), gswa_ref[...]).astype(jnp.bfloat16)
        x1 = (x_ref[rows, :]
              + jnp.dot(na, wo_ref[:A_WIDTH, :], preferred_element_type=jnp.float32)
              + jnp.dot(nb, wo_ref[A_WIDTH:, :], preferred_element_type=jnp.float32))
        mixed.append((x1, _rms(x1, gffn_ref[...]).astype(jnp.bfloat16)))
    for rows, (x1, h) in zip(row_slices, mixed):
        acc = jnp.zeros(x1.shape, jnp.float32)
        for c0 in range(0, d_ff, ff_chunk):
            gate = jnp.dot(h, wg_ref[:, c0:c0 + ff_chunk], preferred_element_type=jnp.float32)
            up = jnp.dot(h, wu_ref[:, c0:c0 + ff_chunk], preferred_element_type=jnp.float32)
            act = (jax.nn.silu(gate) * up).astype(jnp.bfloat16)
            acc = acc + jnp.dot(act, wd_ref[c0:c0 + ff_chunk, :], preferred_element_type=jnp.float32)
        o_ref[rows, :] = _rms(x1 + acc, gfin_ref[...])


def _out_ffn(x2d, o_na, o_swa, g_na, g_swa, wo, g_ffn, wg, wu, wd, g_fin):
    n_tok = x2d.shape[0]
    tm, sub_rows, ff_chunk = FFN_ROWS, FFN_SUB_ROWS, FFN_CHUNK
    d_ff = wg.shape[1]
    const = lambda i: (0, 0)
    row = lambda i: (i, 0)
    resident = lambda shape: pl.BlockSpec(shape, const, pipeline_mode=pl.Buffered(1))
    weights = (D_MODEL * D_MODEL + 3 * D_MODEL * d_ff) * 2
    tiles = 2 * 2 * tm * D_MODEL * 4 + 2 * 2 * tm * A_WIDTH * 2
    vmem = weights + tiles + 2 * (6 * sub_rows * D_MODEL * 4 + 4 * sub_rows * ff_chunk * 4)
    return pl.pallas_call(
        functools.partial(_out_ffn_kernel, ff_chunk=ff_chunk, sub_rows=sub_rows),
        out_shape=jax.ShapeDtypeStruct((n_tok, D_MODEL), jnp.float32),
        grid=(n_tok // tm,),
        in_specs=[
            pl.BlockSpec((tm, D_MODEL), row),
            pl.BlockSpec((tm, A_WIDTH), row),
            pl.BlockSpec((tm, B_WIDTH), row),
            pl.BlockSpec((1, A_WIDTH), const),
            pl.BlockSpec((1, B_WIDTH), const),
            resident((D_MODEL, D_MODEL)),
            pl.BlockSpec((1, D_MODEL), const),
            resident((D_MODEL, d_ff)),
            resident((D_MODEL, d_ff)),
            resident((d_ff, D_MODEL)),
            pl.BlockSpec((1, D_MODEL), const),
        ],
        out_specs=pl.BlockSpec((tm, D_MODEL), row),
        compiler_params=pltpu.CompilerParams(
            dimension_semantics=("arbitrary",), vmem_limit_bytes=_vmem_limit(vmem + (8 << 20))),
        name="out_ffn",
    )(x2d, o_na, o_swa, g_na, g_swa, wo, g_ffn, wg, wu, wd, g_fin)


def kernel(x, g_norm_mix, w_in, na_rpb, swa_sink, g_out_na, g_out_swa, w_out, g_norm_ffn, w_gate, w_up, w_down,
           g_final):
    batch, seq, d_model = x.shape
    assert w_in.shape[0] == 1, "the final RMSNorm is fused into the single layer's FFN call"
    assert d_model == D_MODEL and seq % Q_BLK == 0 and seq // GRID_W >= NA_MAX_KEY_ROWS
    assert B_KV_WIDTH == LANES, "the windowed k (and v) heads together fill one lane tile"
    rope = _rope_tables(seq)
    x2d = x.reshape(batch * seq, d_model)
    proj, (wo, wg, wu, wd) = _in_proj(x2d, g_norm_mix[0][None, :], w_in[0], rope, seq,
                                      (w_out[0], w_gate[0], w_up[0], w_down[0]))
    o_na, o_swa = _attention(proj, na_rpb[0].reshape(-1), swa_sink[0], seq, batch)
    out = _out_ffn(x2d, o_na, o_swa, g_out_na[0][None, :], g_out_swa[0][None, :], wo,
                   g_norm_ffn[0][None, :], wg, wu, wd, g_final[None, :])
    return out.reshape(batch, seq, d_model)
```

```python
import functools
from typing import NamedTuple

import numpy as np
import jax
import jax.numpy as jnp
from jax import lax
from jax.experimental import pallas as pl
from jax.experimental.pallas import tpu as pltpu

D_MODEL = 1024
HEAD_DIM = 64
NA_HEADS = 8
SWA_HEADS = 8
SWA_KV_HEADS = 2
A_WIDTH = NA_HEADS * HEAD_DIM
B_WIDTH = SWA_HEADS * HEAD_DIM
B_KV_WIDTH = SWA_KV_HEADS * HEAD_DIM
IN_WIDTH = 3 * A_WIDTH + B_WIDTH + 2 * B_KV_WIDTH
GRID_W = 64
NA_ROWS = 8
NA_COLS = 16
SWA_WINDOW = 128
ROPE_THETA = 500000.0
ROPE_DIM = HEAD_DIM // 4
RMS_EPS = 1e-6
NEG_INF = -1e30
LOG2E = 1.4426950408889634
Q_SCALE = HEAD_DIM ** -0.5 * LOG2E

LANES = 128
VMEM_CAPACITY = 64 * 1024 * 1024

PROJ_WIDTH = 3 * A_WIDTH + B_WIDTH + 2 * SWA_KV_HEADS * LANES
COL_QA, COL_KA, COL_VA = 0, A_WIDTH, 2 * A_WIDTH
COL_QB = 3 * A_WIDTH
COL_KD = COL_QB + B_WIDTH
COL_VD = COL_KD + SWA_KV_HEADS * LANES
W_COL_KB = COL_QB + B_WIDTH

Q_BLK = 128
NA_Q_ROWS = Q_BLK // GRID_W
NA_MAX_KEY_ROWS = NA_ROWS + NA_Q_ROWS - 1
NA_MAX_KEYS = NA_MAX_KEY_ROWS * GRID_W
NA_CLASSES = 5
ATTN_SKEW = 6
IN_PROJ_ROWS = 1024
FFN_ROWS = 1024
FFN_SUB_ROWS = 512
FFN_CHUNK = 256


def _vmem_limit(nbytes):
    return int(min(VMEM_CAPACITY - (2 << 20), max(nbytes, 16 << 20)))


def _attn_vmem_limit(nbytes):
    assert nbytes + (16 << 20) <= VMEM_CAPACITY - (2 << 20)
    return VMEM_CAPACITY - (2 << 20)


def _na_class_of_block(i, nblk):
    if i < 2:
        return i
    if i >= nblk - 2:
        return NA_CLASSES - (nblk - i)
    return 2


def _na_key_rows(i, rows):
    first = NA_Q_ROWS * i
    lo = int(np.clip(first - NA_ROWS // 2, 0, rows - NA_ROWS))
    hi = int(np.clip(first + NA_Q_ROWS - 1 - NA_ROWS // 2, 0, rows - NA_ROWS)) + NA_ROWS
    return lo, hi


def _na_static_tables(seq):
    rows = seq // GRID_W
    nblk = seq // Q_BLK
    masks = np.full((NA_CLASSES, NA_MAX_KEYS, Q_BLK), NEG_INF, np.float32)
    ridx = np.full((NA_CLASSES, NA_Q_ROWS, NA_MAX_KEY_ROWS), -1, np.int64)
    seen = {}
    for i in range(nblk):
        cls = _na_class_of_block(i, nblk)
        lo, hi = _na_key_rows(i, rows)
        assert (lo * GRID_W) % LANES == 0 and hi - lo <= NA_MAX_KEY_ROWS
        q = np.arange(Q_BLK) + i * Q_BLK
        qr, qc = q // GRID_W, q % GRID_W
        k = np.arange((hi - lo) * GRID_W) + lo * GRID_W
        kr, kc = k // GRID_W, k % GRID_W
        rs = np.clip(qr - NA_ROWS // 2, 0, rows - NA_ROWS)
        cs = np.clip(qc - NA_COLS // 2, 0, GRID_W - NA_COLS)
        ok_r = (kr[:, None] >= rs[None, :]) & (kr[:, None] < rs[None, :] + NA_ROWS)
        ok_c = (kc[:, None] >= cs[None, :]) & (kc[:, None] < cs[None, :] + NA_COLS)
        m = np.full((NA_MAX_KEYS, Q_BLK), NEG_INF, np.float32)
        m[:k.size] = np.where(ok_r & ok_c, 0.0, NEG_INF)
        rr = np.full((NA_Q_ROWS, NA_MAX_KEY_ROWS), -1, np.int64)
        for a in range(NA_Q_ROWS):
            for b in range(hi - lo):
                r, krow = NA_Q_ROWS * i + a, lo + b
                if rs[a * GRID_W] <= krow < rs[a * GRID_W] + NA_ROWS:
                    rr[a, b] = krow - r + NA_ROWS - 1
        if cls in seen:
            assert np.array_equal(seen[cls][0], m) and np.array_equal(seen[cls][1], rr)
        seen[cls] = (m, rr)
        masks[cls], ridx[cls] = m, rr
    return masks, ridx


def _swa_edge_masks():
    assert SWA_WINDOW == Q_BLK
    key = np.arange(Q_BLK)[:, None]
    qry = np.arange(Q_BLK)[None, :]
    tri = np.stack([np.where(key >= qry, 0.0, NEG_INF), np.where(key <= qry, 0.0, NEG_INF)]).astype(np.float32)
    return np.ascontiguousarray(np.concatenate([tri, tri], axis=2))


def _rope_tables(seq):
    half = ROPE_DIM // 2
    inv_freq = np.float32(ROPE_THETA) ** (-np.arange(0, ROPE_DIM, 2, dtype=np.float32) / np.float32(ROPE_DIM))
    ang = np.arange(seq, dtype=np.float32)[:, None] * inv_freq[None, :].astype(np.float32)
    cos, sin = np.cos(ang).astype(np.float32), np.sin(ang).astype(np.float32)
    ones = np.ones((seq, HEAD_DIM - ROPE_DIM), np.float32)
    zeros = np.zeros((seq, HEAD_DIM - ROPE_DIM), np.float32)
    zh = np.zeros((seq, half), np.float32)
    c = np.concatenate([cos, cos, ones], axis=1)
    sa = np.concatenate([-sin, zh, zeros], axis=1)
    sb = np.concatenate([zh, sin, zeros], axis=1)
    rep = LANES // HEAD_DIM
    return tuple(jnp.asarray(np.tile(t, (1, rep))) for t in (c, sa, sb))


def _rms(x, g):
    return x * lax.rsqrt(jnp.mean(x * x, axis=-1, keepdims=True) + RMS_EPS) * g


def _rope_tile(a, c, sa, sb):
    return a * c + pltpu.roll(a, LANES - ROPE_DIM // 2, axis=1) * sa + pltpu.roll(a, ROPE_DIM // 2, axis=1) * sb


def _in_proj_kernel(x_ref, g_ref, w_ref, c_ref, sa_ref, sb_ref, o_ref, w_scr):
    @pl.when(pl.program_id(0) == 0)
    def _():
        for c0 in range(0, w_ref.shape[1], LANES):
            w_scr[:, c0:c0 + LANES] = w_ref[:, c0:c0 + LANES].astype(w_scr.dtype)

    h = _rms(x_ref[...], g_ref[...]).astype(jnp.bfloat16)
    c, sa, sb = c_ref[...], sa_ref[...], sb_ref[...]

    def proj(col, width):
        return jnp.dot(h, w_scr[:, col:col + width], preferred_element_type=jnp.float32)

    kv = proj(W_COL_KB, 2 * B_KV_WIDTH)
    left = lax.broadcasted_iota(jnp.int32, (h.shape[0], LANES), 1) < HEAD_DIM
    for col, tile in ((COL_KD, _rope_tile(kv[:, :B_KV_WIDTH], c, sa, sb)), (COL_VD, kv[:, B_KV_WIDTH:])):
        swapped = pltpu.roll(tile, HEAD_DIM, axis=1)
        o_ref[:, col:col + LANES] = jnp.where(left, tile, swapped).astype(o_ref.dtype)
        o_ref[:, col + LANES:col + 2 * LANES] = jnp.where(left, swapped, tile).astype(o_ref.dtype)
    qb = proj(COL_QB, B_WIDTH)
    for t in range(B_WIDTH // LANES):
        lo = t * LANES
        tile = _rope_tile(qb[:, lo:lo + LANES], c, sa, sb) * Q_SCALE
        o_ref[:, COL_QB + lo:COL_QB + lo + LANES] = tile.astype(o_ref.dtype)
    o_ref[:, COL_QA:COL_QA + A_WIDTH] = (proj(COL_QA, A_WIDTH) * Q_SCALE).astype(o_ref.dtype)
    o_ref[:, COL_KA:COL_KA + A_WIDTH] = proj(COL_KA, A_WIDTH).astype(o_ref.dtype)
    o_ref[:, COL_VA:COL_VA + A_WIDTH] = proj(COL_VA, A_WIDTH).astype(o_ref.dtype)


def _in_proj(x2d, g, w_in, rope, seq):
    n_tok = x2d.shape[0]
    tm = IN_PROJ_ROWS
    pos_blocks = seq // tm
    const = lambda i: (0, 0)
    row = lambda i: (i, 0)
    pos = lambda i: (i % pos_blocks, 0)
    vmem = (2 * tm * D_MODEL * 4 + D_MODEL * IN_WIDTH * (4 + 2) + 2 * tm * PROJ_WIDTH * 2 + 3 * tm * B_WIDTH * 4 * 2
            + 3 * 2 * tm * LANES * 4)
    return pl.pallas_call(
        _in_proj_kernel,
        out_shape=jax.ShapeDtypeStruct((n_tok, PROJ_WIDTH), jnp.bfloat16),
        grid=(n_tok // tm,),
        in_specs=[
            pl.BlockSpec((tm, D_MODEL), row),
            pl.BlockSpec((1, D_MODEL), const),
            pl.BlockSpec((D_MODEL, IN_WIDTH), const, pipeline_mode=pl.Buffered(1)),
            pl.BlockSpec((tm, LANES), pos),
            pl.BlockSpec((tm, LANES), pos),
            pl.BlockSpec((tm, LANES), pos),
        ],
        out_specs=pl.BlockSpec((tm, PROJ_WIDTH), row),
        scratch_shapes=[pltpu.VMEM((D_MODEL, IN_WIDTH), jnp.bfloat16)],
        compiler_params=pltpu.CompilerParams(
            dimension_semantics=("arbitrary",), vmem_limit_bytes=_vmem_limit(vmem + (8 << 20))),
        name="in_proj",
    )(x2d, g, w_in, *rope)


def _transpose_keys(v_ref, vt_ref, ones_below=False):
    top = lax.broadcasted_iota(jnp.int32, (LANES, LANES), 0) < HEAD_DIM
    for j in range(v_ref.shape[0] // LANES):
        tile = v_ref[j * LANES:(j + 1) * LANES, :].T
        vt_ref[j] = jnp.where(top, tile, jnp.ones_like(tile)) if ones_below else tile


class _Stream(NamedTuple):
    q_ref: object
    k_ref: object
    vt_ref: object
    o_ref: object
    sink_row: object = None
    sums_below: bool = False


class _Block(NamedTuple):
    stream: _Stream
    q0: int
    k0: int
    n_keys: int
    add_bias: object


def _attend(blocks, skew):
    lane = lax.broadcasted_iota(jnp.int32, (Q_BLK, LANES), 1)
    left = lane < HEAD_DIM
    top = lax.broadcasted_iota(jnp.int32, (LANES, Q_BLK), 0) < HEAD_DIM

    def scores_of(blk):
        q2 = blk.stream.q_ref[blk.q0:blk.q0 + Q_BLK, :]
        zero = jnp.zeros_like(q2)
        qq = jnp.concatenate([jnp.where(left, q2, zero), jnp.where(left, zero, q2)], axis=0)
        st = lax.dot_general(blk.stream.k_ref[blk.k0:blk.k0 + blk.n_keys, :], qq, (((1,), (1,)), ((), ())),
                             preferred_element_type=jnp.float32)
        return blk.add_bias(st)

    def finish(blk, st):
        _, _, vt_ref, o_ref, sink_row, sums_below = blk.stream
        m = jnp.max(st, axis=0, keepdims=True)
        if sink_row is not None:
            m = jnp.maximum(m, sink_row)
        p = jnp.exp2(st - m)
        tile0, full, part = blk.k0 // LANES, blk.n_keys // LANES, blk.n_keys % LANES
        pieces = [vt_ref[tile0 + j] for j in range(full)]
        if part:
            pieces.append(vt_ref[tile0 + full][:, :part])
        ot = jnp.dot(jnp.concatenate(pieces, axis=1), p.astype(jnp.bfloat16), preferred_element_type=jnp.float32)
        den = ot[HEAD_DIM:HEAD_DIM + 1, :] if sums_below else jnp.sum(p, axis=0, keepdims=True)
        if sink_row is not None:
            den = den + jnp.exp2(sink_row - m)
        if sums_below:
            ot = ot[:HEAD_DIM, :] / den
            x = jnp.concatenate([ot[:, :Q_BLK], ot[:, Q_BLK:]], axis=0)
        else:
            ot = ot / den
            x = jnp.where(top, ot[:, :Q_BLK], ot[:, Q_BLK:])
        o_ref[blk.q0:blk.q0 + Q_BLK, :] = x.T.astype(o_ref.dtype)

    pending = [scores_of(blk) for blk in blocks[:skew]]
    for c, blk in enumerate(blocks):
        st = pending.pop(0)
        if c + skew < len(blocks):
            pending.append(scores_of(blocks[c + skew]))
        finish(blk, st)


def _build_na_bias(rpb_ref, mask_ref, t_scr, hp, ridx):
    n_rrow, n_rcol = 2 * NA_ROWS - 1, 2 * NA_COLS - 1
    kc = lax.broadcasted_iota(jnp.int32, (GRID_W, LANES), 0)
    lane = lax.broadcasted_iota(jnp.int32, (GRID_W, LANES), 1)
    diff = kc - (lane % GRID_W) + (NA_COLS - 1)
    second_row = lane >= GRID_W
    used_rows = sorted({int(r) for r in ridx.reshape(-1) if r >= 0})
    zero = jnp.zeros((GRID_W, LANES), jnp.float32)
    for hh in range(2):
        h = 2 * hp + hh
        toeplitz = {-1: zero}
        for rho in used_rows:
            acc = zero
            for d in range(n_rcol):
                acc = jnp.where(diff == d, rpb_ref[(h * n_rrow + rho) * n_rcol + d] * LOG2E, acc)
            toeplitz[rho] = acc
        for cls in range(NA_CLASSES):
            for b in range(NA_MAX_KEY_ROWS):
                blk = jnp.where(second_row, toeplitz[int(ridx[cls, 1, b])], toeplitz[int(ridx[cls, 0, b])])
                ks = slice(b * GRID_W, (b + 1) * GRID_W)
                t_scr[cls, ks, hh * Q_BLK:(hh + 1) * Q_BLK] = blk + mask_ref[cls, ks, :]


def _swa_add_mask(st, edge_ref, first_tile, n):
    parts = []
    for j in range(st.shape[0] // Q_BLK):
        part = st[j * Q_BLK:(j + 1) * Q_BLK, :]
        if first_tile + j != n:
            part = part + edge_ref[0 if first_tile + j < n else 1]
        parts.append(part)
    return jnp.concatenate(parts, axis=0)


def _attn_kernel(rpb_ref, sink_ref, qa_ref, ka_ref, va_ref, qb_ref, kd_ref, vd_ref, mask_ref, edge_ref, *rest,
                 ridx, seq):
    n_cast = (len(rest) - 5) // 2
    cast_in, (oa_ref, ob_ref), cast_out = rest[:n_cast], rest[n_cast:n_cast + 2], rest[n_cast + 2:-3]
    t_scr, vta_scr, vtb_scr = rest[-3:]
    pair = pl.program_id(0)
    nblk = seq // Q_BLK

    for src, dst in zip(cast_in, cast_out):
        dst[...] = src[...].astype(dst.dtype)

    @pl.when(pl.program_id(1) == 0)
    def _():
        _build_na_bias(rpb_ref, mask_ref, t_scr, pair, ridx)

    _transpose_keys(va_ref, vta_scr)
    _transpose_keys(vd_ref, vtb_scr, ones_below=True)
    lane = lax.broadcasted_iota(jnp.int32, (1, 2 * Q_BLK), 1)
    sink_row = jnp.where(lane < Q_BLK, sink_ref[2 * pair], sink_ref[2 * pair + 1]) * LOG2E
    na = _Stream(qa_ref, ka_ref, vta_scr, oa_ref)
    swa = _Stream(qb_ref, kd_ref, vtb_scr, ob_ref, sink_row, sums_below=True)
    blocks = []
    for i in range(nblk):
        lo, hi = _na_key_rows(i, seq // GRID_W)
        n_keys = (hi - lo) * GRID_W
        add_bias = functools.partial(lambda st, cls, n: st + t_scr[cls, :n, :], cls=_na_class_of_block(i, nblk), n=n_keys)
        blocks.append(_Block(na, i * Q_BLK, lo * GRID_W, n_keys, add_bias))
        first, last = max(i - 1, 0), min(i + 1, nblk - 1)
        add_mask = functools.partial(_swa_add_mask, edge_ref=edge_ref, first_tile=first, n=i)
        blocks.append(_Block(swa, i * Q_BLK, first * Q_BLK, (last - first + 1) * Q_BLK, add_mask))
    _attend(blocks, ATTN_SKEW)


def _row_sliceable(w, steps):
    rows, cols = w.shape
    fold = 1
    while (rows * fold) % (steps * 16):
        fold *= 2
    assert cols % (fold * LANES) == 0
    return w.reshape(rows * fold, cols // fold)


def _attention(proj, rpb_flat, sink, seq, batch, later_weights):
    masks, ridx = _na_static_tables(seq)
    edges = _swa_edge_masks()
    n_tok = proj.shape[0]
    n_pairs = A_WIDTH // LANES
    assert B_WIDTH // LANES == n_pairs
    pairs_per_kv = n_pairs // SWA_KV_HEADS
    col = lambda base: (lambda p, b: (b, base // LANES + p))
    kv_col = lambda base: (lambda p, b: (b, base // LANES + p // pairs_per_kv))
    tile = lambda index_map: pl.BlockSpec((seq, LANES), index_map)
    whole = lambda a: pl.BlockSpec(a.shape, lambda p, b: (0,) * a.ndim)
    steps = n_pairs * batch
    views = [_row_sliceable(w, steps) for w in later_weights]
    slices = [pl.BlockSpec((v.shape[0] // steps, v.shape[1]), lambda p, b: (p * batch + b, 0)) for v in views]
    slice_elems = sum(v.size // steps for v in views)
    table_bytes = NA_CLASSES * NA_MAX_KEYS * 2 * Q_BLK * 4
    vmem = (2 * 8 * seq * LANES * 2 + 2 * (masks.size + edges.size) * 4 + table_bytes + 2 * seq * LANES * 2
            + 2 * slice_elems * (4 + 2))
    out = jax.ShapeDtypeStruct((n_tok, A_WIDTH), jnp.bfloat16)
    o_na, o_swa, *casts = pl.pallas_call(
        functools.partial(_attn_kernel, ridx=ridx, seq=seq),
        out_shape=[out, out] + [jax.ShapeDtypeStruct(v.shape, jnp.bfloat16) for v in views],
        grid=(n_pairs, batch),
        in_specs=[
            pl.BlockSpec(memory_space=pltpu.SMEM),
            pl.BlockSpec(memory_space=pltpu.SMEM),
            tile(col(COL_QA)), tile(col(COL_KA)), tile(col(COL_VA)),
            tile(col(COL_QB)), tile(kv_col(COL_KD)), tile(kv_col(COL_VD)),
            whole(masks), whole(edges),
        ] + slices,
        out_specs=[tile(lambda p, b: (b, p)), tile(lambda p, b: (b, p))] + slices,
        scratch_shapes=[pltpu.VMEM((NA_CLASSES, NA_MAX_KEYS, 2 * Q_BLK), jnp.float32),
                        pltpu.VMEM((seq // LANES, LANES, LANES), jnp.bfloat16),
                        pltpu.VMEM((seq // LANES, LANES, LANES), jnp.bfloat16)],
        compiler_params=pltpu.CompilerParams(
            dimension_semantics=("arbitrary", "arbitrary"), vmem_limit_bytes=_attn_vmem_limit(vmem)),
        name="attention",
    )(rpb_flat, sink, proj, proj, proj, proj, proj, proj, jnp.asarray(masks), jnp.asarray(edges), *views)
    return o_na, o_swa, [c.reshape(w.shape) for c, w in zip(casts, later_weights)]


def _out_ffn_kernel(x_ref, oa_ref, ob_ref, gna_ref, gswa_ref, wo_ref, gffn_ref, wg_ref, wu_ref, wd_ref,
                    gfin_ref, o_ref, *, ff_chunk, sub_rows):
    d_ff = wg_ref.shape[1]
    row_slices = [slice(r0, r0 + sub_rows) for r0 in range(0, x_ref.shape[0], sub_rows)]
    mixed = []
    for rows in row_slices:
        na = _rms(oa_ref[rows, :].astype(jnp.float32), gna_ref[...]).astype(jnp.bfloat16)
        nb = _rms(ob_ref[rows, :].astype(jnp.float32), gswa_ref[...]).astype(jnp.bfloat16)
        x1 = (x_ref[rows, :]
              + jnp.dot(na, wo_ref[:A_WIDTH, :], preferred_element_type=jnp.float32)
              + jnp.dot(nb, wo_ref[A_WIDTH:, :], preferred_element_type=jnp.float32))
        mixed.append((x1, _rms(x1, gffn_ref[...]).astype(jnp.bfloat16)))
    for rows, (x1, h) in zip(row_slices, mixed):
        acc = jnp.zeros(x1.shape, jnp.float32)
        for c0 in range(0, d_ff, ff_chunk):
            gate = jnp.dot(h, wg_ref[:, c0:c0 + ff_chunk], preferred_element_type=jnp.float32)
            up = jnp.dot(h, wu_ref[:, c0:c0 + ff_chunk], preferred_element_type=jnp.float32)
            act = (jax.nn.silu(gate) * up).astype(jnp.bfloat16)
            acc = acc + jnp.dot(act, wd_ref[c0:c0 + ff_chunk, :], preferred_element_type=jnp.float32)
        o_ref[rows, :] = _rms(x1 + acc, gfin_ref[...])


def _out_ffn(x2d, o_na, o_swa, g_na, g_swa, wo, g_ffn, wg, wu, wd, g_fin):
    n_tok = x2d.shape[0]
    tm, sub_rows, ff_chunk = FFN_ROWS, FFN_SUB_ROWS, FFN_CHUNK
    d_ff = wg.shape[1]
    const = lambda i: (0, 0)
    row = lambda i: (i, 0)
    resident = lambda shape: pl.BlockSpec(shape, const, pipeline_mode=pl.Buffered(1))
    weights = (D_MODEL * D_MODEL + 3 * D_MODEL * d_ff) * 2
    tiles = 2 * 2 * tm * D_MODEL * 4 + 2 * 2 * tm * A_WIDTH * 2
    vmem = weights + tiles + 2 * (6 * sub_rows * D_MODEL * 4 + 4 * sub_rows * ff_chunk * 4)
    return pl.pallas_call(
        functools.partial(_out_ffn_kernel, ff_chunk=ff_chunk, sub_rows=sub_rows),
        out_shape=jax.ShapeDtypeStruct((n_tok, D_MODEL), jnp.float32),
        grid=(n_tok // tm,),
        in_specs=[
            pl.BlockSpec((tm, D_MODEL), row),
            pl.BlockSpec((tm, A_WIDTH), row),
            pl.BlockSpec((tm, B_WIDTH), row),
            pl.BlockSpec((1, A_WIDTH), const),
            pl.BlockSpec((1, B_WIDTH), const),
            resident((D_MODEL, D_MODEL)),
            pl.BlockSpec((1, D_MODEL), const),
            resident((D_MODEL, d_ff)),
            resident((D_MODEL, d_ff)),
            resident((d_ff, D_MODEL)),
            pl.BlockSpec((1, D_MODEL), const),
        ],
        out_specs=pl.BlockSpec((tm, D_MODEL), row),
        compiler_params=pltpu.CompilerParams(
            dimension_semantics=("arbitrary",), vmem_limit_bytes=_vmem_limit(vmem + (8 << 20))),
        name="out_ffn",
    )(x2d, o_na, o_swa, g_na, g_swa, wo, g_ffn, wg, wu, wd, g_fin)


def kernel(x, g_norm_mix, w_in, na_rpb, swa_sink, g_out_na, g_out_swa, w_out, g_norm_ffn, w_gate, w_up, w_down,
           g_final):
    batch, seq, d_model = x.shape
    assert w_in.shape[0] == 1, "the final RMSNorm is fused into the single layer's FFN call"
    assert d_model == D_MODEL and seq % Q_BLK == 0 and seq // GRID_W >= NA_MAX_KEY_ROWS
    assert B_KV_WIDTH == LANES, "the windowed k (and v) heads together fill one lane tile"
    rope = _rope_tables(seq)
    x2d = x.reshape(batch * seq, d_model)
    proj = _in_proj(x2d, g_norm_mix[0][None, :], w_in[0], rope, seq)
    o_na, o_swa, (wo, wg, wu, wd) = _attention(proj, na_rpb[0].reshape(-1), swa_sink[0], seq, batch,
                                               (w_out[0], w_gate[0], w_up[0], w_down[0]))
    out = _out_ffn(x2d, o_na, o_swa, g_out_na[0][None, :], g_out_swa[0][None, :], wo,
                   g_norm_ffn[0][None, :], wg, wu, wd, g_final[None, :])
    return out.reshape(batch, seq, d_model)
```

```python
import functools
from typing import NamedTuple

import numpy as np
import jax
import jax.numpy as jnp
from jax import lax
from jax.experimental import pallas as pl
from jax.experimental.pallas import tpu as pltpu

D_MODEL = 1024
HEAD_DIM = 64
NA_HEADS = 8
SWA_HEADS = 8
SWA_KV_HEADS = 2
A_WIDTH = NA_HEADS * HEAD_DIM
B_WIDTH = SWA_HEADS * HEAD_DIM
B_KV_WIDTH = SWA_KV_HEADS * HEAD_DIM
IN_WIDTH = 3 * A_WIDTH + B_WIDTH + 2 * B_KV_WIDTH
GRID_W = 64
NA_ROWS = 8
NA_COLS = 16
SWA_WINDOW = 128
ROPE_THETA = 500000.0
ROPE_DIM = HEAD_DIM // 4
RMS_EPS = 1e-6
NEG_INF = -1e30
LOG2E = 1.4426950408889634
Q_SCALE = HEAD_DIM ** -0.5 * LOG2E

LANES = 128
VMEM_CAPACITY = 64 * 1024 * 1024

PROJ_WIDTH = 3 * A_WIDTH + B_WIDTH + 2 * SWA_KV_HEADS * LANES
COL_QA, COL_KA, COL_VA = 0, A_WIDTH, 2 * A_WIDTH
COL_QB = 3 * A_WIDTH
COL_KD = COL_QB + B_WIDTH
COL_VD = COL_KD + SWA_KV_HEADS * LANES
W_COL_KB = COL_QB + B_WIDTH

Q_BLK = 128
NA_Q_ROWS = Q_BLK // GRID_W
NA_MAX_KEY_ROWS = NA_ROWS + NA_Q_ROWS - 1
NA_MAX_KEYS = NA_MAX_KEY_ROWS * GRID_W
NA_CLASSES = 5
ATTN_SKEW = 6
IN_PROJ_ROWS = 1024
FFN_ROWS = 1024
FFN_SUB_ROWS = 512
FFN_CHUNK = 256


def _vmem_limit(nbytes):
    return int(min(VMEM_CAPACITY - (2 << 20), max(nbytes, 16 << 20)))


def _attn_vmem_limit(nbytes):
    assert nbytes + (16 << 20) <= VMEM_CAPACITY - (2 << 20)
    return VMEM_CAPACITY - (2 << 20)


def _na_class_of_block(i, nblk):
    if i < 2:
        return i
    if i >= nblk - 2:
        return NA_CLASSES - (nblk - i)
    return 2


def _na_key_rows(i, rows):
    first = NA_Q_ROWS * i
    lo = int(np.clip(first - NA_ROWS // 2, 0, rows - NA_ROWS))
    hi = int(np.clip(first + NA_Q_ROWS - 1 - NA_ROWS // 2, 0, rows - NA_ROWS)) + NA_ROWS
    return lo, hi


def _na_static_tables(seq):
    rows = seq // GRID_W
    nblk = seq // Q_BLK
    masks = np.full((NA_CLASSES, NA_MAX_KEYS, Q_BLK), NEG_INF, np.float32)
    ridx = np.full((NA_CLASSES, NA_Q_ROWS, NA_MAX_KEY_ROWS), -1, np.int64)
    seen = {}
    for i in range(nblk):
        cls = _na_class_of_block(i, nblk)
        lo, hi = _na_key_rows(i, rows)
        assert (lo * GRID_W) % LANES == 0 and hi - lo <= NA_MAX_KEY_ROWS
        q = np.arange(Q_BLK) + i * Q_BLK
        qr, qc = q // GRID_W, q % GRID_W
        k = np.arange((hi - lo) * GRID_W) + lo * GRID_W
        kr, kc = k // GRID_W, k % GRID_W
        rs = np.clip(qr - NA_ROWS // 2, 0, rows - NA_ROWS)
        cs = np.clip(qc - NA_COLS // 2, 0, GRID_W - NA_COLS)
        ok_r = (kr[:, None] >= rs[None, :]) & (kr[:, None] < rs[None, :] + NA_ROWS)
        ok_c = (kc[:, None] >= cs[None, :]) & (kc[:, None] < cs[None, :] + NA_COLS)
        m = np.full((NA_MAX_KEYS, Q_BLK), NEG_INF, np.float32)
        m[:k.size] = np.where(ok_r & ok_c, 0.0, NEG_INF)
        rr = np.full((NA_Q_ROWS, NA_MAX_KEY_ROWS), -1, np.int64)
        for a in range(NA_Q_ROWS):
            for b in range(hi - lo):
                r, krow = NA_Q_ROWS * i + a, lo + b
                if rs[a * GRID_W] <= krow < rs[a * GRID_W] + NA_ROWS:
                    rr[a, b] = krow - r + NA_ROWS - 1
        if cls in seen:
            assert np.array_equal(seen[cls][0], m) and np.array_equal(seen[cls][1], rr)
        seen[cls] = (m, rr)
        masks[cls], ridx[cls] = m, rr
    return masks, ridx


def _swa_edge_masks():
    assert SWA_WINDOW == Q_BLK
    key = np.arange(Q_BLK)[:, None]
    qry = np.arange(Q_BLK)[None, :]
    tri = np.stack([np.where(key >= qry, 0.0, NEG_INF), np.where(key <= qry, 0.0, NEG_INF)]).astype(np.float32)
    return np.ascontiguousarray(np.concatenate([tri, tri], axis=2))


def _rope_tables(seq):
    half = ROPE_DIM // 2
    inv_freq = np.float32(ROPE_THETA) ** (-np.arange(0, ROPE_DIM, 2, dtype=np.float32) / np.float32(ROPE_DIM))
    ang = np.arange(seq, dtype=np.float32)[:, None] * inv_freq[None, :].astype(np.float32)
    cos, sin = np.cos(ang).astype(np.float32), np.sin(ang).astype(np.float32)
    ones = np.ones((seq, HEAD_DIM - ROPE_DIM), np.float32)
    zeros = np.zeros((seq, HEAD_DIM - ROPE_DIM), np.float32)
    zh = np.zeros((seq, half), np.float32)
    c = np.concatenate([cos, cos, ones], axis=1)
    sa = np.concatenate([-sin, zh, zeros], axis=1)
    sb = np.concatenate([zh, sin, zeros], axis=1)
    rep = LANES // HEAD_DIM
    return tuple(jnp.asarray(np.tile(t, (1, rep))) for t in (c, sa, sb))


def _rms(x, g):
    return x * lax.rsqrt(jnp.mean(x * x, axis=-1, keepdims=True) + RMS_EPS) * g


def _rope_tile(a, c, sa, sb):
    return a * c + pltpu.roll(a, LANES - ROPE_DIM // 2, axis=1) * sa + pltpu.roll(a, ROPE_DIM // 2, axis=1) * sb


def _in_proj_kernel(x_ref, g_ref, w_ref, c_ref, sa_ref, sb_ref, *rest):
    n_extra = (len(rest) - 2) // 2
    extra_in, o_ref, extra_out, w_scr = rest[:n_extra], rest[n_extra], rest[n_extra + 1:-1], rest[-1]

    @pl.when(pl.program_id(0) == 0)
    def _():
        for c0 in range(0, w_ref.shape[1], LANES):
            w_scr[:, c0:c0 + LANES] = w_ref[:, c0:c0 + LANES].astype(w_scr.dtype)

    for src, dst in zip(extra_in, extra_out):
        dst[...] = src[...].astype(dst.dtype)

    h = _rms(x_ref[...], g_ref[...]).astype(jnp.bfloat16)
    c, sa, sb = c_ref[...], sa_ref[...], sb_ref[...]

    def proj(col, width):
        return jnp.dot(h, w_scr[:, col:col + width], preferred_element_type=jnp.float32)

    kv = proj(W_COL_KB, 2 * B_KV_WIDTH)
    left = lax.broadcasted_iota(jnp.int32, (h.shape[0], LANES), 1) < HEAD_DIM
    for col, tile in ((COL_KD, _rope_tile(kv[:, :B_KV_WIDTH], c, sa, sb)), (COL_VD, kv[:, B_KV_WIDTH:])):
        swapped = pltpu.roll(tile, HEAD_DIM, axis=1)
        o_ref[:, col:col + LANES] = jnp.where(left, tile, swapped).astype(o_ref.dtype)
        o_ref[:, col + LANES:col + 2 * LANES] = jnp.where(left, swapped, tile).astype(o_ref.dtype)
    qb = proj(COL_QB, B_WIDTH)
    for t in range(B_WIDTH // LANES):
        lo = t * LANES
        tile = _rope_tile(qb[:, lo:lo + LANES], c, sa, sb) * Q_SCALE
        o_ref[:, COL_QB + lo:COL_QB + lo + LANES] = tile.astype(o_ref.dtype)
    o_ref[:, COL_QA:COL_QA + A_WIDTH] = (proj(COL_QA, A_WIDTH) * Q_SCALE).astype(o_ref.dtype)
    o_ref[:, COL_KA:COL_KA + A_WIDTH] = proj(COL_KA, A_WIDTH).astype(o_ref.dtype)
    o_ref[:, COL_VA:COL_VA + A_WIDTH] = proj(COL_VA, A_WIDTH).astype(o_ref.dtype)


def _in_proj(x2d, g, w_in, rope, seq, later_weights):
    n_tok = x2d.shape[0]
    tm = IN_PROJ_ROWS
    steps = n_tok // tm
    pos_blocks = seq // tm
    const = lambda i: (0, 0)
    row = lambda i: (i, 0)
    pos = lambda i: (i % pos_blocks, 0)
    slices = [(w.shape[0] // steps, w.shape[1]) for w in later_weights]
    assert all(w.shape[0] % steps == 0 and rows % 16 == 0 for w, (rows, _) in zip(later_weights, slices))
    slice_bytes = sum(rows * cols for rows, cols in slices)
    vmem = (2 * tm * D_MODEL * 4 + D_MODEL * IN_WIDTH * (4 + 2) + 2 * tm * PROJ_WIDTH * 2 + 3 * tm * B_WIDTH * 4 * 2
            + 2 * slice_bytes * (4 + 2) + 3 * 2 * tm * LANES * 4)
    outs = pl.pallas_call(
        _in_proj_kernel,
        out_shape=[jax.ShapeDtypeStruct((n_tok, PROJ_WIDTH), jnp.bfloat16)]
        + [jax.ShapeDtypeStruct(w.shape, jnp.bfloat16) for w in later_weights],
        grid=(steps,),
        in_specs=[
            pl.BlockSpec((tm, D_MODEL), row),
            pl.BlockSpec((1, D_MODEL), const),
            pl.BlockSpec((D_MODEL, IN_WIDTH), const, pipeline_mode=pl.Buffered(1)),
            pl.BlockSpec((tm, LANES), pos),
            pl.BlockSpec((tm, LANES), pos),
            pl.BlockSpec((tm, LANES), pos),
        ] + [pl.BlockSpec(blk, row) for blk in slices],
        out_specs=[pl.BlockSpec((tm, PROJ_WIDTH), row)] + [pl.BlockSpec(blk, row) for blk in slices],
        scratch_shapes=[pltpu.VMEM((D_MODEL, IN_WIDTH), jnp.bfloat16)],
        compiler_params=pltpu.CompilerParams(
            dimension_semantics=("arbitrary",), vmem_limit_bytes=_vmem_limit(vmem + (8 << 20))),
        name="in_proj",
    )(x2d, g, w_in, *rope, *later_weights)
    return outs[0], outs[1:]


def _transpose_keys(v_ref, vt_ref, ones_below=False):
    top = lax.broadcasted_iota(jnp.int32, (LANES, LANES), 0) < HEAD_DIM
    for j in range(v_ref.shape[0] // LANES):
        tile = v_ref[j * LANES:(j + 1) * LANES, :].T
        vt_ref[j] = jnp.where(top, tile, jnp.ones_like(tile)) if ones_below else tile


class _Stream(NamedTuple):
    q_ref: object
    k_ref: object
    vt_ref: object
    o_ref: object
    sink_row: object = None
    sums_below: bool = False


class _Block(NamedTuple):
    stream: _Stream
    q0: int
    k0: int
    n_keys: int
    add_bias: object


def _attend(blocks, skew):
    lane = lax.broadcasted_iota(jnp.int32, (Q_BLK, LANES), 1)
    left = lane < HEAD_DIM
    top = lax.broadcasted_iota(jnp.int32, (LANES, Q_BLK), 0) < HEAD_DIM

    def scores_of(blk):
        q2 = blk.stream.q_ref[blk.q0:blk.q0 + Q_BLK, :]
        zero = jnp.zeros_like(q2)
        qq = jnp.concatenate([jnp.where(left, q2, zero), jnp.where(left, zero, q2)], axis=0)
        st = lax.dot_general(blk.stream.k_ref[blk.k0:blk.k0 + blk.n_keys, :], qq, (((1,), (1,)), ((), ())),
                             preferred_element_type=jnp.float32)
        return blk.add_bias(st)

    def finish(blk, st):
        _, _, vt_ref, o_ref, sink_row, sums_below = blk.stream
        m = jnp.max(st, axis=0, keepdims=True)
        if sink_row is not None:
            m = jnp.maximum(m, sink_row)
        p = jnp.exp2(st - m)
        tile0, full, part = blk.k0 // LANES, blk.n_keys // LANES, blk.n_keys % LANES
        pieces = [vt_ref[tile0 + j] for j in range(full)]
        if part:
            pieces.append(vt_ref[tile0 + full][:, :part])
        ot = jnp.dot(jnp.concatenate(pieces, axis=1), p.astype(jnp.bfloat16), preferred_element_type=jnp.float32)
        den = ot[HEAD_DIM:HEAD_DIM + 1, :] if sums_below else jnp.sum(p, axis=0, keepdims=True)
        if sink_row is not None:
            den = den + jnp.exp2(sink_row - m)
        if sums_below:
            ot = ot[:HEAD_DIM, :] / den
            x = jnp.concatenate([ot[:, :Q_BLK], ot[:, Q_BLK:]], axis=0)
        else:
            ot = ot / den
            x = jnp.where(top, ot[:, :Q_BLK], ot[:, Q_BLK:])
        o_ref[blk.q0:blk.q0 + Q_BLK, :] = x.T.astype(o_ref.dtype)

    pending = [scores_of(blk) for blk in blocks[:skew]]
    for c, blk in enumerate(blocks):
        st = pending.pop(0)
        if c + skew < len(blocks):
            pending.append(scores_of(blocks[c + skew]))
        finish(blk, st)


def _build_na_bias(rpb_ref, mask_ref, t_scr, hp, ridx):
    n_rrow, n_rcol = 2 * NA_ROWS - 1, 2 * NA_COLS - 1
    sublanes = 8
    lane8 = lax.broadcasted_iota(jnp.int32, (sublanes, LANES), 1)
    second_row = lax.broadcasted_iota(jnp.int32, (GRID_W, LANES), 1) >= GRID_W
    used_rows = sorted({int(r) for r in ridx.reshape(-1) if r >= 0})
    zero = jnp.zeros((GRID_W, LANES), jnp.float32)
    for hh in range(2):
        h = 2 * hp + hh
        toeplitz = {-1: zero}
        for rho in used_rows:
            band = jnp.zeros((sublanes, LANES), jnp.float32)
            for d in range(n_rcol):
                band = jnp.where(lane8 == (NA_COLS - 1 - d) % LANES, rpb_ref[(h * n_rrow + rho) * n_rcol + d] * LOG2E, band)
            rows = jnp.concatenate([band] * (GRID_W // sublanes), axis=0)
            first = pltpu.roll(rows, 0, axis=1, stride=1, stride_axis=0)
            second = pltpu.roll(rows, GRID_W, axis=1, stride=1, stride_axis=0)
            toeplitz[rho] = jnp.where(second_row, second, first)
        for cls in range(NA_CLASSES):
            for b in range(NA_MAX_KEY_ROWS):
                blk = jnp.where(second_row, toeplitz[int(ridx[cls, 1, b])], toeplitz[int(ridx[cls, 0, b])])
                ks = slice(b * GRID_W, (b + 1) * GRID_W)
                t_scr[cls, ks, hh * Q_BLK:(hh + 1) * Q_BLK] = blk + mask_ref[cls, ks, :]


def _swa_add_mask(st, edge_ref, first_tile, n):
    parts = []
    for j in range(st.shape[0] // Q_BLK):
        part = st[j * Q_BLK:(j + 1) * Q_BLK, :]
        if first_tile + j != n:
            part = part + edge_ref[0 if first_tile + j < n else 1]
        parts.append(part)
    return jnp.concatenate(parts, axis=0)


def _attn_kernel(rpb_ref, sink_ref, qa_ref, ka_ref, va_ref, qb_ref, kd_ref, vd_ref, mask_ref, edge_ref,
                 oa_ref, ob_ref, t_scr, vta_scr, vtb_scr, *, ridx, seq):
    pair = pl.program_id(0)
    nblk = seq // Q_BLK

    @pl.when(pl.program_id(1) == 0)
    def _():
        _build_na_bias(rpb_ref, mask_ref, t_scr, pair, ridx)

    _transpose_keys(va_ref, vta_scr)
    _transpose_keys(vd_ref, vtb_scr, ones_below=True)
    lane = lax.broadcasted_iota(jnp.int32, (1, 2 * Q_BLK), 1)
    sink_row = jnp.where(lane < Q_BLK, sink_ref[2 * pair], sink_ref[2 * pair + 1]) * LOG2E
    na = _Stream(qa_ref, ka_ref, vta_scr, oa_ref)
    swa = _Stream(qb_ref, kd_ref, vtb_scr, ob_ref, sink_row, sums_below=True)
    blocks = []
    for i in range(nblk):
        lo, hi = _na_key_rows(i, seq // GRID_W)
        n_keys = (hi - lo) * GRID_W
        add_bias = functools.partial(lambda st, cls, n: st + t_scr[cls, :n, :], cls=_na_class_of_block(i, nblk), n=n_keys)
        blocks.append(_Block(na, i * Q_BLK, lo * GRID_W, n_keys, add_bias))
        first, last = max(i - 1, 0), min(i + 1, nblk - 1)
        add_mask = functools.partial(_swa_add_mask, edge_ref=edge_ref, first_tile=first, n=i)
        blocks.append(_Block(swa, i * Q_BLK, first * Q_BLK, (last - first + 1) * Q_BLK, add_mask))
    _attend(blocks, ATTN_SKEW)


def _attention(proj, rpb_flat, sink, seq, batch):
    masks, ridx = _na_static_tables(seq)
    edges = _swa_edge_masks()
    n_tok = proj.shape[0]
    n_pairs = A_WIDTH // LANES
    assert B_WIDTH // LANES == n_pairs
    pairs_per_kv = n_pairs // SWA_KV_HEADS
    col = lambda base: (lambda p, b: (b, base // LANES + p))
    kv_col = lambda base: (lambda p, b: (b, base // LANES + p // pairs_per_kv))
    tile = lambda index_map: pl.BlockSpec((seq, LANES), index_map)
    whole = lambda a: pl.BlockSpec(a.shape, lambda p, b: (0,) * a.ndim)
    table_bytes = NA_CLASSES * NA_MAX_KEYS * 2 * Q_BLK * 4
    vmem = 2 * 8 * seq * LANES * 2 + 2 * (masks.size + edges.size) * 4 + table_bytes + 2 * seq * LANES * 2
    out = jax.ShapeDtypeStruct((n_tok, A_WIDTH), jnp.bfloat16)
    return pl.pallas_call(
        functools.partial(_attn_kernel, ridx=ridx, seq=seq),
        out_shape=[out, out],
        grid=(n_pairs, batch),
        in_specs=[
            pl.BlockSpec(memory_space=pltpu.SMEM),
            pl.BlockSpec(memory_space=pltpu.SMEM),
            tile(col(COL_QA)), tile(col(COL_KA)), tile(col(COL_VA)),
            tile(col(COL_QB)), tile(kv_col(COL_KD)), tile(kv_col(COL_VD)),
            whole(masks), whole(edges),
        ],
        out_specs=[tile(lambda p, b: (b, p)), tile(lambda p, b: (b, p))],
        scratch_shapes=[pltpu.VMEM((NA_CLASSES, NA_MAX_KEYS, 2 * Q_BLK), jnp.float32),
                        pltpu.VMEM((seq // LANES, LANES, LANES), jnp.bfloat16),
                        pltpu.VMEM((seq // LANES, LANES, LANES), jnp.bfloat16)],
        compiler_params=pltpu.CompilerParams(
            dimension_semantics=("arbitrary", "arbitrary"), vmem_limit_bytes=_attn_vmem_limit(vmem)),
        name="attention",
    )(rpb_flat, sink, proj, proj, proj, proj, proj, proj, jnp.asarray(masks), jnp.asarray(edges))


def _out_ffn_kernel(x_ref, oa_ref, ob_ref, gna_ref, gswa_ref, wo_ref, gffn_ref, wg_ref, wu_ref, wd_ref,
                    gfin_ref, o_ref, *, ff_chunk, sub_rows):
    d_ff = wg_ref.shape[1]
    row_slices = [slice(r0, r0 + sub_rows) for r0 in range(0, x_ref.shape[0], sub_rows)]
    mixed = []
    for rows in row_slices:
        na = _rms(oa_ref[rows, :].astype(jnp.float32), gna_ref[...]).astype(jnp.bfloat16)
        nb = _rms(ob_ref[rows, :].astype(jnp.float32), gswa_ref[...]).astype(jnp.bfloat16)
        x1 = (x_ref[rows, :]
              + jnp.dot(na, wo_ref[:A_WIDTH, :], preferred_element_type=jnp.float32)
              + jnp.dot(nb, wo_ref[A_WIDTH:, :], preferred_element_type=jnp.float32))
        mixed.append((x1, _rms(x1, gffn_ref[...]).astype(jnp.bfloat16)))
    for rows, (x1, h) in zip(row_slices, mixed):
        acc = jnp.zeros(x1.shape, jnp.float32)
        for c0 in range(0, d_ff, ff_chunk):
            gate = jnp.dot(h, wg_ref[:, c0:c0 + ff_chunk], preferred_element_type=jnp.float32)
            up = jnp.dot(h, wu_ref[:, c0:c0 + ff_chunk], preferred_element_type=jnp.float32)
            act = (jax.nn.silu(gate) * up).astype(jnp.bfloat16)
            acc = acc + jnp.dot(act, wd_ref[c0:c0 + ff_chunk, :], preferred_element_type=jnp.float32)
        o_ref[rows, :] = _rms(x1 + acc, gfin_ref[...])


def _out_ffn(x2d, o_na, o_swa, g_na, g_swa, wo, g_ffn, wg, wu, wd, g_fin):
    n_tok = x2d.shape[0]
    tm, sub_rows, ff_chunk = FFN_ROWS, FFN_SUB_ROWS, FFN_CHUNK
    d_ff = wg.shape[1]
    const = lambda i: (0, 0)
    row = lambda i: (i, 0)
    resident = lambda shape: pl.BlockSpec(shape, const, pipeline_mode=pl.Buffered(1))
    weights = (D_MODEL * D_MODEL + 3 * D_MODEL * d_ff) * 2
    tiles = 2 * 2 * tm * D_MODEL * 4 + 2 * 2 * tm * A_WIDTH * 2
    vmem = weights + tiles + 2 * (6 * sub_rows * D_MODEL * 4 + 4 * sub_rows * ff_chunk * 4)
    return pl.pallas_call(
        functools.partial(_out_ffn_kernel, ff_chunk=ff_chunk, sub_rows=sub_rows),
        out_shape=jax.ShapeDtypeStruct((n_tok, D_MODEL), jnp.float32),
        grid=(n_tok // tm,),
        in_specs=[
            pl.BlockSpec((tm, D_MODEL), row),
            pl.BlockSpec((tm, A_WIDTH), row),
            pl.BlockSpec((tm, B_WIDTH), row),
            pl.BlockSpec((1, A_WIDTH), const),
            pl.BlockSpec((1, B_WIDTH), const),
            resident((D_MODEL, D_MODEL)),
            pl.BlockSpec((1, D_MODEL), const),
            resident((D_MODEL, d_ff)),
            resident((D_MODEL, d_ff)),
            resident((d_ff, D_MODEL)),
            pl.BlockSpec((1, D_MODEL), const),
        ],
        out_specs=pl.BlockSpec((tm, D_MODEL), row),
        compiler_params=pltpu.CompilerParams(
            dimension_semantics=("arbitrary",), vmem_limit_bytes=_vmem_limit(vmem + (8 << 20))),
        name="out_ffn",
    )(x2d, o_na, o_swa, g_na, g_swa, wo, g_ffn, wg, wu, wd, g_fin)


def kernel(x, g_norm_mix, w_in, na_rpb, swa_sink, g_out_na, g_out_swa, w_out, g_norm_ffn, w_gate, w_up, w_down,
           g_final):
    batch, seq, d_model = x.shape
    assert w_in.shape[0] == 1, "the final RMSNorm is fused into the single layer's FFN call"
    assert d_model == D_MODEL and seq % Q_BLK == 0 and seq // GRID_W >= NA_MAX_KEY_ROWS
    assert B_KV_WIDTH == LANES, "the windowed k (and v) heads together fill one lane tile"
    rope = _rope_tables(seq)
    x2d = x.reshape(batch * seq, d_model)
    proj, (wo, wg, wu, wd) = _in_proj(x2d, g_norm_mix[0][None, :], w_in[0], rope, seq,
                                      (w_out[0], w_gate[0], w_up[0], w_down[0]))
    o_na, o_swa = _attention(proj, na_rpb[0].reshape(-1), swa_sink[0], seq, batch)
    out = _out_ffn(x2d, o_na, o_swa, g_out_na[0][None, :], g_out_swa[0][None, :], wo,
                   g_norm_ffn[0][None, :], wg, wu, wd, g_final[None, :])
    return out.reshape(batch, seq, d_model)
```

```python
import functools
from typing import NamedTuple

import numpy as np
import jax
import jax.numpy as jnp
from jax import lax
from jax.experimental import pallas as pl
from jax.experimental.pallas import tpu as pltpu

D_MODEL = 1024
HEAD_DIM = 64
NA_HEADS = 8
SWA_HEADS = 8
SWA_KV_HEADS = 2
A_WIDTH = NA_HEADS * HEAD_DIM
B_WIDTH = SWA_HEADS * HEAD_DIM
B_KV_WIDTH = SWA_KV_HEADS * HEAD_DIM
IN_WIDTH = 3 * A_WIDTH + B_WIDTH + 2 * B_KV_WIDTH
GRID_W = 64
NA_ROWS = 8
NA_COLS = 16
SWA_WINDOW = 128
ROPE_THETA = 500000.0
ROPE_DIM = HEAD_DIM // 4
RMS_EPS = 1e-6
NEG_INF = -1e30
LOG2E = 1.4426950408889634
Q_SCALE = HEAD_DIM ** -0.5 * LOG2E

LANES = 128
VMEM_CAPACITY = 64 * 1024 * 1024

PROJ_WIDTH = 3 * A_WIDTH + B_WIDTH + 2 * SWA_KV_HEADS * LANES
COL_QA, COL_KA, COL_VA = 0, A_WIDTH, 2 * A_WIDTH
COL_QB = 3 * A_WIDTH
COL_KD = COL_QB + B_WIDTH
COL_VD = COL_KD + SWA_KV_HEADS * LANES
W_COL_KB = COL_QB + B_WIDTH

Q_BLK = 128
NA_Q_ROWS = Q_BLK // GRID_W
NA_MAX_KEY_ROWS = NA_ROWS + NA_Q_ROWS - 1
NA_MAX_KEYS = NA_MAX_KEY_ROWS * GRID_W
NA_CLASSES = 5
ATTN_SKEW = 4
IN_PROJ_ROWS = 1024
FFN_ROWS = 1024
FFN_SUB_ROWS = 512
FFN_CHUNK = 256


def _vmem_limit(nbytes):
    return int(min(VMEM_CAPACITY - (2 << 20), max(nbytes, 16 << 20)))


def _attn_vmem_limit(nbytes):
    assert nbytes + (16 << 20) <= VMEM_CAPACITY - (2 << 20)
    return VMEM_CAPACITY - (2 << 20)


def _na_class_of_block(i, nblk):
    if i < 2:
        return i
    if i >= nblk - 2:
        return NA_CLASSES - (nblk - i)
    return 2


def _na_key_rows(i, rows):
    first = NA_Q_ROWS * i
    lo = int(np.clip(first - NA_ROWS // 2, 0, rows - NA_ROWS))
    hi = int(np.clip(first + NA_Q_ROWS - 1 - NA_ROWS // 2, 0, rows - NA_ROWS)) + NA_ROWS
    return lo, hi


def _na_static_tables(seq):
    rows = seq // GRID_W
    nblk = seq // Q_BLK
    masks = np.full((NA_CLASSES, NA_MAX_KEYS, Q_BLK), NEG_INF, np.float32)
    ridx = np.full((NA_CLASSES, NA_Q_ROWS, NA_MAX_KEY_ROWS), -1, np.int64)
    seen = {}
    for i in range(nblk):
        cls = _na_class_of_block(i, nblk)
        lo, hi = _na_key_rows(i, rows)
        assert (lo * GRID_W) % LANES == 0 and hi - lo <= NA_MAX_KEY_ROWS
        q = np.arange(Q_BLK) + i * Q_BLK
        qr, qc = q // GRID_W, q % GRID_W
        k = np.arange((hi - lo) * GRID_W) + lo * GRID_W
        kr, kc = k // GRID_W, k % GRID_W
        rs = np.clip(qr - NA_ROWS // 2, 0, rows - NA_ROWS)
        cs = np.clip(qc - NA_COLS // 2, 0, GRID_W - NA_COLS)
        ok_r = (kr[:, None] >= rs[None, :]) & (kr[:, None] < rs[None, :] + NA_ROWS)
        ok_c = (kc[:, None] >= cs[None, :]) & (kc[:, None] < cs[None, :] + NA_COLS)
        m = np.full((NA_MAX_KEYS, Q_BLK), NEG_INF, np.float32)
        m[:k.size] = np.where(ok_r & ok_c, 0.0, NEG_INF)
        rr = np.full((NA_Q_ROWS, NA_MAX_KEY_ROWS), -1, np.int64)
        for a in range(NA_Q_ROWS):
            for b in range(hi - lo):
                r, krow = NA_Q_ROWS * i + a, lo + b
                if rs[a * GRID_W] <= krow < rs[a * GRID_W] + NA_ROWS:
                    rr[a, b] = krow - r + NA_ROWS - 1
        if cls in seen:
            assert np.array_equal(seen[cls][0], m) and np.array_equal(seen[cls][1], rr)
        seen[cls] = (m, rr)
        masks[cls], ridx[cls] = m, rr
    return masks, ridx


def _swa_edge_masks():
    assert SWA_WINDOW == Q_BLK
    key = np.arange(Q_BLK)[:, None]
    qry = np.arange(Q_BLK)[None, :]
    tri = np.stack([np.where(key >= qry, 0.0, NEG_INF), np.where(key <= qry, 0.0, NEG_INF)]).astype(np.float32)
    return np.ascontiguousarray(np.concatenate([tri, tri], axis=2))


def _rope_tables(seq):
    half = ROPE_DIM // 2
    inv_freq = np.float32(ROPE_THETA) ** (-np.arange(0, ROPE_DIM, 2, dtype=np.float32) / np.float32(ROPE_DIM))
    ang = np.arange(seq, dtype=np.float32)[:, None] * inv_freq[None, :].astype(np.float32)
    cos, sin = np.cos(ang).astype(np.float32), np.sin(ang).astype(np.float32)
    ones = np.ones((seq, HEAD_DIM - ROPE_DIM), np.float32)
    zeros = np.zeros((seq, HEAD_DIM - ROPE_DIM), np.float32)
    zh = np.zeros((seq, half), np.float32)
    c = np.concatenate([cos, cos, ones], axis=1)
    sa = np.concatenate([-sin, zh, zeros], axis=1)
    sb = np.concatenate([zh, sin, zeros], axis=1)
    rep = LANES // HEAD_DIM
    return tuple(jnp.asarray(np.tile(t, (1, rep))) for t in (c, sa, sb))


def _rms(x, g):
    return x * lax.rsqrt(jnp.mean(x * x, axis=-1, keepdims=True) + RMS_EPS) * g


def _rope_tile(a, c, sa, sb):
    return a * c + pltpu.roll(a, LANES - ROPE_DIM // 2, axis=1) * sa + pltpu.roll(a, ROPE_DIM // 2, axis=1) * sb


def _in_proj_kernel(x_ref, g_ref, w_ref, c_ref, sa_ref, sb_ref, *rest):
    n_extra = (len(rest) - 2) // 2
    extra_in, o_ref, extra_out, w_scr = rest[:n_extra], rest[n_extra], rest[n_extra + 1:-1], rest[-1]

    @pl.when(pl.program_id(0) == 0)
    def _():
        for c0 in range(0, w_ref.shape[1], LANES):
            w_scr[:, c0:c0 + LANES] = w_ref[:, c0:c0 + LANES].astype(w_scr.dtype)

    for src, dst in zip(extra_in, extra_out):
        dst[...] = src[...].astype(dst.dtype)

    h = _rms(x_ref[...], g_ref[...]).astype(jnp.bfloat16)
    c, sa, sb = c_ref[...], sa_ref[...], sb_ref[...]

    def proj(col, width):
        return jnp.dot(h, w_scr[:, col:col + width], preferred_element_type=jnp.float32)

    kv = proj(W_COL_KB, 2 * B_KV_WIDTH)
    left = lax.broadcasted_iota(jnp.int32, (h.shape[0], LANES), 1) < HEAD_DIM
    for col, tile in ((COL_KD, _rope_tile(kv[:, :B_KV_WIDTH], c, sa, sb)), (COL_VD, kv[:, B_KV_WIDTH:])):
        swapped = pltpu.roll(tile, HEAD_DIM, axis=1)
        o_ref[:, col:col + LANES] = jnp.where(left, tile, swapped).astype(o_ref.dtype)
        o_ref[:, col + LANES:col + 2 * LANES] = jnp.where(left, swapped, tile).astype(o_ref.dtype)
    qb = proj(COL_QB, B_WIDTH)
    for t in range(B_WIDTH // LANES):
        lo = t * LANES
        tile = _rope_tile(qb[:, lo:lo + LANES], c, sa, sb) * Q_SCALE
        o_ref[:, COL_QB + lo:COL_QB + lo + LANES] = tile.astype(o_ref.dtype)
    o_ref[:, COL_QA:COL_QA + A_WIDTH] = (proj(COL_QA, A_WIDTH) * Q_SCALE).astype(o_ref.dtype)
    o_ref[:, COL_KA:COL_KA + A_WIDTH] = proj(COL_KA, A_WIDTH).astype(o_ref.dtype)
    o_ref[:, COL_VA:COL_VA + A_WIDTH] = proj(COL_VA, A_WIDTH).astype(o_ref.dtype)


def _in_proj(x2d, g, w_in, rope, seq, later_weights):
    n_tok = x2d.shape[0]
    tm = IN_PROJ_ROWS
    steps = n_tok // tm
    pos_blocks = seq // tm
    const = lambda i: (0, 0)
    row = lambda i: (i, 0)
    pos = lambda i: (i % pos_blocks, 0)
    slices = [(w.shape[0] // steps, w.shape[1]) for w in later_weights]
    assert all(w.shape[0] % steps == 0 and rows % 16 == 0 for w, (rows, _) in zip(later_weights, slices))
    slice_bytes = sum(rows * cols for rows, cols in slices)
    vmem = (2 * tm * D_MODEL * 4 + D_MODEL * IN_WIDTH * (4 + 2) + 2 * tm * PROJ_WIDTH * 2 + 3 * tm * B_WIDTH * 4 * 2
            + 2 * slice_bytes * (4 + 2) + 3 * 2 * tm * LANES * 4)
    outs = pl.pallas_call(
        _in_proj_kernel,
        out_shape=[jax.ShapeDtypeStruct((n_tok, PROJ_WIDTH), jnp.bfloat16)]
        + [jax.ShapeDtypeStruct(w.shape, jnp.bfloat16) for w in later_weights],
        grid=(steps,),
        in_specs=[
            pl.BlockSpec((tm, D_MODEL), row),
            pl.BlockSpec((1, D_MODEL), const),
            pl.BlockSpec((D_MODEL, IN_WIDTH), const, pipeline_mode=pl.Buffered(1)),
            pl.BlockSpec((tm, LANES), pos),
            pl.BlockSpec((tm, LANES), pos),
            pl.BlockSpec((tm, LANES), pos),
        ] + [pl.BlockSpec(blk, row) for blk in slices],
        out_specs=[pl.BlockSpec((tm, PROJ_WIDTH), row)] + [pl.BlockSpec(blk, row) for blk in slices],
        scratch_shapes=[pltpu.VMEM((D_MODEL, IN_WIDTH), jnp.bfloat16)],
        compiler_params=pltpu.CompilerParams(
            dimension_semantics=("arbitrary",), vmem_limit_bytes=_vmem_limit(vmem + (8 << 20))),
        name="in_proj",
    )(x2d, g, w_in, *rope, *later_weights)
    return outs[0], outs[1:]


def _transpose_keys(v_ref, vt_ref, ones_below=False):
    top = lax.broadcasted_iota(jnp.int32, (LANES, LANES), 0) < HEAD_DIM
    for j in range(v_ref.shape[0] // LANES):
        tile = v_ref[j * LANES:(j + 1) * LANES, :].T
        vt_ref[j] = jnp.where(top, tile, jnp.ones_like(tile)) if ones_below else tile


class _Stream(NamedTuple):
    q_ref: object
    k_ref: object
    vt_ref: object
    o_ref: object
    sink_row: object = None
    sums_below: bool = False


class _Block(NamedTuple):
    stream: _Stream
    q0: int
    k0: int
    n_keys: int
    add_bias: object


def _attend(blocks, skew):
    lane = lax.broadcasted_iota(jnp.int32, (Q_BLK, LANES), 1)
    left = lane < HEAD_DIM
    top = lax.broadcasted_iota(jnp.int32, (LANES, Q_BLK), 0) < HEAD_DIM

    def scores_of(blk):
        q2 = blk.stream.q_ref[blk.q0:blk.q0 + Q_BLK, :]
        zero = jnp.zeros_like(q2)
        qq = jnp.concatenate([jnp.where(left, q2, zero), jnp.where(left, zero, q2)], axis=0)
        st = lax.dot_general(blk.stream.k_ref[blk.k0:blk.k0 + blk.n_keys, :], qq, (((1,), (1,)), ((), ())),
                             preferred_element_type=jnp.float32)
        return blk.add_bias(st)

    def finish(blk, st):
        _, _, vt_ref, o_ref, sink_row, sums_below = blk.stream
        m = jnp.max(st, axis=0, keepdims=True)
        if sink_row is not None:
            m = jnp.maximum(m, sink_row)
        p = jnp.exp2(st - m)
        tile0, full, part = blk.k0 // LANES, blk.n_keys // LANES, blk.n_keys % LANES
        pieces = [vt_ref[tile0 + j] for j in range(full)]
        if part:
            pieces.append(vt_ref[tile0 + full][:, :part])
        ot = jnp.dot(jnp.concatenate(pieces, axis=1), p.astype(jnp.bfloat16), preferred_element_type=jnp.float32)
        den = ot[HEAD_DIM:HEAD_DIM + 1, :] if sums_below else jnp.sum(p, axis=0, keepdims=True)
        if sink_row is not None:
            den = den + jnp.exp2(sink_row - m)
        if sums_below:
            ot = ot[:HEAD_DIM, :] / den
            x = jnp.concatenate([ot[:, :Q_BLK], ot[:, Q_BLK:]], axis=0)
        else:
            ot = ot / den
            x = jnp.where(top, ot[:, :Q_BLK], ot[:, Q_BLK:])
        o_ref[blk.q0:blk.q0 + Q_BLK, :] = x.T.astype(o_ref.dtype)

    pending = [scores_of(blk) for blk in blocks[:skew]]
    for c, blk in enumerate(blocks):
        st = pending.pop(0)
        if c + skew < len(blocks):
            pending.append(scores_of(blocks[c + skew]))
        finish(blk, st)


def _build_na_bias(rpb_ref, mask_ref, t_scr, hp, ridx):
    n_rrow, n_rcol = 2 * NA_ROWS - 1, 2 * NA_COLS - 1
    sublanes = 8
    lane8 = lax.broadcasted_iota(jnp.int32, (sublanes, LANES), 1)
    second_row = lax.broadcasted_iota(jnp.int32, (GRID_W, LANES), 1) >= GRID_W
    used_rows = sorted({int(r) for r in ridx.reshape(-1) if r >= 0})
    zero = jnp.zeros((GRID_W, LANES), jnp.float32)
    for hh in range(2):
        h = 2 * hp + hh
        toeplitz = {-1: zero}
        for rho in used_rows:
            band = jnp.zeros((sublanes, LANES), jnp.float32)
            for d in range(n_rcol):
                band = jnp.where(lane8 == (NA_COLS - 1 - d) % LANES, rpb_ref[(h * n_rrow + rho) * n_rcol + d] * LOG2E, band)
            rows = jnp.concatenate([band] * (GRID_W // sublanes), axis=0)
            first = pltpu.roll(rows, 0, axis=1, stride=1, stride_axis=0)
            second = pltpu.roll(rows, GRID_W, axis=1, stride=1, stride_axis=0)
            toeplitz[rho] = jnp.where(second_row, second, first)
        for cls in range(NA_CLASSES):
            for b in range(NA_MAX_KEY_ROWS):
                blk = jnp.where(second_row, toeplitz[int(ridx[cls, 1, b])], toeplitz[int(ridx[cls, 0, b])])
                ks = slice(b * GRID_W, (b + 1) * GRID_W)
                t_scr[cls, ks, hh * Q_BLK:(hh + 1) * Q_BLK] = blk + mask_ref[cls, ks, :]


def _swa_add_mask(st, edge_ref, first_tile, n):
    parts = []
    for j in range(st.shape[0] // Q_BLK):
        part = st[j * Q_BLK:(j + 1) * Q_BLK, :]
        if first_tile + j != n:
            part = part + edge_ref[0 if first_tile + j < n else 1]
        parts.append(part)
    return jnp.concatenate(parts, axis=0)


def _attn_kernel(rpb_ref, sink_ref, qa_ref, ka_ref, va_ref, qb_ref, kd_ref, vd_ref, mask_ref, edge_ref,
                 oa_ref, ob_ref, t_scr, vta_scr, vtb_scr, *, ridx, seq):
    pair = pl.program_id(0)
    nblk = seq // Q_BLK

    @pl.when(pl.program_id(1) == 0)
    def _():
        _build_na_bias(rpb_ref, mask_ref, t_scr, pair, ridx)

    _transpose_keys(va_ref, vta_scr)
    _transpose_keys(vd_ref, vtb_scr, ones_below=True)
    lane = lax.broadcasted_iota(jnp.int32, (1, 2 * Q_BLK), 1)
    sink_row = jnp.where(lane < Q_BLK, sink_ref[2 * pair], sink_ref[2 * pair + 1]) * LOG2E
    na = _Stream(qa_ref, ka_ref, vta_scr, oa_ref)
    swa = _Stream(qb_ref, kd_ref, vtb_scr, ob_ref, sink_row, sums_below=True)
    blocks = []
    for i in range(nblk):
        lo, hi = _na_key_rows(i, seq // GRID_W)
        n_keys = (hi - lo) * GRID_W
        add_bias = functools.partial(lambda st, cls, n: st + t_scr[cls, :n, :], cls=_na_class_of_block(i, nblk), n=n_keys)
        blocks.append(_Block(na, i * Q_BLK, lo * GRID_W, n_keys, add_bias))
        first, last = max(i - 1, 0), min(i + 1, nblk - 1)
        add_mask = functools.partial(_swa_add_mask, edge_ref=edge_ref, first_tile=first, n=i)
        blocks.append(_Block(swa, i * Q_BLK, first * Q_BLK, (last - first + 1) * Q_BLK, add_mask))
    _attend(blocks, ATTN_SKEW)


def _attention(proj, rpb_flat, sink, seq, batch):
    masks, ridx = _na_static_tables(seq)
    edges = _swa_edge_masks()
    n_tok = proj.shape[0]
    n_pairs = A_WIDTH // LANES
    assert B_WIDTH // LANES == n_pairs
    pairs_per_kv = n_pairs // SWA_KV_HEADS
    col = lambda base: (lambda p, b: (b, base // LANES + p))
    kv_col = lambda base: (lambda p, b: (b, base // LANES + p // pairs_per_kv))
    tile = lambda index_map: pl.BlockSpec((seq, LANES), index_map)
    whole = lambda a: pl.BlockSpec(a.shape, lambda p, b: (0,) * a.ndim)
    table_bytes = NA_CLASSES * NA_MAX_KEYS * 2 * Q_BLK * 4
    vmem = 2 * 8 * seq * LANES * 2 + 2 * (masks.size + edges.size) * 4 + table_bytes + 2 * seq * LANES * 2
    out = jax.ShapeDtypeStruct((n_tok, A_WIDTH), jnp.bfloat16)
    return pl.pallas_call(
        functools.partial(_attn_kernel, ridx=ridx, seq=seq),
        out_shape=[out, out],
        grid=(n_pairs, batch),
        in_specs=[
            pl.BlockSpec(memory_space=pltpu.SMEM),
            pl.BlockSpec(memory_space=pltpu.SMEM),
            tile(col(COL_QA)), tile(col(COL_KA)), tile(col(COL_VA)),
            tile(col(COL_QB)), tile(kv_col(COL_KD)), tile(kv_col(COL_VD)),
            whole(masks), whole(edges),
        ],
        out_specs=[tile(lambda p, b: (b, p)), tile(lambda p, b: (b, p))],
        scratch_shapes=[pltpu.VMEM((NA_CLASSES, NA_MAX_KEYS, 2 * Q_BLK), jnp.float32),
                        pltpu.VMEM((seq // LANES, LANES, LANES), jnp.bfloat16),
                        pltpu.VMEM((seq // LANES, LANES, LANES), jnp.bfloat16)],
        compiler_params=pltpu.CompilerParams(
            dimension_semantics=("arbitrary", "arbitrary"), vmem_limit_bytes=_attn_vmem_limit(vmem)),
        name="attention",
    )(rpb_flat, sink, proj, proj, proj, proj, proj, proj, jnp.asarray(masks), jnp.asarray(edges))


def _out_ffn_kernel(x_ref, oa_ref, ob_ref, gna_ref, gswa_ref, wo_ref, gffn_ref, wg_ref, wu_ref, wd_ref,
                    gfin_ref, o_ref, *, ff_chunk, sub_rows):
    d_ff = wg_ref.shape[1]
    row_slices = [slice(r0, r0 + sub_rows) for r0 in range(0, x_ref.shape[0], sub_rows)]
    mixed = []
    for rows in row_slices:
        na = _rms(oa_ref[rows, :].astype(jnp.float32), gna_ref[...]).astype(jnp.bfloat16)
        nb = _rms(ob_ref[rows, :].astype(jnp.float32), gswa_ref[...]).astype(jnp.bfloat16)
        x1 = (x_ref[rows, :]
              + jnp.dot(na, wo_ref[:A_WIDTH, :], preferred_element_type=jnp.float32)
              + jnp.dot(nb, wo_ref[A_WIDTH:, :], preferred_element_type=jnp.float32))
        mixed.append((x1, _rms(x1, gffn_ref[...]).astype(jnp.bfloat16)))
    for rows, (x1, h) in zip(row_slices, mixed):
        acc = jnp.zeros(x1.shape, jnp.float32)
        for c0 in range(0, d_ff, ff_chunk):
            gate = jnp.dot(h, wg_ref[:, c0:c0 + ff_chunk], preferred_element_type=jnp.float32)
            up = jnp.dot(h, wu_ref[:, c0:c0 + ff_chunk], preferred_element_type=jnp.float32)
            act = (jax.nn.silu(gate) * up).astype(jnp.bfloat16)
            acc = acc + jnp.dot(act, wd_ref[c0:c0 + ff_chunk, :], preferred_element_type=jnp.float32)
        o_ref[rows, :] = _rms(x1 + acc, gfin_ref[...])


def _out_ffn(x2d, o_na, o_swa, g_na, g_swa, wo, g_ffn, wg, wu, wd, g_fin):
    n_tok = x2d.shape[0]
    tm, sub_rows, ff_chunk = FFN_ROWS, FFN_SUB_ROWS, FFN_CHUNK
    d_ff = wg.shape[1]
    const = lambda i: (0, 0)
    row = lambda i: (i, 0)
    resident = lambda shape: pl.BlockSpec(shape, const, pipeline_mode=pl.Buffered(1))
    weights = (D_MODEL * D_MODEL + 3 * D_MODEL * d_ff) * 2
    tiles = 2 * 2 * tm * D_MODEL * 4 + 2 * 2 * tm * A_WIDTH * 2
    vmem = weights + tiles + 2 * (6 * sub_rows * D_MODEL * 4 + 4 * sub_rows * ff_chunk * 4)
    return pl.pallas_call(
        functools.partial(_out_ffn_kernel, ff_chunk=ff_chunk, sub_rows=sub_rows),
        out_shape=jax.ShapeDtypeStruct((n_tok, D_MODEL), jnp.float32),
        grid=(n_tok // tm,),
        in_specs=[
            pl.BlockSpec((tm, D_MODEL), row),
            pl.BlockSpec((tm, A_WIDTH), row),
            pl.BlockSpec((tm, B_WIDTH), row),
            pl.BlockSpec((1, A_WIDTH), const),
            pl.BlockSpec((1, B_WIDTH), const),
            resident((D_MODEL, D_MODEL)),
            pl.BlockSpec((1, D_MODEL), const),
            resident((D_MODEL, d_ff)),
            resident((D_MODEL, d_ff)),
            resident((d_ff, D_MODEL)),
            pl.BlockSpec((1, D_MODEL), const),
        ],
        out_specs=pl.BlockSpec((tm, D_MODEL), row),
        compiler_params=pltpu.CompilerParams(
            dimension_semantics=("arbitrary",), vmem_limit_bytes=_vmem_limit(vmem + (8 << 20))),
        name="out_ffn",
    )(x2d, o_na, o_swa, g_na, g_swa, wo, g_ffn, wg, wu, wd, g_fin)


def kernel(x, g_norm_mix, w_in, na_rpb, swa_sink, g_out_na, g_out_swa, w_out, g_norm_ffn, w_gate, w_up, w_down,
           g_final):
    batch, seq, d_model = x.shape
    assert w_in.shape[0] == 1, "the final RMSNorm is fused into the single layer's FFN call"
    assert d_model == D_MODEL and seq % Q_BLK == 0 and seq // GRID_W >= NA_MAX_KEY_ROWS
    assert B_KV_WIDTH == LANES, "the windowed k (and v) heads together fill one lane tile"
    rope = _rope_tables(seq)
    x2d = x.reshape(batch * seq, d_model)
    proj, (wo, wg, wu, wd) = _in_proj(x2d, g_norm_mix[0][None, :], w_in[0], rope, seq,
                                      (w_out[0], w_gate[0], w_up[0], w_down[0]))
    o_na, o_swa = _attention(proj, na_rpb[0].reshape(-1), swa_sink[0], seq, batch)
    out = _out_ffn(x2d, o_na, o_swa, g_out_na[0][None, :], g_out_swa[0][None, :], wo,
                   g_norm_ffn[0][None, :], wg, wu, wd, g_final[None, :])
    return out.reshape(batch, seq, d_model)
```

```python
import functools
from typing import NamedTuple

import numpy as np
import jax
import jax.numpy as jnp
from jax import lax
from jax.experimental import pallas as pl
from jax.experimental.pallas import tpu as pltpu

D_MODEL = 1024
HEAD_DIM = 64
NA_HEADS = 8
SWA_HEADS = 8
SWA_KV_HEADS = 2
A_WIDTH = NA_HEADS * HEAD_DIM
B_WIDTH = SWA_HEADS * HEAD_DIM
B_KV_WIDTH = SWA_KV_HEADS * HEAD_DIM
IN_WIDTH = 3 * A_WIDTH + B_WIDTH + 2 * B_KV_WIDTH
GRID_W = 64
NA_ROWS = 8
NA_COLS = 16
SWA_WINDOW = 128
ROPE_THETA = 500000.0
ROPE_DIM = HEAD_DIM // 4
RMS_EPS = 1e-6
NEG_INF = -1e30
LOG2E = 1.4426950408889634
Q_SCALE = HEAD_DIM ** -0.5 * LOG2E

LANES = 128
VMEM_CAPACITY = 64 * 1024 * 1024

PROJ_WIDTH = 3 * A_WIDTH + B_WIDTH + 2 * SWA_KV_HEADS * LANES
COL_QA, COL_KA, COL_VA = 0, A_WIDTH, 2 * A_WIDTH
COL_QB = 3 * A_WIDTH
COL_KD = COL_QB + B_WIDTH
COL_VD = COL_KD + SWA_KV_HEADS * LANES
W_COL_KB = COL_QB + B_WIDTH

Q_BLK = 128
NA_Q_ROWS = Q_BLK // GRID_W
NA_MAX_KEY_ROWS = NA_ROWS + NA_Q_ROWS - 1
NA_MAX_KEYS = NA_MAX_KEY_ROWS * GRID_W
NA_CLASSES = 5
ATTN_SKEW = 6
ATTN_PAIRS = 2
IN_PROJ_ROWS = 1024
FFN_ROWS = 1024
FFN_SUB_ROWS = 512
FFN_CHUNK = 256


def _vmem_limit(nbytes):
    return int(min(VMEM_CAPACITY - (2 << 20), max(nbytes, 16 << 20)))


def _attn_vmem_limit(nbytes):
    assert nbytes + (16 << 20) <= VMEM_CAPACITY - (2 << 20)
    return VMEM_CAPACITY - (2 << 20)


def _na_class_of_block(i, nblk):
    if i < 2:
        return i
    if i >= nblk - 2:
        return NA_CLASSES - (nblk - i)
    return 2


def _na_key_rows(i, rows):
    first = NA_Q_ROWS * i
    lo = int(np.clip(first - NA_ROWS // 2, 0, rows - NA_ROWS))
    hi = int(np.clip(first + NA_Q_ROWS - 1 - NA_ROWS // 2, 0, rows - NA_ROWS)) + NA_ROWS
    return lo, hi


def _na_static_tables(seq):
    rows = seq // GRID_W
    nblk = seq // Q_BLK
    masks = np.full((NA_CLASSES, NA_MAX_KEYS, Q_BLK), NEG_INF, np.float32)
    ridx = np.full((NA_CLASSES, NA_Q_ROWS, NA_MAX_KEY_ROWS), -1, np.int64)
    seen = {}
    for i in range(nblk):
        cls = _na_class_of_block(i, nblk)
        lo, hi = _na_key_rows(i, rows)
        assert (lo * GRID_W) % LANES == 0 and hi - lo <= NA_MAX_KEY_ROWS
        q = np.arange(Q_BLK) + i * Q_BLK
        qr, qc = q // GRID_W, q % GRID_W
        k = np.arange((hi - lo) * GRID_W) + lo * GRID_W
        kr, kc = k // GRID_W, k % GRID_W
        rs = np.clip(qr - NA_ROWS // 2, 0, rows - NA_ROWS)
        cs = np.clip(qc - NA_COLS // 2, 0, GRID_W - NA_COLS)
        ok_r = (kr[:, None] >= rs[None, :]) & (kr[:, None] < rs[None, :] + NA_ROWS)
        ok_c = (kc[:, None] >= cs[None, :]) & (kc[:, None] < cs[None, :] + NA_COLS)
        m = np.full((NA_MAX_KEYS, Q_BLK), NEG_INF, np.float32)
        m[:k.size] = np.where(ok_r & ok_c, 0.0, NEG_INF)
        rr = np.full((NA_Q_ROWS, NA_MAX_KEY_ROWS), -1, np.int64)
        for a in range(NA_Q_ROWS):
            for b in range(hi - lo):
                r, krow = NA_Q_ROWS * i + a, lo + b
                if rs[a * GRID_W] <= krow < rs[a * GRID_W] + NA_ROWS:
                    rr[a, b] = krow - r + NA_ROWS - 1
        if cls in seen:
            assert np.array_equal(seen[cls][0], m) and np.array_equal(seen[cls][1], rr)
        seen[cls] = (m, rr)
        masks[cls], ridx[cls] = m, rr
    return masks, ridx


def _swa_edge_masks():
    assert SWA_WINDOW == Q_BLK
    key = np.arange(Q_BLK)[:, None]
    qry = np.arange(Q_BLK)[None, :]
    tri = np.stack([np.where(key >= qry, 0.0, NEG_INF), np.where(key <= qry, 0.0, NEG_INF)]).astype(np.float32)
    return np.ascontiguousarray(np.concatenate([tri, tri], axis=2))


def _rope_tables(seq):
    half = ROPE_DIM // 2
    inv_freq = np.float32(ROPE_THETA) ** (-np.arange(0, ROPE_DIM, 2, dtype=np.float32) / np.float32(ROPE_DIM))
    ang = np.arange(seq, dtype=np.float32)[:, None] * inv_freq[None, :].astype(np.float32)
    cos, sin = np.cos(ang).astype(np.float32), np.sin(ang).astype(np.float32)
    ones = np.ones((seq, HEAD_DIM - ROPE_DIM), np.float32)
    zeros = np.zeros((seq, HEAD_DIM - ROPE_DIM), np.float32)
    zh = np.zeros((seq, half), np.float32)
    c = np.concatenate([cos, cos, ones], axis=1)
    sa = np.concatenate([-sin, zh, zeros], axis=1)
    sb = np.concatenate([zh, sin, zeros], axis=1)
    rep = LANES // HEAD_DIM
    return tuple(jnp.asarray(np.tile(t, (1, rep))) for t in (c, sa, sb))


def _rms(x, g):
    return x * lax.rsqrt(jnp.mean(x * x, axis=-1, keepdims=True) + RMS_EPS) * g


def _rope_tile(a, c, sa, sb):
    return a * c + pltpu.roll(a, LANES - ROPE_DIM // 2, axis=1) * sa + pltpu.roll(a, ROPE_DIM // 2, axis=1) * sb


def _in_proj_kernel(x_ref, g_ref, w_ref, c_ref, sa_ref, sb_ref, *rest):
    n_extra = (len(rest) - 2) // 2
    extra_in, o_ref, extra_out, w_scr = rest[:n_extra], rest[n_extra], rest[n_extra + 1:-1], rest[-1]

    @pl.when(pl.program_id(0) == 0)
    def _():
        for c0 in range(0, w_ref.shape[1], LANES):
            w_scr[:, c0:c0 + LANES] = w_ref[:, c0:c0 + LANES].astype(w_scr.dtype)

    for src, dst in zip(extra_in, extra_out):
        dst[...] = src[...].astype(dst.dtype)

    h = _rms(x_ref[...], g_ref[...]).astype(jnp.bfloat16)
    c, sa, sb = c_ref[...], sa_ref[...], sb_ref[...]

    def proj(col, width):
        return jnp.dot(h, w_scr[:, col:col + width], preferred_element_type=jnp.float32)

    kv = proj(W_COL_KB, 2 * B_KV_WIDTH)
    left = lax.broadcasted_iota(jnp.int32, (h.shape[0], LANES), 1) < HEAD_DIM
    for col, tile in ((COL_KD, _rope_tile(kv[:, :B_KV_WIDTH], c, sa, sb)), (COL_VD, kv[:, B_KV_WIDTH:])):
        swapped = pltpu.roll(tile, HEAD_DIM, axis=1)
        o_ref[:, col:col + LANES] = jnp.where(left, tile, swapped).astype(o_ref.dtype)
        o_ref[:, col + LANES:col + 2 * LANES] = jnp.where(left, swapped, tile).astype(o_ref.dtype)
    qb = proj(COL_QB, B_WIDTH)
    for t in range(B_WIDTH // LANES):
        lo = t * LANES
        tile = _rope_tile(qb[:, lo:lo + LANES], c, sa, sb) * Q_SCALE
        o_ref[:, COL_QB + lo:COL_QB + lo + LANES] = tile.astype(o_ref.dtype)
    o_ref[:, COL_QA:COL_QA + A_WIDTH] = (proj(COL_QA, A_WIDTH) * Q_SCALE).astype(o_ref.dtype)
    o_ref[:, COL_KA:COL_KA + A_WIDTH] = proj(COL_KA, A_WIDTH).astype(o_ref.dtype)
    o_ref[:, COL_VA:COL_VA + A_WIDTH] = proj(COL_VA, A_WIDTH).astype(o_ref.dtype)


def _in_proj(x2d, g, w_in, rope, seq, later_weights):
    n_tok = x2d.shape[0]
    tm = IN_PROJ_ROWS
    steps = n_tok // tm
    pos_blocks = seq // tm
    const = lambda i: (0, 0)
    row = lambda i: (i, 0)
    pos = lambda i: (i % pos_blocks, 0)
    slices = [(w.shape[0] // steps, w.shape[1]) for w in later_weights]
    assert all(w.shape[0] % steps == 0 and rows % 16 == 0 for w, (rows, _) in zip(later_weights, slices))
    slice_bytes = sum(rows * cols for rows, cols in slices)
    vmem = (2 * tm * D_MODEL * 4 + D_MODEL * IN_WIDTH * (4 + 2) + 2 * tm * PROJ_WIDTH * 2 + 3 * tm * B_WIDTH * 4 * 2
            + 2 * slice_bytes * (4 + 2) + 3 * 2 * tm * LANES * 4)
    outs = pl.pallas_call(
        _in_proj_kernel,
        out_shape=[jax.ShapeDtypeStruct((n_tok, PROJ_WIDTH), jnp.bfloat16)]
        + [jax.ShapeDtypeStruct(w.shape, jnp.bfloat16) for w in later_weights],
        grid=(steps,),
        in_specs=[
            pl.BlockSpec((tm, D_MODEL), row),
            pl.BlockSpec((1, D_MODEL), const),
            pl.BlockSpec((D_MODEL, IN_WIDTH), const, pipeline_mode=pl.Buffered(1)),
            pl.BlockSpec((tm, LANES), pos),
            pl.BlockSpec((tm, LANES), pos),
            pl.BlockSpec((tm, LANES), pos),
        ] + [pl.BlockSpec(blk, row) for blk in slices],
        out_specs=[pl.BlockSpec((tm, PROJ_WIDTH), row)] + [pl.BlockSpec(blk, row) for blk in slices],
        scratch_shapes=[pltpu.VMEM((D_MODEL, IN_WIDTH), jnp.bfloat16)],
        compiler_params=pltpu.CompilerParams(
            dimension_semantics=("arbitrary",), vmem_limit_bytes=_vmem_limit(vmem + (8 << 20))),
        name="in_proj",
    )(x2d, g, w_in, *rope, *later_weights)
    return outs[0], outs[1:]


def _transpose_keys(v_ref, vt_ref, ones_below=False):
    top = lax.broadcasted_iota(jnp.int32, (LANES, LANES), 0) < HEAD_DIM
    for j in range(v_ref.shape[0] // LANES):
        tile = v_ref[j * LANES:(j + 1) * LANES, :].T
        vt_ref[j] = jnp.where(top, tile, jnp.ones_like(tile)) if ones_below else tile


class _Stream(NamedTuple):
    q_ref: object
    k_ref: object
    vt_ref: object
    o_ref: object
    sink_row: object = None
    sums_below: bool = False


class _Block(NamedTuple):
    stream: _Stream
    q0: int
    k0: int
    n_keys: int
    add_bias: object


def _attend(blocks, skew):
    lane = lax.broadcasted_iota(jnp.int32, (Q_BLK, LANES), 1)
    left = lane < HEAD_DIM
    top = lax.broadcasted_iota(jnp.int32, (LANES, Q_BLK), 0) < HEAD_DIM

    def scores_of(blk):
        q2 = blk.stream.q_ref[blk.q0:blk.q0 + Q_BLK, :]
        zero = jnp.zeros_like(q2)
        qq = jnp.concatenate([jnp.where(left, q2, zero), jnp.where(left, zero, q2)], axis=0)
        st = lax.dot_general(blk.stream.k_ref[blk.k0:blk.k0 + blk.n_keys, :], qq, (((1,), (1,)), ((), ())),
                             preferred_element_type=jnp.float32)
        return blk.add_bias(st)

    def finish(blk, st):
        _, _, vt_ref, o_ref, sink_row, sums_below = blk.stream
        m = jnp.max(st, axis=0, keepdims=True)
        if sink_row is not None:
            m = jnp.maximum(m, sink_row)
        p = jnp.exp2(st - m)
        tile0, full, part = blk.k0 // LANES, blk.n_keys // LANES, blk.n_keys % LANES
        pieces = [vt_ref[tile0 + j] for j in range(full)]
        if part:
            pieces.append(vt_ref[tile0 + full][:, :part])
        ot = jnp.dot(jnp.concatenate(pieces, axis=1), p.astype(jnp.bfloat16), preferred_element_type=jnp.float32)
        den = ot[HEAD_DIM:HEAD_DIM + 1, :] if sums_below else jnp.sum(p, axis=0, keepdims=True)
        if sink_row is not None:
            den = den + jnp.exp2(sink_row - m)
        if sums_below:
            ot = ot[:HEAD_DIM, :] / den
            x = jnp.concatenate([ot[:, :Q_BLK], ot[:, Q_BLK:]], axis=0)
        else:
            ot = ot / den
            x = jnp.where(top, ot[:, :Q_BLK], ot[:, Q_BLK:])
        o_ref[blk.q0:blk.q0 + Q_BLK, :] = x.T.astype(o_ref.dtype)

    pending = [scores_of(blk) for blk in blocks[:skew]]
    for c, blk in enumerate(blocks):
        st = pending.pop(0)
        if c + skew < len(blocks):
            pending.append(scores_of(blocks[c + skew]))
        finish(blk, st)


def _build_na_bias(rpb_ref, mask_ref, t_scr, hp, ridx):
    n_rrow, n_rcol = 2 * NA_ROWS - 1, 2 * NA_COLS - 1
    sublanes = 8
    lane8 = lax.broadcasted_iota(jnp.int32, (sublanes, LANES), 1)
    second_row = lax.broadcasted_iota(jnp.int32, (GRID_W, LANES), 1) >= GRID_W
    used_rows = sorted({int(r) for r in ridx.reshape(-1) if r >= 0})
    zero = jnp.zeros((GRID_W, LANES), jnp.float32)
    for hh in range(2):
        h = 2 * hp + hh
        toeplitz = {-1: zero}
        for rho in used_rows:
            band = jnp.zeros((sublanes, LANES), jnp.float32)
            for d in range(n_rcol):
                band = jnp.where(lane8 == (NA_COLS - 1 - d) % LANES, rpb_ref[(h * n_rrow + rho) * n_rcol + d] * LOG2E, band)
            rows = jnp.concatenate([band] * (GRID_W // sublanes), axis=0)
            first = pltpu.roll(rows, 0, axis=1, stride=1, stride_axis=0)
            second = pltpu.roll(rows, GRID_W, axis=1, stride=1, stride_axis=0)
            toeplitz[rho] = jnp.where(second_row, second, first)
        for cls in range(NA_CLASSES):
            for b in range(NA_MAX_KEY_ROWS):
                blk = jnp.where(second_row, toeplitz[int(ridx[cls, 1, b])], toeplitz[int(ridx[cls, 0, b])])
                ks = slice(b * GRID_W, (b + 1) * GRID_W)
                t_scr[cls, ks, hh * Q_BLK:(hh + 1) * Q_BLK] = blk + mask_ref[cls, ks, :]


def _swa_add_mask(st, edge_ref, first_tile, n):
    parts = []
    for j in range(st.shape[0] // Q_BLK):
        part = st[j * Q_BLK:(j + 1) * Q_BLK, :]
        if first_tile + j != n:
            part = part + edge_ref[0 if first_tile + j < n else 1]
        parts.append(part)
    return jnp.concatenate(parts, axis=0)


def _attn_kernel(rpb_ref, sink_ref, qa_ref, ka_ref, va_ref, qb_ref, kd_ref, vd_ref, mask_ref, edge_ref,
                 oa_ref, ob_ref, t_scr, vta_scr, vtb_scr, *, ridx, seq):
    nblk = seq // Q_BLK
    pairs = [ATTN_PAIRS * pl.program_id(0) + j for j in range(ATTN_PAIRS)]
    lanes = [slice(j * LANES, (j + 1) * LANES) for j in range(ATTN_PAIRS)]

    @pl.when(pl.program_id(1) == 0)
    def _():
        for j, pair in enumerate(pairs):
            _build_na_bias(rpb_ref, mask_ref, t_scr.at[j], pair, ridx)

    for j in range(ATTN_PAIRS):
        _transpose_keys(va_ref.at[:, lanes[j]], vta_scr.at[j])
    _transpose_keys(vd_ref, vtb_scr, ones_below=True)
    lane = lax.broadcasted_iota(jnp.int32, (1, 2 * Q_BLK), 1)
    streams = []
    for j, pair in enumerate(pairs):
        sink_row = jnp.where(lane < Q_BLK, sink_ref[2 * pair], sink_ref[2 * pair + 1]) * LOG2E
        streams.append((_Stream(qa_ref.at[:, lanes[j]], ka_ref.at[:, lanes[j]], vta_scr.at[j], oa_ref.at[:, lanes[j]]),
                        _Stream(qb_ref.at[:, lanes[j]], kd_ref, vtb_scr, ob_ref.at[:, lanes[j]], sink_row, sums_below=True)))
    blocks = []
    for i in range(nblk):
        lo, hi = _na_key_rows(i, seq // GRID_W)
        n_keys = (hi - lo) * GRID_W
        first, last = max(i - 1, 0), min(i + 1, nblk - 1)
        add_mask = functools.partial(_swa_add_mask, edge_ref=edge_ref, first_tile=first, n=i)
        for j, (na, swa) in enumerate(streams):
            add_bias = functools.partial(lambda st, j, cls, n: st + t_scr[j, cls, :n, :],
                                         j=j, cls=_na_class_of_block(i, nblk), n=n_keys)
            blocks.append(_Block(na, i * Q_BLK, lo * GRID_W, n_keys, add_bias))
            blocks.append(_Block(swa, i * Q_BLK, first * Q_BLK, (last - first + 1) * Q_BLK, add_mask))
    _attend(blocks, ATTN_SKEW)


def _attention(proj, rpb_flat, sink, seq, batch):
    masks, ridx = _na_static_tables(seq)
    edges = _swa_edge_masks()
    n_tok = proj.shape[0]
    n_steps = A_WIDTH // (ATTN_PAIRS * LANES)
    assert B_WIDTH == A_WIDTH and B_WIDTH // (SWA_KV_HEADS * LANES) == ATTN_PAIRS, "one windowed kv head per step"
    width = ATTN_PAIRS * LANES
    col = lambda base: pl.BlockSpec((seq, width), lambda s, b: (b, base // width + s))
    kv_col = lambda base: pl.BlockSpec((seq, LANES), lambda s, b: (b, base // LANES + s))
    whole = lambda a: pl.BlockSpec(a.shape, lambda s, b: (0,) * a.ndim)
    table_bytes = ATTN_PAIRS * NA_CLASSES * NA_MAX_KEYS * 2 * Q_BLK * 4
    vmem = (2 * (6 * ATTN_PAIRS + 2) * seq * LANES * 2 + 2 * (masks.size + edges.size) * 4 + table_bytes
            + (ATTN_PAIRS + 1) * seq * LANES * 2)
    out = jax.ShapeDtypeStruct((n_tok, A_WIDTH), jnp.bfloat16)
    out_tile = pl.BlockSpec((seq, width), lambda s, b: (b, s))
    return pl.pallas_call(
        functools.partial(_attn_kernel, ridx=ridx, seq=seq),
        out_shape=[out, out],
        grid=(n_steps, batch),
        in_specs=[
            pl.BlockSpec(memory_space=pltpu.SMEM),
            pl.BlockSpec(memory_space=pltpu.SMEM),
            col(COL_QA), col(COL_KA), col(COL_VA), col(COL_QB), kv_col(COL_KD), kv_col(COL_VD),
            whole(masks), whole(edges),
        ],
        out_specs=[out_tile, out_tile],
        scratch_shapes=[pltpu.VMEM((ATTN_PAIRS, NA_CLASSES, NA_MAX_KEYS, 2 * Q_BLK), jnp.float32),
                        pltpu.VMEM((ATTN_PAIRS, seq // LANES, LANES, LANES), jnp.bfloat16),
                        pltpu.VMEM((seq // LANES, LANES, LANES), jnp.bfloat16)],
        compiler_params=pltpu.CompilerParams(
            dimension_semantics=("arbitrary", "arbitrary"), vmem_limit_bytes=_attn_vmem_limit(vmem)),
        name="attention",
    )(rpb_flat, sink, proj, proj, proj, proj, proj, proj, jnp.asarray(masks), jnp.asarray(edges))


def _out_ffn_kernel(x_ref, oa_ref, ob_ref, gna_ref, gswa_ref, wo_ref, gffn_ref, wg_ref, wu_ref, wd_ref,
                    gfin_ref, o_ref, *, ff_chunk, sub_rows):
    d_ff = wg_ref.shape[1]
    row_slices = [slice(r0, r0 + sub_rows) for r0 in range(0, x_ref.shape[0], sub_rows)]
    mixed = []
    for rows in row_slices:
        na = _rms(oa_ref[rows, :].astype(jnp.float32), gna_ref[...]).astype(jnp.bfloat16)
        nb = _rms(ob_ref[rows, :].astype(jnp.float32), gswa_ref[...]).astype(jnp.bfloat16)
        x1 = (x_ref[rows, :]
              + jnp.dot(na, wo_ref[:A_WIDTH, :], preferred_element_type=jnp.float32)
              + jnp.dot(nb, wo_ref[A_WIDTH:, :], preferred_element_type=jnp.float32))
        mixed.append((x1, _rms(x1, gffn_ref[...]).astype(jnp.bfloat16)))
    for rows, (x1, h) in zip(row_slices, mixed):
        acc = jnp.zeros(x1.shape, jnp.float32)
        for c0 in range(0, d_ff, ff_chunk):
            gate = jnp.dot(h, wg_ref[:, c0:c0 + ff_chunk], preferred_element_type=jnp.float32)
            up = jnp.dot(h, wu_ref[:, c0:c0 + ff_chunk], preferred_element_type=jnp.float32)
            act = (jax.nn.silu(gate) * up).astype(jnp.bfloat16)
            acc = acc + jnp.dot(act, wd_ref[c0:c0 + ff_chunk, :], preferred_element_type=jnp.float32)
        o_ref[rows, :] = _rms(x1 + acc, gfin_ref[...])


def _out_ffn(x2d, o_na, o_swa, g_na, g_swa, wo, g_ffn, wg, wu, wd, g_fin):
    n_tok = x2d.shape[0]
    tm, sub_rows, ff_chunk = FFN_ROWS, FFN_SUB_ROWS, FFN_CHUNK
    d_ff = wg.shape[1]
    const = lambda i: (0, 0)
    row = lambda i: (i, 0)
    resident = lambda shape: pl.BlockSpec(shape, const, pipeline_mode=pl.Buffered(1))
    weights = (D_MODEL * D_MODEL + 3 * D_MODEL * d_ff) * 2
    tiles = 2 * 2 * tm * D_MODEL * 4 + 2 * 2 * tm * A_WIDTH * 2
    vmem = weights + tiles + 2 * (6 * sub_rows * D_MODEL * 4 + 4 * sub_rows * ff_chunk * 4)
    return pl.pallas_call(
        functools.partial(_out_ffn_kernel, ff_chunk=ff_chunk, sub_rows=sub_rows),
        out_shape=jax.ShapeDtypeStruct((n_tok, D_MODEL), jnp.float32),
        grid=(n_tok // tm,),
        in_specs=[
            pl.BlockSpec((tm, D_MODEL), row),
            pl.BlockSpec((tm, A_WIDTH), row),
            pl.BlockSpec((tm, B_WIDTH), row),
            pl.BlockSpec((1, A_WIDTH), const),
            pl.BlockSpec((1, B_WIDTH), const),
            resident((D_MODEL, D_MODEL)),
            pl.BlockSpec((1, D_MODEL), const),
            resident((D_MODEL, d_ff)),
            resident((D_MODEL, d_ff)),
            resident((d_ff, D_MODEL)),
            pl.BlockSpec((1, D_MODEL), const),
        ],
        out_specs=pl.BlockSpec((tm, D_MODEL), row),
        compiler_params=pltpu.CompilerParams(
            dimension_semantics=("arbitrary",), vmem_limit_bytes=_vmem_limit(vmem + (8 << 20))),
        name="out_ffn",
    )(x2d, o_na, o_swa, g_na, g_swa, wo, g_ffn, wg, wu, wd, g_fin)


def kernel(x, g_norm_mix, w_in, na_rpb, swa_sink, g_out_na, g_out_swa, w_out, g_norm_ffn, w_gate, w_up, w_down,
           g_final):
    batch, seq, d_model = x.shape
    assert w_in.shape[0] == 1, "the final RMSNorm is fused into the single layer's FFN call"
    assert d_model == D_MODEL and seq % Q_BLK == 0 and seq // GRID_W >= NA_MAX_KEY_ROWS
    assert B_KV_WIDTH == LANES, "the windowed k (and v) heads together fill one lane tile"
    rope = _rope_tables(seq)
    x2d = x.reshape(batch * seq, d_model)
    proj, (wo, wg, wu, wd) = _in_proj(x2d, g_norm_mix[0][None, :], w_in[0], rope, seq,
                                      (w_out[0], w_gate[0], w_up[0], w_down[0]))
    o_na, o_swa = _attention(proj, na_rpb[0].reshape(-1), swa_sink[0], seq, batch)
    out = _out_ffn(x2d, o_na, o_swa, g_out_na[0][None, :], g_out_swa[0][None, :], wo,
                   g_norm_ffn[0][None, :], wg, wu, wd, g_final[None, :])
    return out.reshape(batch, seq, d_model)
```

```python
import functools
from typing import NamedTuple

import numpy as np
import jax
import jax.numpy as jnp
from jax import lax
from jax.experimental import pallas as pl
from jax.experimental.pallas import tpu as pltpu

D_MODEL = 1024
HEAD_DIM = 64
NA_HEADS = 8
SWA_HEADS = 8
SWA_KV_HEADS = 2
A_WIDTH = NA_HEADS * HEAD_DIM
B_WIDTH = SWA_HEADS * HEAD_DIM
B_KV_WIDTH = SWA_KV_HEADS * HEAD_DIM
IN_WIDTH = 3 * A_WIDTH + B_WIDTH + 2 * B_KV_WIDTH
GRID_W = 64
NA_ROWS = 8
NA_COLS = 16
SWA_WINDOW = 128
ROPE_THETA = 500000.0
ROPE_DIM = HEAD_DIM // 4
RMS_EPS = 1e-6
NEG_INF = -1e30
LOG2E = 1.4426950408889634
Q_SCALE = HEAD_DIM ** -0.5 * LOG2E

LANES = 128
VMEM_CAPACITY = 64 * 1024 * 1024

PROJ_WIDTH = 3 * A_WIDTH + B_WIDTH + 2 * SWA_KV_HEADS * LANES
COL_QA, COL_KA, COL_VA = 0, A_WIDTH, 2 * A_WIDTH
COL_QB = 3 * A_WIDTH
COL_KD = COL_QB + B_WIDTH
COL_VD = COL_KD + SWA_KV_HEADS * LANES
W_COL_KB = COL_QB + B_WIDTH

Q_BLK = 128
NA_Q_ROWS = Q_BLK // GRID_W
NA_MAX_KEY_ROWS = NA_ROWS + NA_Q_ROWS - 1
NA_MAX_KEYS = NA_MAX_KEY_ROWS * GRID_W
NA_CLASSES = 5
ATTN_SKEW = 6
ATTN_PAIRS = 4
PAIRS_PER_KV = SWA_HEADS // (2 * SWA_KV_HEADS)
IN_PROJ_ROWS = 1024
FFN_ROWS = 1024
FFN_SUB_ROWS = 512
FFN_CHUNK = 256


def _vmem_limit(nbytes):
    return int(min(VMEM_CAPACITY - (2 << 20), max(nbytes, 16 << 20)))


def _attn_vmem_limit(nbytes):
    assert nbytes + (16 << 20) <= VMEM_CAPACITY - (2 << 20)
    return VMEM_CAPACITY - (2 << 20)


def _na_class_of_block(i, nblk):
    if i < 2:
        return i
    if i >= nblk - 2:
        return NA_CLASSES - (nblk - i)
    return 2


def _na_key_rows(i, rows):
    first = NA_Q_ROWS * i
    lo = int(np.clip(first - NA_ROWS // 2, 0, rows - NA_ROWS))
    hi = int(np.clip(first + NA_Q_ROWS - 1 - NA_ROWS // 2, 0, rows - NA_ROWS)) + NA_ROWS
    return lo, hi


def _na_static_tables(seq):
    rows = seq // GRID_W
    nblk = seq // Q_BLK
    masks = np.full((NA_CLASSES, NA_MAX_KEYS, Q_BLK), NEG_INF, np.float32)
    ridx = np.full((NA_CLASSES, NA_Q_ROWS, NA_MAX_KEY_ROWS), -1, np.int64)
    seen = {}
    for i in range(nblk):
        cls = _na_class_of_block(i, nblk)
        lo, hi = _na_key_rows(i, rows)
        assert (lo * GRID_W) % LANES == 0 and hi - lo <= NA_MAX_KEY_ROWS
        q = np.arange(Q_BLK) + i * Q_BLK
        qr, qc = q // GRID_W, q % GRID_W
        k = np.arange((hi - lo) * GRID_W) + lo * GRID_W
        kr, kc = k // GRID_W, k % GRID_W
        rs = np.clip(qr - NA_ROWS // 2, 0, rows - NA_ROWS)
        cs = np.clip(qc - NA_COLS // 2, 0, GRID_W - NA_COLS)
        ok_r = (kr[:, None] >= rs[None, :]) & (kr[:, None] < rs[None, :] + NA_ROWS)
        ok_c = (kc[:, None] >= cs[None, :]) & (kc[:, None] < cs[None, :] + NA_COLS)
        m = np.full((NA_MAX_KEYS, Q_BLK), NEG_INF, np.float32)
        m[:k.size] = np.where(ok_r & ok_c, 0.0, NEG_INF)
        rr = np.full((NA_Q_ROWS, NA_MAX_KEY_ROWS), -1, np.int64)
        for a in range(NA_Q_ROWS):
            for b in range(hi - lo):
                r, krow = NA_Q_ROWS * i + a, lo + b
                if rs[a * GRID_W] <= krow < rs[a * GRID_W] + NA_ROWS:
                    rr[a, b] = krow - r + NA_ROWS - 1
        if cls in seen:
            assert np.array_equal(seen[cls][0], m) and np.array_equal(seen[cls][1], rr)
        seen[cls] = (m, rr)
        masks[cls], ridx[cls] = m, rr
    return masks, ridx


def _swa_edge_masks():
    assert SWA_WINDOW == Q_BLK
    key = np.arange(Q_BLK)[:, None]
    qry = np.arange(Q_BLK)[None, :]
    tri = np.stack([np.where(key >= qry, 0.0, NEG_INF), np.where(key <= qry, 0.0, NEG_INF)]).astype(np.float32)
    return np.ascontiguousarray(np.concatenate([tri, tri], axis=2))


def _rope_tables(seq):
    half = ROPE_DIM // 2
    inv_freq = np.float32(ROPE_THETA) ** (-np.arange(0, ROPE_DIM, 2, dtype=np.float32) / np.float32(ROPE_DIM))
    ang = np.arange(seq, dtype=np.float32)[:, None] * inv_freq[None, :].astype(np.float32)
    cos, sin = np.cos(ang).astype(np.float32), np.sin(ang).astype(np.float32)
    ones = np.ones((seq, HEAD_DIM - ROPE_DIM), np.float32)
    zeros = np.zeros((seq, HEAD_DIM - ROPE_DIM), np.float32)
    zh = np.zeros((seq, half), np.float32)
    c = np.concatenate([cos, cos, ones], axis=1)
    sa = np.concatenate([-sin, zh, zeros], axis=1)
    sb = np.concatenate([zh, sin, zeros], axis=1)
    rep = LANES // HEAD_DIM
    return tuple(jnp.asarray(np.tile(t, (1, rep))) for t in (c, sa, sb))


def _rms(x, g):
    return x * lax.rsqrt(jnp.mean(x * x, axis=-1, keepdims=True) + RMS_EPS) * g


def _rope_tile(a, c, sa, sb):
    return a * c + pltpu.roll(a, LANES - ROPE_DIM // 2, axis=1) * sa + pltpu.roll(a, ROPE_DIM // 2, axis=1) * sb


def _in_proj_kernel(x_ref, g_ref, w_ref, c_ref, sa_ref, sb_ref, *rest):
    n_extra = (len(rest) - 2) // 2
    extra_in, o_ref, extra_out, w_scr = rest[:n_extra], rest[n_extra], rest[n_extra + 1:-1], rest[-1]

    @pl.when(pl.program_id(0) == 0)
    def _():
        for c0 in range(0, w_ref.shape[1], LANES):
            w_scr[:, c0:c0 + LANES] = w_ref[:, c0:c0 + LANES].astype(w_scr.dtype)

    for src, dst in zip(extra_in, extra_out):
        dst[...] = src[...].astype(dst.dtype)

    h = _rms(x_ref[...], g_ref[...]).astype(jnp.bfloat16)
    c, sa, sb = c_ref[...], sa_ref[...], sb_ref[...]

    def proj(col, width):
        return jnp.dot(h, w_scr[:, col:col + width], preferred_element_type=jnp.float32)

    kv = proj(W_COL_KB, 2 * B_KV_WIDTH)
    left = lax.broadcasted_iota(jnp.int32, (h.shape[0], LANES), 1) < HEAD_DIM
    for col, tile in ((COL_KD, _rope_tile(kv[:, :B_KV_WIDTH], c, sa, sb)), (COL_VD, kv[:, B_KV_WIDTH:])):
        swapped = pltpu.roll(tile, HEAD_DIM, axis=1)
        o_ref[:, col:col + LANES] = jnp.where(left, tile, swapped).astype(o_ref.dtype)
        o_ref[:, col + LANES:col + 2 * LANES] = jnp.where(left, swapped, tile).astype(o_ref.dtype)
    qb = proj(COL_QB, B_WIDTH)
    for t in range(B_WIDTH // LANES):
        lo = t * LANES
        tile = _rope_tile(qb[:, lo:lo + LANES], c, sa, sb) * Q_SCALE
        o_ref[:, COL_QB + lo:COL_QB + lo + LANES] = tile.astype(o_ref.dtype)
    o_ref[:, COL_QA:COL_QA + A_WIDTH] = (proj(COL_QA, A_WIDTH) * Q_SCALE).astype(o_ref.dtype)
    o_ref[:, COL_KA:COL_KA + A_WIDTH] = proj(COL_KA, A_WIDTH).astype(o_ref.dtype)
    o_ref[:, COL_VA:COL_VA + A_WIDTH] = proj(COL_VA, A_WIDTH).astype(o_ref.dtype)


def _in_proj(x2d, g, w_in, rope, seq, later_weights):
    n_tok = x2d.shape[0]
    tm = IN_PROJ_ROWS
    steps = n_tok // tm
    pos_blocks = seq // tm
    const = lambda i: (0, 0)
    row = lambda i: (i, 0)
    pos = lambda i: (i % pos_blocks, 0)
    slices = [(w.shape[0] // steps, w.shape[1]) for w in later_weights]
    assert all(w.shape[0] % steps == 0 and rows % 16 == 0 for w, (rows, _) in zip(later_weights, slices))
    slice_bytes = sum(rows * cols for rows, cols in slices)
    vmem = (2 * tm * D_MODEL * 4 + D_MODEL * IN_WIDTH * (4 + 2) + 2 * tm * PROJ_WIDTH * 2 + 3 * tm * B_WIDTH * 4 * 2
            + 2 * slice_bytes * (4 + 2) + 3 * 2 * tm * LANES * 4)
    outs = pl.pallas_call(
        _in_proj_kernel,
        out_shape=[jax.ShapeDtypeStruct((n_tok, PROJ_WIDTH), jnp.bfloat16)]
        + [jax.ShapeDtypeStruct(w.shape, jnp.bfloat16) for w in later_weights],
        grid=(steps,),
        in_specs=[
            pl.BlockSpec((tm, D_MODEL), row),
            pl.BlockSpec((1, D_MODEL), const),
            pl.BlockSpec((D_MODEL, IN_WIDTH), const, pipeline_mode=pl.Buffered(1)),
            pl.BlockSpec((tm, LANES), pos),
            pl.BlockSpec((tm, LANES), pos),
            pl.BlockSpec((tm, LANES), pos),
        ] + [pl.BlockSpec(blk, row) for blk in slices],
        out_specs=[pl.BlockSpec((tm, PROJ_WIDTH), row)] + [pl.BlockSpec(blk, row) for blk in slices],
        scratch_shapes=[pltpu.VMEM((D_MODEL, IN_WIDTH), jnp.bfloat16)],
        compiler_params=pltpu.CompilerParams(
            dimension_semantics=("arbitrary",), vmem_limit_bytes=_vmem_limit(vmem + (8 << 20))),
        name="in_proj",
    )(x2d, g, w_in, *rope, *later_weights)
    return outs[0], outs[1:]


def _transpose_keys(v_ref, vt_ref, ones_below=False):
    top = lax.broadcasted_iota(jnp.int32, (LANES, LANES), 0) < HEAD_DIM
    for j in range(v_ref.shape[0] // LANES):
        tile = v_ref[j * LANES:(j + 1) * LANES, :].T
        vt_ref[j] = jnp.where(top, tile, jnp.ones_like(tile)) if ones_below else tile


class _Stream(NamedTuple):
    q_ref: object
    k_ref: object
    vt_ref: object
    o_ref: object
    sink_row: object = None
    sums_below: bool = False


class _Block(NamedTuple):
    stream: _Stream
    q0: int
    k0: int
    n_keys: int
    add_bias: object


def _attend(blocks, skew):
    lane = lax.broadcasted_iota(jnp.int32, (Q_BLK, LANES), 1)
    left = lane < HEAD_DIM
    top = lax.broadcasted_iota(jnp.int32, (LANES, Q_BLK), 0) < HEAD_DIM

    def scores_of(blk):
        q2 = blk.stream.q_ref[blk.q0:blk.q0 + Q_BLK, :]
        zero = jnp.zeros_like(q2)
        qq = jnp.concatenate([jnp.where(left, q2, zero), jnp.where(left, zero, q2)], axis=0)
        st = lax.dot_general(blk.stream.k_ref[blk.k0:blk.k0 + blk.n_keys, :], qq, (((1,), (1,)), ((), ())),
                             preferred_element_type=jnp.float32)
        return blk.add_bias(st)

    def finish(blk, st):
        _, _, vt_ref, o_ref, sink_row, sums_below = blk.stream
        m = jnp.max(st, axis=0, keepdims=True)
        if sink_row is not None:
            m = jnp.maximum(m, sink_row)
        p = jnp.exp2(st - m)
        tile0, full, part = blk.k0 // LANES, blk.n_keys // LANES, blk.n_keys % LANES
        pieces = [vt_ref[tile0 + j] for j in range(full)]
        if part:
            pieces.append(vt_ref[tile0 + full][:, :part])
        ot = jnp.dot(jnp.concatenate(pieces, axis=1), p.astype(jnp.bfloat16), preferred_element_type=jnp.float32)
        den = ot[HEAD_DIM:HEAD_DIM + 1, :] if sums_below else jnp.sum(p, axis=0, keepdims=True)
        if sink_row is not None:
            den = den + jnp.exp2(sink_row - m)
        if sums_below:
            ot = ot[:HEAD_DIM, :] / den
            x = jnp.concatenate([ot[:, :Q_BLK], ot[:, Q_BLK:]], axis=0)
        else:
            ot = ot / den
            x = jnp.where(top, ot[:, :Q_BLK], ot[:, Q_BLK:])
        o_ref[blk.q0:blk.q0 + Q_BLK, :] = x.T.astype(o_ref.dtype)

    pending = [scores_of(blk) for blk in blocks[:skew]]
    for c, blk in enumerate(blocks):
        st = pending.pop(0)
        if c + skew < len(blocks):
            pending.append(scores_of(blocks[c + skew]))
        finish(blk, st)


def _build_na_bias(rpb_ref, mask_ref, t_scr, hp, ridx):
    n_rrow, n_rcol = 2 * NA_ROWS - 1, 2 * NA_COLS - 1
    sublanes = 8
    lane8 = lax.broadcasted_iota(jnp.int32, (sublanes, LANES), 1)
    second_row = lax.broadcasted_iota(jnp.int32, (GRID_W, LANES), 1) >= GRID_W
    used_rows = sorted({int(r) for r in ridx.reshape(-1) if r >= 0})
    zero = jnp.zeros((GRID_W, LANES), jnp.float32)
    for hh in range(2):
        h = 2 * hp + hh
        toeplitz = {-1: zero}
        for rho in used_rows:
            band = jnp.zeros((sublanes, LANES), jnp.float32)
            for d in range(n_rcol):
                band = jnp.where(lane8 == (NA_COLS - 1 - d) % LANES, rpb_ref[(h * n_rrow + rho) * n_rcol + d] * LOG2E, band)
            rows = jnp.concatenate([band] * (GRID_W // sublanes), axis=0)
            first = pltpu.roll(rows, 0, axis=1, stride=1, stride_axis=0)
            second = pltpu.roll(rows, GRID_W, axis=1, stride=1, stride_axis=0)
            toeplitz[rho] = jnp.where(second_row, second, first)
        for cls in range(NA_CLASSES):
            for b in range(NA_MAX_KEY_ROWS):
                blk = jnp.where(second_row, toeplitz[int(ridx[cls, 1, b])], toeplitz[int(ridx[cls, 0, b])])
                ks = slice(b * GRID_W, (b + 1) * GRID_W)
                t_scr[cls, ks, hh * Q_BLK:(hh + 1) * Q_BLK] = blk + mask_ref[cls, ks, :]


def _swa_add_mask(st, edge_ref, first_tile, n):
    parts = []
    for j in range(st.shape[0] // Q_BLK):
        part = st[j * Q_BLK:(j + 1) * Q_BLK, :]
        if first_tile + j != n:
            part = part + edge_ref[0 if first_tile + j < n else 1]
        parts.append(part)
    return jnp.concatenate(parts, axis=0)


def _attn_kernel(rpb_ref, sink_ref, qa_ref, ka_ref, va_ref, qb_ref, kd_ref, vd_ref, mask_ref, edge_ref,
                 oa_ref, ob_ref, t_scr, vta_scr, vtb_scr, *, ridx, seq):
    nblk = seq // Q_BLK
    pairs = [ATTN_PAIRS * pl.program_id(0) + j for j in range(ATTN_PAIRS)]
    lanes = [slice(j * LANES, (j + 1) * LANES) for j in range(ATTN_PAIRS)]

    @pl.when(pl.program_id(1) == 0)
    def _():
        for j, pair in enumerate(pairs):
            _build_na_bias(rpb_ref, mask_ref, t_scr.at[j], pair, ridx)

    for j in range(ATTN_PAIRS):
        _transpose_keys(va_ref.at[:, lanes[j]], vta_scr.at[j])
    for g in range(ATTN_PAIRS // PAIRS_PER_KV):
        _transpose_keys(vd_ref.at[:, lanes[g]], vtb_scr.at[g], ones_below=True)
    lane = lax.broadcasted_iota(jnp.int32, (1, 2 * Q_BLK), 1)
    streams = []
    for j, pair in enumerate(pairs):
        g = j // PAIRS_PER_KV
        sink_row = jnp.where(lane < Q_BLK, sink_ref[2 * pair], sink_ref[2 * pair + 1]) * LOG2E
        streams.append((_Stream(qa_ref.at[:, lanes[j]], ka_ref.at[:, lanes[j]], vta_scr.at[j], oa_ref.at[:, lanes[j]]),
                        _Stream(qb_ref.at[:, lanes[j]], kd_ref.at[:, lanes[g]], vtb_scr.at[g], ob_ref.at[:, lanes[j]],
                                sink_row, sums_below=True)))
    blocks = []
    for i in range(nblk):
        lo, hi = _na_key_rows(i, seq // GRID_W)
        n_keys = (hi - lo) * GRID_W
        first, last = max(i - 1, 0), min(i + 1, nblk - 1)
        add_mask = functools.partial(_swa_add_mask, edge_ref=edge_ref, first_tile=first, n=i)
        for j, (na, swa) in enumerate(streams):
            add_bias = functools.partial(lambda st, j, cls, n: st + t_scr[j, cls, :n, :],
                                         j=j, cls=_na_class_of_block(i, nblk), n=n_keys)
            blocks.append(_Block(na, i * Q_BLK, lo * GRID_W, n_keys, add_bias))
            blocks.append(_Block(swa, i * Q_BLK, first * Q_BLK, (last - first + 1) * Q_BLK, add_mask))
    _attend(blocks, ATTN_SKEW)


def _attention(proj, rpb_flat, sink, seq, batch):
    masks, ridx = _na_static_tables(seq)
    edges = _swa_edge_masks()
    n_tok = proj.shape[0]
    n_steps = A_WIDTH // (ATTN_PAIRS * LANES)
    assert B_WIDTH == A_WIDTH and ATTN_PAIRS % PAIRS_PER_KV == 0
    kv_per_step = ATTN_PAIRS // PAIRS_PER_KV
    width, kv_width = ATTN_PAIRS * LANES, kv_per_step * LANES
    col = lambda base: pl.BlockSpec((seq, width), lambda s, b: (b, base // width + s))
    kv_col = lambda base: pl.BlockSpec((seq, kv_width), lambda s, b: (b, base // kv_width + s))
    whole = lambda a: pl.BlockSpec(a.shape, lambda s, b: (0,) * a.ndim)
    table_bytes = ATTN_PAIRS * NA_CLASSES * NA_MAX_KEYS * 2 * Q_BLK * 4
    vmem = (2 * (6 * ATTN_PAIRS + 2 * kv_per_step) * seq * LANES * 2 + 2 * (masks.size + edges.size) * 4 + table_bytes
            + (ATTN_PAIRS + kv_per_step) * seq * LANES * 2)
    out = jax.ShapeDtypeStruct((n_tok, A_WIDTH), jnp.bfloat16)
    out_tile = pl.BlockSpec((seq, width), lambda s, b: (b, s))
    return pl.pallas_call(
        functools.partial(_attn_kernel, ridx=ridx, seq=seq),
        out_shape=[out, out],
        grid=(n_steps, batch),
        in_specs=[
            pl.BlockSpec(memory_space=pltpu.SMEM),
            pl.BlockSpec(memory_space=pltpu.SMEM),
            col(COL_QA), col(COL_KA), col(COL_VA), col(COL_QB), kv_col(COL_KD), kv_col(COL_VD),
            whole(masks), whole(edges),
        ],
        out_specs=[out_tile, out_tile],
        scratch_shapes=[pltpu.VMEM((ATTN_PAIRS, NA_CLASSES, NA_MAX_KEYS, 2 * Q_BLK), jnp.float32),
                        pltpu.VMEM((ATTN_PAIRS, seq // LANES, LANES, LANES), jnp.bfloat16),
                        pltpu.VMEM((kv_per_step, seq // LANES, LANES, LANES), jnp.bfloat16)],
        compiler_params=pltpu.CompilerParams(
            dimension_semantics=("arbitrary", "arbitrary"), vmem_limit_bytes=_attn_vmem_limit(vmem)),
        name="attention",
    )(rpb_flat, sink, proj, proj, proj, proj, proj, proj, jnp.asarray(masks), jnp.asarray(edges))


def _out_ffn_kernel(x_ref, oa_ref, ob_ref, gna_ref, gswa_ref, wo_ref, gffn_ref, wg_ref, wu_ref, wd_ref,
                    gfin_ref, o_ref, *, ff_chunk, sub_rows):
    d_ff = wg_ref.shape[1]
    row_slices = [slice(r0, r0 + sub_rows) for r0 in range(0, x_ref.shape[0], sub_rows)]
    mixed = []
    for rows in row_slices:
        na = _rms(oa_ref[rows, :].astype(jnp.float32), gna_ref[...]).astype(jnp.bfloat16)
        nb = _rms(ob_ref[rows, :].astype(jnp.float32), gswa_ref[...]).astype(jnp.bfloat16)
        x1 = (x_ref[rows, :]
              + jnp.dot(na, wo_ref[:A_WIDTH, :], preferred_element_type=jnp.float32)
              + jnp.dot(nb, wo_ref[A_WIDTH:, :], preferred_element_type=jnp.float32))
        mixed.append((x1, _rms(x1, gffn_ref[...]).astype(jnp.bfloat16)))
    for rows, (x1, h) in zip(row_slices, mixed):
        acc = jnp.zeros(x1.shape, jnp.float32)
        for c0 in range(0, d_ff, ff_chunk):
            gate = jnp.dot(h, wg_ref[:, c0:c0 + ff_chunk], preferred_element_type=jnp.float32)
            up = jnp.dot(h, wu_ref[:, c0:c0 + ff_chunk], preferred_element_type=jnp.float32)
            act = (jax.nn.silu(gate) * up).astype(jnp.bfloat16)
            acc = acc + jnp.dot(act, wd_ref[c0:c0 + ff_chunk, :], preferred_element_type=jnp.float32)
        o_ref[rows, :] = _rms(x1 + acc, gfin_ref[...])


def _out_ffn(x2d, o_na, o_swa, g_na, g_swa, wo, g_ffn, wg, wu, wd, g_fin):
    n_tok = x2d.shape[0]
    tm, sub_rows, ff_chunk = FFN_ROWS, FFN_SUB_ROWS, FFN_CHUNK
    d_ff = wg.shape[1]
    const = lambda i: (0, 0)
    row = lambda i: (i, 0)
    resident = lambda shape: pl.BlockSpec(shape, const, pipeline_mode=pl.Buffered(1))
    weights = (D_MODEL * D_MODEL + 3 * D_MODEL * d_ff) * 2
    tiles = 2 * 2 * tm * D_MODEL * 4 + 2 * 2 * tm * A_WIDTH * 2
    vmem = weights + tiles + 2 * (6 * sub_rows * D_MODEL * 4 + 4 * sub_rows * ff_chunk * 4)
    return pl.pallas_call(
        functools.partial(_out_ffn_kernel, ff_chunk=ff_chunk, sub_rows=sub_rows),
        out_shape=jax.ShapeDtypeStruct((n_tok, D_MODEL), jnp.float32),
        grid=(n_tok // tm,),
        in_specs=[
            pl.BlockSpec((tm, D_MODEL), row),
            pl.BlockSpec((tm, A_WIDTH), row),
            pl.BlockSpec((tm, B_WIDTH), row),
            pl.BlockSpec((1, A_WIDTH), const),
            pl.BlockSpec((1, B_WIDTH), const),
            resident((D_MODEL, D_MODEL)),
            pl.BlockSpec((1, D_MODEL), const),
            resident((D_MODEL, d_ff)),
            resident((D_MODEL, d_ff)),
            resident((d_ff, D_MODEL)),
            pl.BlockSpec((1, D_MODEL), const),
        ],
        out_specs=pl.BlockSpec((tm, D_MODEL), row),
        compiler_params=pltpu.CompilerParams(
            dimension_semantics=("arbitrary",), vmem_limit_bytes=_vmem_limit(vmem + (8 << 20))),
        name="out_ffn",
    )(x2d, o_na, o_swa, g_na, g_swa, wo, g_ffn, wg, wu, wd, g_fin)


def kernel(x, g_norm_mix, w_in, na_rpb, swa_sink, g_out_na, g_out_swa, w_out, g_norm_ffn, w_gate, w_up, w_down,
           g_final):
    batch, seq, d_model = x.shape
    assert w_in.shape[0] == 1, "the final RMSNorm is fused into the single layer's FFN call"
    assert d_model == D_MODEL and seq % Q_BLK == 0 and seq // GRID_W >= NA_MAX_KEY_ROWS
    assert B_KV_WIDTH == LANES, "the windowed k (and v) heads together fill one lane tile"
    rope = _rope_tables(seq)
    x2d = x.reshape(batch * seq, d_model)
    proj, (wo, wg, wu, wd) = _in_proj(x2d, g_norm_mix[0][None, :], w_in[0], rope, seq,
                                      (w_out[0], w_gate[0], w_up[0], w_down[0]))
    o_na, o_swa = _attention(proj, na_rpb[0].reshape(-1), swa_sink[0], seq, batch)
    out = _out_ffn(x2d, o_na, o_swa, g_out_na[0][None, :], g_out_swa[0][None, :], wo,
                   g_norm_ffn[0][None, :], wg, wu, wd, g_final[None, :])
    return out.reshape(batch, seq, d_model)
```

```python
import functools
from typing import NamedTuple

import numpy as np
import jax
import jax.numpy as jnp
from jax import lax
from jax.experimental import pallas as pl
from jax.experimental.pallas import tpu as pltpu

D_MODEL = 1024
HEAD_DIM = 64
NA_HEADS = 8
SWA_HEADS = 8
SWA_KV_HEADS = 2
A_WIDTH = NA_HEADS * HEAD_DIM
B_WIDTH = SWA_HEADS * HEAD_DIM
B_KV_WIDTH = SWA_KV_HEADS * HEAD_DIM
IN_WIDTH = 3 * A_WIDTH + B_WIDTH + 2 * B_KV_WIDTH
GRID_W = 64
NA_ROWS = 8
NA_COLS = 16
SWA_WINDOW = 128
ROPE_THETA = 500000.0
ROPE_DIM = HEAD_DIM // 4
RMS_EPS = 1e-6
NEG_INF = -1e30
LOG2E = 1.4426950408889634
Q_SCALE = HEAD_DIM ** -0.5 * LOG2E

LANES = 128
SUBLANES = 8
BF16_ROWS = 16
VMEM_CAPACITY = 64 * 1024 * 1024

PROJ_WIDTH = 3 * A_WIDTH + B_WIDTH + 2 * SWA_KV_HEADS * LANES
COL_QA, COL_KA, COL_VA = 0, A_WIDTH, 2 * A_WIDTH
COL_QB = 3 * A_WIDTH
COL_KD = COL_QB + B_WIDTH
COL_VD = COL_KD + SWA_KV_HEADS * LANES
W_COL_KB = COL_QB + B_WIDTH

Q_BLK = 128
NA_Q_ROWS = Q_BLK // GRID_W
NA_MAX_KEY_ROWS = NA_ROWS + NA_Q_ROWS - 1
NA_MAX_KEYS = NA_MAX_KEY_ROWS * GRID_W
NA_CLASSES = 5
ATTN_SKEW = 6
ATTN_PAIRS = 2
IN_PROJ_ROWS = 1024
FFN_ROWS = 1024
FFN_SUB_ROWS = 512
FFN_CHUNK = 256


def _vmem_limit(nbytes):
    return int(min(VMEM_CAPACITY - (2 << 20), max(nbytes, 16 << 20)))


def _attn_vmem_limit(nbytes):
    assert nbytes + (16 << 20) <= VMEM_CAPACITY - (2 << 20)
    return VMEM_CAPACITY - (2 << 20)


def _na_class_of_block(i, nblk):
    if i < 2:
        return i
    if i >= nblk - 2:
        return NA_CLASSES - (nblk - i)
    return 2


def _na_key_rows(i, rows):
    first = NA_Q_ROWS * i
    lo = int(np.clip(first - NA_ROWS // 2, 0, rows - NA_ROWS))
    hi = int(np.clip(first + NA_Q_ROWS - 1 - NA_ROWS // 2, 0, rows - NA_ROWS)) + NA_ROWS
    return lo, hi


def _na_static_tables(seq):
    rows = seq // GRID_W
    nblk = seq // Q_BLK
    masks = np.full((NA_CLASSES, NA_MAX_KEYS, Q_BLK), NEG_INF, np.float32)
    ridx = np.full((NA_CLASSES, NA_Q_ROWS, NA_MAX_KEY_ROWS), -1, np.int64)
    seen = {}
    for i in range(nblk):
        cls = _na_class_of_block(i, nblk)
        lo, hi = _na_key_rows(i, rows)
        assert (lo * GRID_W) % LANES == 0 and hi - lo <= NA_MAX_KEY_ROWS
        q = np.arange(Q_BLK) + i * Q_BLK
        qr, qc = q // GRID_W, q % GRID_W
        k = np.arange((hi - lo) * GRID_W) + lo * GRID_W
        kr, kc = k // GRID_W, k % GRID_W
        rs = np.clip(qr - NA_ROWS // 2, 0, rows - NA_ROWS)
        cs = np.clip(qc - NA_COLS // 2, 0, GRID_W - NA_COLS)
        ok_r = (kr[:, None] >= rs[None, :]) & (kr[:, None] < rs[None, :] + NA_ROWS)
        ok_c = (kc[:, None] >= cs[None, :]) & (kc[:, None] < cs[None, :] + NA_COLS)
        m = np.full((NA_MAX_KEYS, Q_BLK), NEG_INF, np.float32)
        m[:k.size] = np.where(ok_r & ok_c, 0.0, NEG_INF)
        rr = np.full((NA_Q_ROWS, NA_MAX_KEY_ROWS), -1, np.int64)
        for a in range(NA_Q_ROWS):
            for b in range(hi - lo):
                r, krow = NA_Q_ROWS * i + a, lo + b
                if rs[a * GRID_W] <= krow < rs[a * GRID_W] + NA_ROWS:
                    rr[a, b] = krow - r + NA_ROWS - 1
        if cls in seen:
            assert np.array_equal(seen[cls][0], m) and np.array_equal(seen[cls][1], rr)
        seen[cls] = (m, rr)
        masks[cls], ridx[cls] = m, rr
    return masks, ridx


def _swa_edge_masks():
    assert SWA_WINDOW == Q_BLK
    key = np.arange(Q_BLK)[:, None]
    qry = np.arange(Q_BLK)[None, :]
    tri = np.stack([np.where(key >= qry, 0.0, NEG_INF), np.where(key <= qry, 0.0, NEG_INF)]).astype(np.float32)
    return np.ascontiguousarray(np.concatenate([tri, tri], axis=2))


def _rope_tables(seq):
    half = ROPE_DIM // 2
    inv_freq = np.float32(ROPE_THETA) ** (-np.arange(0, ROPE_DIM, 2, dtype=np.float32) / np.float32(ROPE_DIM))
    ang = np.arange(seq, dtype=np.float32)[:, None] * inv_freq[None, :].astype(np.float32)
    cos, sin = np.cos(ang).astype(np.float32), np.sin(ang).astype(np.float32)
    ones = np.ones((seq, HEAD_DIM - ROPE_DIM), np.float32)
    zeros = np.zeros((seq, HEAD_DIM - ROPE_DIM), np.float32)
    zh = np.zeros((seq, half), np.float32)
    c = np.concatenate([cos, cos, ones], axis=1)
    sa = np.concatenate([-sin, zh, zeros], axis=1)
    sb = np.concatenate([zh, sin, zeros], axis=1)
    rep = LANES // HEAD_DIM
    return tuple(jnp.asarray(np.tile(t, (1, rep))) for t in (c, sa, sb))


def _rms(x, g):
    return x * lax.rsqrt(jnp.mean(x * x, axis=-1, keepdims=True) + RMS_EPS) * g


def _rope_tile(a, c, sa, sb):
    return a * c + pltpu.roll(a, LANES - ROPE_DIM // 2, axis=1) * sa + pltpu.roll(a, ROPE_DIM // 2, axis=1) * sb


def _in_proj_kernel(x_ref, g_ref, w_ref, c_ref, sa_ref, sb_ref, *rest):
    n_extra = (len(rest) - 2) // 2
    extra_in, o_ref, extra_out, w_scr = rest[:n_extra], rest[n_extra], rest[n_extra + 1:-1], rest[-1]

    @pl.when(pl.program_id(0) == 0)
    def _():
        for c0 in range(0, w_ref.shape[1], LANES):
            w_scr[:, c0:c0 + LANES] = w_ref[:, c0:c0 + LANES].astype(w_scr.dtype)

    for src, dst in zip(extra_in, extra_out):
        dst[...] = src[...].astype(dst.dtype)

    h = _rms(x_ref[...], g_ref[...]).astype(jnp.bfloat16)
    tm, seq = x_ref.shape[0], c_ref.shape[0]
    pos = pl.ds(pl.multiple_of((pl.program_id(0) % (seq // tm)) * tm, tm), tm)
    c, sa, sb = c_ref[pos, :], sa_ref[pos, :], sb_ref[pos, :]

    def proj(col, width):
        return jnp.dot(h, w_scr[:, col:col + width], preferred_element_type=jnp.float32)

    kv = proj(W_COL_KB, 2 * B_KV_WIDTH)
    left = lax.broadcasted_iota(jnp.int32, (h.shape[0], LANES), 1) < HEAD_DIM
    for col, tile in ((COL_KD, _rope_tile(kv[:, :B_KV_WIDTH], c, sa, sb)), (COL_VD, kv[:, B_KV_WIDTH:])):
        swapped = pltpu.roll(tile, HEAD_DIM, axis=1)
        o_ref[:, col:col + LANES] = jnp.where(left, tile, swapped).astype(o_ref.dtype)
        o_ref[:, col + LANES:col + 2 * LANES] = jnp.where(left, swapped, tile).astype(o_ref.dtype)
    qb = proj(COL_QB, B_WIDTH)
    for t in range(B_WIDTH // LANES):
        lo = t * LANES
        tile = _rope_tile(qb[:, lo:lo + LANES], c, sa, sb) * Q_SCALE
        o_ref[:, COL_QB + lo:COL_QB + lo + LANES] = tile.astype(o_ref.dtype)
    o_ref[:, COL_QA:COL_QA + A_WIDTH] = (proj(COL_QA, A_WIDTH) * Q_SCALE).astype(o_ref.dtype)
    o_ref[:, COL_KA:COL_KA + A_WIDTH] = proj(COL_KA, A_WIDTH).astype(o_ref.dtype)
    o_ref[:, COL_VA:COL_VA + A_WIDTH] = proj(COL_VA, A_WIDTH).astype(o_ref.dtype)


def _in_proj(x2d, g, w_in, rope, seq, later_weights):
    n_tok = x2d.shape[0]
    tm = IN_PROJ_ROWS
    steps = n_tok // tm
    assert seq % tm == 0
    const = lambda i: (0, 0)
    row = lambda i: (i, 0)
    slices = [(w.shape[0] // steps, w.shape[1]) for w in later_weights]
    assert all(w.shape[0] % steps == 0 and rows % BF16_ROWS == 0 for w, (rows, _) in zip(later_weights, slices))
    slice_bytes = sum(rows * cols for rows, cols in slices)
    vmem = (2 * tm * D_MODEL * 4 + D_MODEL * IN_WIDTH * (4 + 2) + 2 * tm * PROJ_WIDTH * 2 + 3 * tm * B_WIDTH * 4 * 2
            + 2 * slice_bytes * (4 + 2) + 3 * 2 * seq * LANES * 4)
    outs = pl.pallas_call(
        _in_proj_kernel,
        out_shape=[jax.ShapeDtypeStruct((n_tok, PROJ_WIDTH), jnp.bfloat16)]
        + [jax.ShapeDtypeStruct(w.shape, jnp.bfloat16) for w in later_weights],
        grid=(steps,),
        in_specs=[
            pl.BlockSpec((tm, D_MODEL), row),
            pl.BlockSpec((1, D_MODEL), const),
            pl.BlockSpec((D_MODEL, IN_WIDTH), const, pipeline_mode=pl.Buffered(1)),
            pl.BlockSpec((seq, LANES), const),
            pl.BlockSpec((seq, LANES), const),
            pl.BlockSpec((seq, LANES), const),
        ] + [pl.BlockSpec(blk, row) for blk in slices],
        out_specs=[pl.BlockSpec((tm, PROJ_WIDTH), row)] + [pl.BlockSpec(blk, row) for blk in slices],
        scratch_shapes=[pltpu.VMEM((D_MODEL, IN_WIDTH), jnp.bfloat16)],
        compiler_params=pltpu.CompilerParams(
            dimension_semantics=("arbitrary",), vmem_limit_bytes=_vmem_limit(vmem + (8 << 20))),
        name="in_proj",
    )(x2d, g, w_in, *rope, *later_weights)
    return outs[0], outs[1:]


def _transpose_keys(v_ref, vt_ref, ones_below=False):
    top = lax.broadcasted_iota(jnp.int32, (LANES, LANES), 0) < HEAD_DIM
    for j in range(v_ref.shape[0] // LANES):
        tile = v_ref[j * LANES:(j + 1) * LANES, :].T
        vt_ref[j] = jnp.where(top, tile, jnp.ones_like(tile)) if ones_below else tile


class _Stream(NamedTuple):
    q_ref: object
    k_ref: object
    vt_ref: object
    o_ref: object
    sink_row: object = None
    sums_below: bool = False


class _Block(NamedTuple):
    stream: _Stream
    q0: int
    k0: int
    n_keys: int
    add_bias: object


def _attend(blocks, skew):
    lane = lax.broadcasted_iota(jnp.int32, (Q_BLK, LANES), 1)
    left = lane < HEAD_DIM
    top = lax.broadcasted_iota(jnp.int32, (LANES, Q_BLK), 0) < HEAD_DIM

    def scores_of(blk):
        q2 = blk.stream.q_ref[blk.q0:blk.q0 + Q_BLK, :]
        zero = jnp.zeros_like(q2)
        qq = jnp.concatenate([jnp.where(left, q2, zero), jnp.where(left, zero, q2)], axis=0)
        st = lax.dot_general(blk.stream.k_ref[blk.k0:blk.k0 + blk.n_keys, :], qq, (((1,), (1,)), ((), ())),
                             preferred_element_type=jnp.float32)
        return blk.add_bias(st)

    def finish(blk, st):
        _, _, vt_ref, o_ref, sink_row, sums_below = blk.stream
        m = jnp.max(st, axis=0, keepdims=True)
        if sink_row is not None:
            m = jnp.maximum(m, sink_row)
        p = jnp.exp2(st - m)
        tile0, full, part = blk.k0 // LANES, blk.n_keys // LANES, blk.n_keys % LANES
        pieces = [vt_ref[tile0 + j] for j in range(full)]
        if part:
            pieces.append(vt_ref[tile0 + full][:, :part])
        ot = jnp.dot(jnp.concatenate(pieces, axis=1), p.astype(jnp.bfloat16), preferred_element_type=jnp.float32)
        den = ot[HEAD_DIM:HEAD_DIM + 1, :] if sums_below else jnp.sum(p, axis=0, keepdims=True)
        if sink_row is not None:
            den = den + jnp.exp2(sink_row - m)
        if sums_below:
            ot = ot[:HEAD_DIM, :] / den
            x = jnp.concatenate([ot[:, :Q_BLK], ot[:, Q_BLK:]], axis=0)
        else:
            ot = ot / den
            x = jnp.where(top, ot[:, :Q_BLK], ot[:, Q_BLK:])
        o_ref[blk.q0:blk.q0 + Q_BLK, :] = x.T.astype(o_ref.dtype)

    pending = [scores_of(blk) for blk in blocks[:skew]]
    for c, blk in enumerate(blocks):
        st = pending.pop(0)
        if c + skew < len(blocks):
            pending.append(scores_of(blocks[c + skew]))
        finish(blk, st)


def _build_na_bias(rpb_ref, mask_ref, t_scr, hp, ridx):
    n_rrow, n_rcol = 2 * NA_ROWS - 1, 2 * NA_COLS - 1
    lane8 = lax.broadcasted_iota(jnp.int32, (SUBLANES, LANES), 1)
    second_row = lax.broadcasted_iota(jnp.int32, (GRID_W, LANES), 1) >= GRID_W
    used_rows = sorted({int(r) for r in ridx.reshape(-1) if r >= 0})
    zero = jnp.zeros((GRID_W, LANES), jnp.float32)
    for hh in range(2):
        h = 2 * hp + hh
        toeplitz = {-1: zero}
        for rho in used_rows:
            band = jnp.zeros((SUBLANES, LANES), jnp.float32)
            for d in range(n_rcol):
                band = jnp.where(lane8 == (NA_COLS - 1 - d) % LANES, rpb_ref[(h * n_rrow + rho) * n_rcol + d] * LOG2E, band)
            rows = jnp.concatenate([band] * (GRID_W // SUBLANES), axis=0)
            first = pltpu.roll(rows, 0, axis=1, stride=1, stride_axis=0)
            second = pltpu.roll(rows, GRID_W, axis=1, stride=1, stride_axis=0)
            toeplitz[rho] = jnp.where(second_row, second, first)
        for cls in range(NA_CLASSES):
            for b in range(NA_MAX_KEY_ROWS):
                blk = jnp.where(second_row, toeplitz[int(ridx[cls, 1, b])], toeplitz[int(ridx[cls, 0, b])])
                ks = slice(b * GRID_W, (b + 1) * GRID_W)
                t_scr[cls, ks, hh * Q_BLK:(hh + 1) * Q_BLK] = blk + mask_ref[cls, ks, :]


def _swa_add_mask(st, edge_ref, first_tile, n):
    parts = []
    for j in range(st.shape[0] // Q_BLK):
        part = st[j * Q_BLK:(j + 1) * Q_BLK, :]
        if first_tile + j != n:
            part = part + edge_ref[0 if first_tile + j < n else 1]
        parts.append(part)
    return jnp.concatenate(parts, axis=0)


def _attn_kernel(rpb_ref, sink_ref, qa_ref, ka_ref, va_ref, qb_ref, kd_ref, vd_ref, mask_ref, edge_ref,
                 oa_ref, ob_ref, t_scr, vta_scr, vtb_scr, *, ridx, seq):
    nblk = seq // Q_BLK
    pairs = [ATTN_PAIRS * pl.program_id(0) + j for j in range(ATTN_PAIRS)]
    lanes = [slice(j * LANES, (j + 1) * LANES) for j in range(ATTN_PAIRS)]

    @pl.when(pl.program_id(1) == 0)
    def _():
        for j, pair in enumerate(pairs):
            _build_na_bias(rpb_ref, mask_ref, t_scr.at[j], pair, ridx)

    for j in range(ATTN_PAIRS):
        _transpose_keys(va_ref.at[:, lanes[j]], vta_scr.at[j])
    _transpose_keys(vd_ref, vtb_scr, ones_below=True)
    lane = lax.broadcasted_iota(jnp.int32, (1, 2 * Q_BLK), 1)
    streams = []
    for j, pair in enumerate(pairs):
        sink_row = jnp.where(lane < Q_BLK, sink_ref[2 * pair], sink_ref[2 * pair + 1]) * LOG2E
        streams.append((_Stream(qa_ref.at[:, lanes[j]], ka_ref.at[:, lanes[j]], vta_scr.at[j], oa_ref.at[:, lanes[j]]),
                        _Stream(qb_ref.at[:, lanes[j]], kd_ref, vtb_scr, ob_ref.at[:, lanes[j]], sink_row, sums_below=True)))
    blocks = []
    for i in range(nblk):
        lo, hi = _na_key_rows(i, seq // GRID_W)
        n_keys = (hi - lo) * GRID_W
        first, last = max(i - 1, 0), min(i + 1, nblk - 1)
        add_mask = functools.partial(_swa_add_mask, edge_ref=edge_ref, first_tile=first, n=i)
        for j, (na, swa) in enumerate(streams):
            add_bias = functools.partial(lambda st, j, cls, n: st + t_scr[j, cls, :n, :],
                                         j=j, cls=_na_class_of_block(i, nblk), n=n_keys)
            blocks.append(_Block(na, i * Q_BLK, lo * GRID_W, n_keys, add_bias))
            blocks.append(_Block(swa, i * Q_BLK, first * Q_BLK, (last - first + 1) * Q_BLK, add_mask))
    _attend(blocks, ATTN_SKEW)


def _attention(proj, rpb_flat, sink, seq, batch):
    masks, ridx = _na_static_tables(seq)
    edges = _swa_edge_masks()
    n_tok = proj.shape[0]
    n_steps = A_WIDTH // (ATTN_PAIRS * LANES)
    assert B_WIDTH == A_WIDTH and B_WIDTH // (SWA_KV_HEADS * LANES) == ATTN_PAIRS, "one windowed kv head per step"
    width = ATTN_PAIRS * LANES
    col = lambda base: pl.BlockSpec((seq, width), lambda s, b: (b, base // width + s))
    kv_col = lambda base: pl.BlockSpec((seq, LANES), lambda s, b: (b, base // LANES + s))
    whole = lambda a: pl.BlockSpec(a.shape, lambda s, b: (0,) * a.ndim)
    table_bytes = ATTN_PAIRS * NA_CLASSES * NA_MAX_KEYS * 2 * Q_BLK * 4
    vmem = (2 * (6 * ATTN_PAIRS + 2) * seq * LANES * 2 + 2 * (masks.size + edges.size) * 4 + table_bytes
            + (ATTN_PAIRS + 1) * seq * LANES * 2)
    out = jax.ShapeDtypeStruct((n_tok, A_WIDTH), jnp.bfloat16)
    out_tile = pl.BlockSpec((seq, width), lambda s, b: (b, s))
    return pl.pallas_call(
        functools.partial(_attn_kernel, ridx=ridx, seq=seq),
        out_shape=[out, out],
        grid=(n_steps, batch),
        in_specs=[
            pl.BlockSpec(memory_space=pltpu.SMEM),
            pl.BlockSpec(memory_space=pltpu.SMEM),
            col(COL_QA), col(COL_KA), col(COL_VA), col(COL_QB), kv_col(COL_KD), kv_col(COL_VD),
            whole(masks), whole(edges),
        ],
        out_specs=[out_tile, out_tile],
        scratch_shapes=[pltpu.VMEM((ATTN_PAIRS, NA_CLASSES, NA_MAX_KEYS, 2 * Q_BLK), jnp.float32),
                        pltpu.VMEM((ATTN_PAIRS, seq // LANES, LANES, LANES), jnp.bfloat16),
                        pltpu.VMEM((seq // LANES, LANES, LANES), jnp.bfloat16)],
        compiler_params=pltpu.CompilerParams(
            dimension_semantics=("arbitrary", "arbitrary"), vmem_limit_bytes=_attn_vmem_limit(vmem)),
        name="attention",
    )(rpb_flat, sink, proj, proj, proj, proj, proj, proj, jnp.asarray(masks), jnp.asarray(edges))


def _out_ffn_kernel(x_ref, oa_ref, ob_ref, gna_ref, gswa_ref, wo_ref, gffn_ref, wg_ref, wu_ref, wd_ref,
                    gfin_ref, o_ref, *, ff_chunk, sub_rows):
    d_ff = wg_ref.shape[1]
    row_slices = [slice(r0, r0 + sub_rows) for r0 in range(0, x_ref.shape[0], sub_rows)]
    mixed = []
    for rows in row_slices:
        na = _rms(oa_ref[rows, :].astype(jnp.float32), gna_ref[...]).astype(jnp.bfloat16)
        nb = _rms(ob_ref[rows, :].astype(jnp.float32), gswa_ref[...]).astype(jnp.bfloat16)
        x1 = (x_ref[rows, :]
              + jnp.dot(na, wo_ref[:A_WIDTH, :], preferred_element_type=jnp.float32)
              + jnp.dot(nb, wo_ref[A_WIDTH:, :], preferred_element_type=jnp.float32))
        mixed.append((x1, _rms(x1, gffn_ref[...]).astype(jnp.bfloat16)))
    for rows, (x1, h) in zip(row_slices, mixed):
        acc = jnp.zeros(x1.shape, jnp.float32)
        for c0 in range(0, d_ff, ff_chunk):
            gate = jnp.dot(h, wg_ref[:, c0:c0 + ff_chunk], preferred_element_type=jnp.float32)
            up = jnp.dot(h, wu_ref[:, c0:c0 + ff_chunk], preferred_element_type=jnp.float32)
            act = (jax.nn.silu(gate) * up).astype(jnp.bfloat16)
            acc = acc + jnp.dot(act, wd_ref[c0:c0 + ff_chunk, :], preferred_element_type=jnp.float32)
        o_ref[rows, :] = _rms(x1 + acc, gfin_ref[...])


def _out_ffn(x2d, o_na, o_swa, g_na, g_swa, wo, g_ffn, wg, wu, wd, g_fin):
    n_tok = x2d.shape[0]
    tm, sub_rows, ff_chunk = FFN_ROWS, FFN_SUB_ROWS, FFN_CHUNK
    d_ff = wg.shape[1]
    const = lambda i: (0, 0)
    row = lambda i: (i, 0)
    resident = lambda shape: pl.BlockSpec(shape, const, pipeline_mode=pl.Buffered(1))
    weights = (D_MODEL * D_MODEL + 3 * D_MODEL * d_ff) * 2
    tiles = 2 * 2 * tm * D_MODEL * 4 + 2 * 2 * tm * A_WIDTH * 2
    vmem = weights + tiles + 2 * (6 * sub_rows * D_MODEL * 4 + 4 * sub_rows * ff_chunk * 4)
    return pl.pallas_call(
        functools.partial(_out_ffn_kernel, ff_chunk=ff_chunk, sub_rows=sub_rows),
        out_shape=jax.ShapeDtypeStruct((n_tok, D_MODEL), jnp.float32),
        grid=(n_tok // tm,),
        in_specs=[
            pl.BlockSpec((tm, D_MODEL), row),
            pl.BlockSpec((tm, A_WIDTH), row),
            pl.BlockSpec((tm, B_WIDTH), row),
            pl.BlockSpec((1, A_WIDTH), const),
            pl.BlockSpec((1, B_WIDTH), const),
            resident((D_MODEL, D_MODEL)),
            pl.BlockSpec((1, D_MODEL), const),
            resident((D_MODEL, d_ff)),
            resident((D_MODEL, d_ff)),
            resident((d_ff, D_MODEL)),
            pl.BlockSpec((1, D_MODEL), const),
        ],
        out_specs=pl.BlockSpec((tm, D_MODEL), row),
        compiler_params=pltpu.CompilerParams(
            dimension_semantics=("arbitrary",), vmem_limit_bytes=_vmem_limit(vmem + (8 << 20))),
        name="out_ffn",
    )(x2d, o_na, o_swa, g_na, g_swa, wo, g_ffn, wg, wu, wd, g_fin)


def kernel(x, g_norm_mix, w_in, na_rpb, swa_sink, g_out_na, g_out_swa, w_out, g_norm_ffn, w_gate, w_up, w_down,
           g_final):
    batch, seq, d_model = x.shape
    assert w_in.shape[0] == 1, "the final RMSNorm is fused into the single layer's FFN call"
    assert d_model == D_MODEL and seq % Q_BLK == 0 and seq // GRID_W >= NA_MAX_KEY_ROWS
    assert B_KV_WIDTH == LANES, "the windowed k (and v) heads together fill one lane tile"
    rope = _rope_tables(seq)
    x2d = x.reshape(batch * seq, d_model)
    proj, (wo, wg, wu, wd) = _in_proj(x2d, g_norm_mix[0][None, :], w_in[0], rope, seq,
                                      (w_out[0], w_gate[0], w_up[0], w_down[0]))
    o_na, o_swa = _attention(proj, na_rpb[0].reshape(-1), swa_sink[0], seq, batch)
    out = _out_ffn(x2d, o_na, o_swa, g_out_na[0][None, :], g_out_swa[0][None, :], wo,
                   g_norm_ffn[0][None, :], wg, wu, wd, g_final[None, :])
    return out.reshape(batch, seq, d_model)
```

```python
import functools
from typing import NamedTuple

import numpy as np
import jax
import jax.numpy as jnp
from jax import lax
from jax.experimental import pallas as pl
from jax.experimental.pallas import tpu as pltpu

D_MODEL = 1024
HEAD_DIM = 64
NA_HEADS = 8
SWA_HEADS = 8
SWA_KV_HEADS = 2
A_WIDTH = NA_HEADS * HEAD_DIM
B_WIDTH = SWA_HEADS * HEAD_DIM
B_KV_WIDTH = SWA_KV_HEADS * HEAD_DIM
IN_WIDTH = 3 * A_WIDTH + B_WIDTH + 2 * B_KV_WIDTH
GRID_W = 64
NA_ROWS = 8
NA_COLS = 16
SWA_WINDOW = 128
ROPE_THETA = 500000.0
ROPE_DIM = HEAD_DIM // 4
RMS_EPS = 1e-6
NEG_INF = -1e30
LOG2E = 1.4426950408889634
Q_SCALE = HEAD_DIM ** -0.5 * LOG2E

LANES = 128
SUBLANES = 8
BF16_ROWS = 16
VMEM_CAPACITY = 64 * 1024 * 1024

PROJ_WIDTH = 3 * A_WIDTH + B_WIDTH + 2 * SWA_KV_HEADS * LANES
COL_QA, COL_KA, COL_VA = 0, A_WIDTH, 2 * A_WIDTH
COL_QB = 3 * A_WIDTH
COL_KD = COL_QB + B_WIDTH
COL_VD = COL_KD + SWA_KV_HEADS * LANES
W_COL_KB = COL_QB + B_WIDTH

Q_BLK = 128
NA_Q_ROWS = Q_BLK // GRID_W
NA_MAX_KEY_ROWS = NA_ROWS + NA_Q_ROWS - 1
NA_MAX_KEYS = NA_MAX_KEY_ROWS * GRID_W
NA_CLASSES = 5
ATTN_SKEW = 6
ATTN_PAIRS = 2
IN_PROJ_ROWS = 1024
FFN_ROWS = 1024
FFN_SUB_ROWS = 512
FFN_CHUNK = 256


def _vmem_limit(nbytes):
    return int(min(VMEM_CAPACITY - (2 << 20), max(nbytes, 16 << 20)))


def _attn_vmem_limit(nbytes):
    assert nbytes + (16 << 20) <= VMEM_CAPACITY - (2 << 20)
    return VMEM_CAPACITY - (2 << 20)


def _na_class_of_block(i, nblk):
    if i < 2:
        return i
    if i >= nblk - 2:
        return NA_CLASSES - (nblk - i)
    return 2


def _na_key_rows(i, rows):
    first = NA_Q_ROWS * i
    lo = int(np.clip(first - NA_ROWS // 2, 0, rows - NA_ROWS))
    hi = int(np.clip(first + NA_Q_ROWS - 1 - NA_ROWS // 2, 0, rows - NA_ROWS)) + NA_ROWS
    return lo, hi


def _na_static_tables(seq):
    rows = seq // GRID_W
    nblk = seq // Q_BLK
    masks = np.full((NA_CLASSES, NA_MAX_KEYS, Q_BLK), NEG_INF, np.float32)
    ridx = np.full((NA_CLASSES, NA_Q_ROWS, NA_MAX_KEY_ROWS), -1, np.int64)
    seen = {}
    for i in range(nblk):
        cls = _na_class_of_block(i, nblk)
        lo, hi = _na_key_rows(i, rows)
        assert (lo * GRID_W) % LANES == 0 and hi - lo <= NA_MAX_KEY_ROWS
        q = np.arange(Q_BLK) + i * Q_BLK
        qr, qc = q // GRID_W, q % GRID_W
        k = np.arange((hi - lo) * GRID_W) + lo * GRID_W
        kr, kc = k // GRID_W, k % GRID_W
        rs = np.clip(qr - NA_ROWS // 2, 0, rows - NA_ROWS)
        cs = np.clip(qc - NA_COLS // 2, 0, GRID_W - NA_COLS)
        ok_r = (kr[:, None] >= rs[None, :]) & (kr[:, None] < rs[None, :] + NA_ROWS)
        ok_c = (kc[:, None] >= cs[None, :]) & (kc[:, None] < cs[None, :] + NA_COLS)
        m = np.full((NA_MAX_KEYS, Q_BLK), NEG_INF, np.float32)
        m[:k.size] = np.where(ok_r & ok_c, 0.0, NEG_INF)
        rr = np.full((NA_Q_ROWS, NA_MAX_KEY_ROWS), -1, np.int64)
        for a in range(NA_Q_ROWS):
            for b in range(hi - lo):
                r, krow = NA_Q_ROWS * i + a, lo + b
                if rs[a * GRID_W] <= krow < rs[a * GRID_W] + NA_ROWS:
                    rr[a, b] = krow - r + NA_ROWS - 1
        if cls in seen:
            assert np.array_equal(seen[cls][0], m) and np.array_equal(seen[cls][1], rr)
        seen[cls] = (m, rr)
        masks[cls], ridx[cls] = m, rr
    return masks, ridx


def _swa_edge_masks():
    assert SWA_WINDOW == Q_BLK
    key = np.arange(Q_BLK)[:, None]
    qry = np.arange(Q_BLK)[None, :]
    tri = np.stack([np.where(key >= qry, 0.0, NEG_INF), np.where(key <= qry, 0.0, NEG_INF)]).astype(np.float32)
    return np.ascontiguousarray(np.concatenate([tri, tri], axis=2))


def _rope_tables(seq):
    half = ROPE_DIM // 2
    inv_freq = np.float32(ROPE_THETA) ** (-np.arange(0, ROPE_DIM, 2, dtype=np.float32) / np.float32(ROPE_DIM))
    ang = np.arange(seq, dtype=np.float32)[:, None] * inv_freq[None, :].astype(np.float32)
    cos, sin = np.cos(ang).astype(np.float32), np.sin(ang).astype(np.float32)
    ones = np.ones((seq, HEAD_DIM - ROPE_DIM), np.float32)
    zeros = np.zeros((seq, HEAD_DIM - ROPE_DIM), np.float32)
    zh = np.zeros((seq, half), np.float32)
    c = np.concatenate([cos, cos, ones], axis=1)
    sa = np.concatenate([-sin, zh, zeros], axis=1)
    sb = np.concatenate([zh, sin, zeros], axis=1)
    rep = LANES // HEAD_DIM
    return tuple(jnp.asarray(np.tile(t, (1, rep))) for t in (c, sa, sb))


def _rms(x, g):
    return x * lax.rsqrt(jnp.mean(x * x, axis=-1, keepdims=True) + RMS_EPS) * g


def _rope_tile(a, c, sa, sb):
    return a * c + pltpu.roll(a, LANES - ROPE_DIM // 2, axis=1) * sa + pltpu.roll(a, ROPE_DIM // 2, axis=1) * sb


def _in_proj_kernel(x_ref, g_ref, w_ref, c_ref, sa_ref, sb_ref, *rest):
    n_extra = (len(rest) - 2) // 2
    extra_in, o_ref, extra_out, w_scr = rest[:n_extra], rest[n_extra], rest[n_extra + 1:-1], rest[-1]

    @pl.when(pl.program_id(0) == 0)
    def _():
        for c0 in range(0, w_ref.shape[1], LANES):
            w_scr[:, c0:c0 + LANES] = w_ref[:, c0:c0 + LANES].astype(w_scr.dtype)

    for src, dst in zip(extra_in, extra_out):
        dst[...] = src[...].astype(dst.dtype)

    h = _rms(x_ref[...], g_ref[...]).astype(jnp.bfloat16)
    tm, seq = x_ref.shape[0], c_ref.shape[0]
    pos = pl.ds(pl.multiple_of((pl.program_id(0) % (seq // tm)) * tm, tm), tm)
    c, sa, sb = c_ref[pos, :], sa_ref[pos, :], sb_ref[pos, :]

    def proj(col, width):
        return jnp.dot(h, w_scr[:, col:col + width], preferred_element_type=jnp.float32)

    kv = proj(W_COL_KB, 2 * B_KV_WIDTH)
    left = lax.broadcasted_iota(jnp.int32, (h.shape[0], LANES), 1) < HEAD_DIM
    for col, tile in ((COL_KD, _rope_tile(kv[:, :B_KV_WIDTH], c, sa, sb)), (COL_VD, kv[:, B_KV_WIDTH:])):
        swapped = pltpu.roll(tile, HEAD_DIM, axis=1)
        o_ref[:, col:col + LANES] = jnp.where(left, tile, swapped).astype(o_ref.dtype)
        o_ref[:, col + LANES:col + 2 * LANES] = jnp.where(left, swapped, tile).astype(o_ref.dtype)
    qb = proj(COL_QB, B_WIDTH)
    for t in range(B_WIDTH // LANES):
        lo = t * LANES
        tile = _rope_tile(qb[:, lo:lo + LANES], c, sa, sb) * Q_SCALE
        o_ref[:, COL_QB + lo:COL_QB + lo + LANES] = tile.astype(o_ref.dtype)
    o_ref[:, COL_QA:COL_QA + A_WIDTH] = (proj(COL_QA, A_WIDTH) * Q_SCALE).astype(o_ref.dtype)
    o_ref[:, COL_KA:COL_KA + A_WIDTH] = proj(COL_KA, A_WIDTH).astype(o_ref.dtype)
    o_ref[:, COL_VA:COL_VA + A_WIDTH] = proj(COL_VA, A_WIDTH).astype(o_ref.dtype)


def _in_proj(x2d, g, w_in, rope, seq, later_weights):
    n_tok = x2d.shape[0]
    tm = IN_PROJ_ROWS
    steps = n_tok // tm
    assert seq % tm == 0
    const = lambda i: (0, 0)
    row = lambda i: (i, 0)
    slices = [(w.shape[0] // steps, w.shape[1]) for w in later_weights]
    assert all(w.shape[0] % steps == 0 and rows % BF16_ROWS == 0 for w, (rows, _) in zip(later_weights, slices))
    slice_bytes = sum(rows * cols for rows, cols in slices)
    vmem = (2 * tm * D_MODEL * 4 + D_MODEL * IN_WIDTH * (4 + 2) + 2 * tm * PROJ_WIDTH * 2 + 3 * tm * B_WIDTH * 4 * 2
            + 2 * slice_bytes * (4 + 2) + 3 * 2 * seq * LANES * 4)
    outs = pl.pallas_call(
        _in_proj_kernel,
        out_shape=[jax.ShapeDtypeStruct((n_tok, PROJ_WIDTH), jnp.bfloat16)]
        + [jax.ShapeDtypeStruct(w.shape, jnp.bfloat16) for w in later_weights],
        grid=(steps,),
        in_specs=[
            pl.BlockSpec((tm, D_MODEL), row),
            pl.BlockSpec((1, D_MODEL), const),
            pl.BlockSpec((D_MODEL, IN_WIDTH), const, pipeline_mode=pl.Buffered(1)),
            pl.BlockSpec((seq, LANES), const),
            pl.BlockSpec((seq, LANES), const),
            pl.BlockSpec((seq, LANES), const),
        ] + [pl.BlockSpec(blk, row) for blk in slices],
        out_specs=[pl.BlockSpec((tm, PROJ_WIDTH), row)] + [pl.BlockSpec(blk, row) for blk in slices],
        scratch_shapes=[pltpu.VMEM((D_MODEL, IN_WIDTH), jnp.bfloat16)],
        compiler_params=pltpu.CompilerParams(
            dimension_semantics=("arbitrary",), vmem_limit_bytes=_vmem_limit(vmem + (8 << 20))),
        name="in_proj",
    )(x2d, g, w_in, *rope, *later_weights)
    return outs[0], outs[1:]


def _transpose_keys(v_ref, vt_ref, ones_below=False):
    top = lax.broadcasted_iota(jnp.int32, (LANES, LANES), 0) < HEAD_DIM
    for j in range(v_ref.shape[0] // LANES):
        tile = v_ref[j * LANES:(j + 1) * LANES, :].T
        vt_ref[j] = jnp.where(top, tile, jnp.ones_like(tile)) if ones_below else tile


class _Stream(NamedTuple):
    q_ref: object
    k_ref: object
    vt_ref: object
    o_ref: object
    sink_row: object = None
    sums_below: bool = False


class _Block(NamedTuple):
    stream: _Stream
    q0: int
    k0: int
    n_keys: int
    add_bias: object


def _attend(blocks, skew):
    lane = lax.broadcasted_iota(jnp.int32, (Q_BLK, LANES), 1)
    left = lane < HEAD_DIM
    top = lax.broadcasted_iota(jnp.int32, (LANES, Q_BLK), 0) < HEAD_DIM

    def scores_of(blk):
        q2 = blk.stream.q_ref[blk.q0:blk.q0 + Q_BLK, :]
        zero = jnp.zeros_like(q2)
        qq = jnp.concatenate([jnp.where(left, q2, zero), jnp.where(left, zero, q2)], axis=0)
        st = lax.dot_general(blk.stream.k_ref[blk.k0:blk.k0 + blk.n_keys, :], qq, (((1,), (1,)), ((), ())),
                             preferred_element_type=jnp.float32)
        return blk.add_bias(st)

    def finish(blk, st):
        _, _, vt_ref, o_ref, sink_row, sums_below = blk.stream
        m = jnp.max(st, axis=0, keepdims=True)
        if sink_row is not None:
            m = jnp.maximum(m, sink_row)
        p = jnp.exp2(st - m)
        tile0, full, part = blk.k0 // LANES, blk.n_keys // LANES, blk.n_keys % LANES
        pieces = [vt_ref[tile0 + j] for j in range(full)]
        if part:
            pieces.append(vt_ref[tile0 + full][:, :part])
        ot = jnp.dot(jnp.concatenate(pieces, axis=1), p.astype(jnp.bfloat16), preferred_element_type=jnp.float32)
        den = ot[HEAD_DIM:HEAD_DIM + 1, :] if sums_below else jnp.sum(p, axis=0, keepdims=True)
        if sink_row is not None:
            den = den + jnp.exp2(sink_row - m)
        if sums_below:
            ot = ot[:HEAD_DIM, :] / den
            x = jnp.concatenate([ot[:, :Q_BLK], ot[:, Q_BLK:]], axis=0)
        else:
            ot = ot / den
            x = jnp.where(top, ot[:, :Q_BLK], ot[:, Q_BLK:])
        o_ref[blk.q0:blk.q0 + Q_BLK, :] = x.astype(o_ref.dtype).T

    pending = [scores_of(blk) for blk in blocks[:skew]]
    for c, blk in enumerate(blocks):
        st = pending.pop(0)
        if c + skew < len(blocks):
            pending.append(scores_of(blocks[c + skew]))
        finish(blk, st)


def _build_na_bias(rpb_ref, mask_ref, t_scr, hp, ridx):
    n_rrow, n_rcol = 2 * NA_ROWS - 1, 2 * NA_COLS - 1
    lane8 = lax.broadcasted_iota(jnp.int32, (SUBLANES, LANES), 1)
    second_row = lax.broadcasted_iota(jnp.int32, (GRID_W, LANES), 1) >= GRID_W
    used_rows = sorted({int(r) for r in ridx.reshape(-1) if r >= 0})
    zero = jnp.zeros((GRID_W, LANES), jnp.float32)
    for hh in range(2):
        h = 2 * hp + hh
        toeplitz = {-1: zero}
        for rho in used_rows:
            band = jnp.zeros((SUBLANES, LANES), jnp.float32)
            for d in range(n_rcol):
                band = jnp.where(lane8 == (NA_COLS - 1 - d) % LANES, rpb_ref[(h * n_rrow + rho) * n_rcol + d] * LOG2E, band)
            rows = jnp.concatenate([band] * (GRID_W // SUBLANES), axis=0)
            first = pltpu.roll(rows, 0, axis=1, stride=1, stride_axis=0)
            second = pltpu.roll(rows, GRID_W, axis=1, stride=1, stride_axis=0)
            toeplitz[rho] = jnp.where(second_row, second, first)
        for cls in range(NA_CLASSES):
            for b in range(NA_MAX_KEY_ROWS):
                blk = jnp.where(second_row, toeplitz[int(ridx[cls, 1, b])], toeplitz[int(ridx[cls, 0, b])])
                ks = slice(b * GRID_W, (b + 1) * GRID_W)
                t_scr[cls, ks, hh * Q_BLK:(hh + 1) * Q_BLK] = blk + mask_ref[cls, ks, :]


def _swa_add_mask(st, edge_ref, first_tile, n):
    parts = []
    for j in range(st.shape[0] // Q_BLK):
        part = st[j * Q_BLK:(j + 1) * Q_BLK, :]
        if first_tile + j != n:
            part = part + edge_ref[0 if first_tile + j < n else 1]
        parts.append(part)
    return jnp.concatenate(parts, axis=0)


def _attn_kernel(rpb_ref, sink_ref, qa_ref, ka_ref, va_ref, qb_ref, kd_ref, vd_ref, mask_ref, edge_ref,
                 oa_ref, ob_ref, t_scr, vta_scr, vtb_scr, *, ridx, seq):
    nblk = seq // Q_BLK
    pairs = [ATTN_PAIRS * pl.program_id(0) + j for j in range(ATTN_PAIRS)]
    lanes = [slice(j * LANES, (j + 1) * LANES) for j in range(ATTN_PAIRS)]

    @pl.when(pl.program_id(1) == 0)
    def _():
        for j, pair in enumerate(pairs):
            _build_na_bias(rpb_ref, mask_ref, t_scr.at[j], pair, ridx)

    for j in range(ATTN_PAIRS):
        _transpose_keys(va_ref.at[:, lanes[j]], vta_scr.at[j])
    _transpose_keys(vd_ref, vtb_scr, ones_below=True)
    lane = lax.broadcasted_iota(jnp.int32, (1, 2 * Q_BLK), 1)
    streams = []
    for j, pair in enumerate(pairs):
        sink_row = jnp.where(lane < Q_BLK, sink_ref[2 * pair], sink_ref[2 * pair + 1]) * LOG2E
        streams.append((_Stream(qa_ref.at[:, lanes[j]], ka_ref.at[:, lanes[j]], vta_scr.at[j], oa_ref.at[:, lanes[j]]),
                        _Stream(qb_ref.at[:, lanes[j]], kd_ref, vtb_scr, ob_ref.at[:, lanes[j]], sink_row, sums_below=True)))
    blocks = []
    for i in range(nblk):
        lo, hi = _na_key_rows(i, seq // GRID_W)
        n_keys = (hi - lo) * GRID_W
        first, last = max(i - 1, 0), min(i + 1, nblk - 1)
        add_mask = functools.partial(_swa_add_mask, edge_ref=edge_ref, first_tile=first, n=i)
        for j, (na, swa) in enumerate(streams):
            add_bias = functools.partial(lambda st, j, cls, n: st + t_scr[j, cls, :n, :],
                                         j=j, cls=_na_class_of_block(i, nblk), n=n_keys)
            blocks.append(_Block(na, i * Q_BLK, lo * GRID_W, n_keys, add_bias))
            blocks.append(_Block(swa, i * Q_BLK, first * Q_BLK, (last - first + 1) * Q_BLK, add_mask))
    _attend(blocks, ATTN_SKEW)


def _attention(proj, rpb_flat, sink, seq, batch):
    masks, ridx = _na_static_tables(seq)
    edges = _swa_edge_masks()
    n_tok = proj.shape[0]
    n_steps = A_WIDTH // (ATTN_PAIRS * LANES)
    assert B_WIDTH == A_WIDTH and B_WIDTH // (SWA_KV_HEADS * LANES) == ATTN_PAIRS, "one windowed kv head per step"
    width = ATTN_PAIRS * LANES
    col = lambda base: pl.BlockSpec((seq, width), lambda s, b: (b, base // width + s))
    kv_col = lambda base: pl.BlockSpec((seq, LANES), lambda s, b: (b, base // LANES + s))
    whole = lambda a: pl.BlockSpec(a.shape, lambda s, b: (0,) * a.ndim)
    table_bytes = ATTN_PAIRS * NA_CLASSES * NA_MAX_KEYS * 2 * Q_BLK * 4
    vmem = (2 * (6 * ATTN_PAIRS + 2) * seq * LANES * 2 + 2 * (masks.size + edges.size) * 4 + table_bytes
            + (ATTN_PAIRS + 1) * seq * LANES * 2)
    out = jax.ShapeDtypeStruct((n_tok, A_WIDTH), jnp.bfloat16)
    out_tile = pl.BlockSpec((seq, width), lambda s, b: (b, s))
    return pl.pallas_call(
        functools.partial(_attn_kernel, ridx=ridx, seq=seq),
        out_shape=[out, out],
        grid=(n_steps, batch),
        in_specs=[
            pl.BlockSpec(memory_space=pltpu.SMEM),
            pl.BlockSpec(memory_space=pltpu.SMEM),
            col(COL_QA), col(COL_KA), col(COL_VA), col(COL_QB), kv_col(COL_KD), kv_col(COL_VD),
            whole(masks), whole(edges),
        ],
        out_specs=[out_tile, out_tile],
        scratch_shapes=[pltpu.VMEM((ATTN_PAIRS, NA_CLASSES, NA_MAX_KEYS, 2 * Q_BLK), jnp.float32),
                        pltpu.VMEM((ATTN_PAIRS, seq // LANES, LANES, LANES), jnp.bfloat16),
                        pltpu.VMEM((seq // LANES, LANES, LANES), jnp.bfloat16)],
        compiler_params=pltpu.CompilerParams(
            dimension_semantics=("arbitrary", "arbitrary"), vmem_limit_bytes=_attn_vmem_limit(vmem)),
        name="attention",
    )(rpb_flat, sink, proj, proj, proj, proj, proj, proj, jnp.asarray(masks), jnp.asarray(edges))


def _out_ffn_kernel(x_ref, oa_ref, ob_ref, gna_ref, gswa_ref, wo_ref, gffn_ref, wg_ref, wu_ref, wd_ref,
                    gfin_ref, o_ref, *, ff_chunk, sub_rows):
    d_ff = wg_ref.shape[1]
    row_slices = [slice(r0, r0 + sub_rows) for r0 in range(0, x_ref.shape[0], sub_rows)]
    mixed = []
    for rows in row_slices:
        na = _rms(oa_ref[rows, :].astype(jnp.float32), gna_ref[...]).astype(jnp.bfloat16)
        nb = _rms(ob_ref[rows, :].astype(jnp.float32), gswa_ref[...]).astype(jnp.bfloat16)
        x1 = (x_ref[rows, :]
              + jnp.dot(na, wo_ref[:A_WIDTH, :], preferred_element_type=jnp.float32)
              + jnp.dot(nb, wo_ref[A_WIDTH:, :], preferred_element_type=jnp.float32))
        mixed.append((x1, _rms(x1, gffn_ref[...]).astype(jnp.bfloat16)))
    for rows, (x1, h) in zip(row_slices, mixed):
        acc = jnp.zeros(x1.shape, jnp.float32)
        for c0 in range(0, d_ff, ff_chunk):
            gate = jnp.dot(h, wg_ref[:, c0:c0 + ff_chunk], preferred_element_type=jnp.float32)
            up = jnp.dot(h, wu_ref[:, c0:c0 + ff_chunk], preferred_element_type=jnp.float32)
            act = (jax.nn.silu(gate) * up).astype(jnp.bfloat16)
            acc = acc + jnp.dot(act, wd_ref[c0:c0 + ff_chunk, :], preferred_element_type=jnp.float32)
        o_ref[rows, :] = _rms(x1 + acc, gfin_ref[...])


def _out_ffn(x2d, o_na, o_swa, g_na, g_swa, wo, g_ffn, wg, wu, wd, g_fin):
    n_tok = x2d.shape[0]
    tm, sub_rows, ff_chunk = FFN_ROWS, FFN_SUB_ROWS, FFN_CHUNK
    d_ff = wg.shape[1]
    const = lambda i: (0, 0)
    row = lambda i: (i, 0)
    resident = lambda shape: pl.BlockSpec(shape, const, pipeline_mode=pl.Buffered(1))
    weights = (D_MODEL * D_MODEL + 3 * D_MODEL * d_ff) * 2
    tiles = 2 * 2 * tm * D_MODEL * 4 + 2 * 2 * tm * A_WIDTH * 2
    vmem = weights + tiles + 2 * (6 * sub_rows * D_MODEL * 4 + 4 * sub_rows * ff_chunk * 4)
    return pl.pallas_call(
        functools.partial(_out_ffn_kernel, ff_chunk=ff_chunk, sub_rows=sub_rows),
        out_shape=jax.ShapeDtypeStruct((n_tok, D_MODEL), jnp.float32),
        grid=(n_tok // tm,),
        in_specs=[
            pl.BlockSpec((tm, D_MODEL), row),
            pl.BlockSpec((tm, A_WIDTH), row),
            pl.BlockSpec((tm, B_WIDTH), row),
            pl.BlockSpec((1, A_WIDTH), const),
            pl.BlockSpec((1, B_WIDTH), const),
            resident((D_MODEL, D_MODEL)),
            pl.BlockSpec((1, D_MODEL), const),
            resident((D_MODEL, d_ff)),
            resident((D_MODEL, d_ff)),
            resident((d_ff, D_MODEL)),
            pl.BlockSpec((1, D_MODEL), const),
        ],
        out_specs=pl.BlockSpec((tm, D_MODEL), row),
        compiler_params=pltpu.CompilerParams(
            dimension_semantics=("arbitrary",), vmem_limit_bytes=_vmem_limit(vmem + (8 << 20))),
        name="out_ffn",
    )(x2d, o_na, o_swa, g_na, g_swa, wo, g_ffn, wg, wu, wd, g_fin)


def kernel(x, g_norm_mix, w_in, na_rpb, swa_sink, g_out_na, g_out_swa, w_out, g_norm_ffn, w_gate, w_up, w_down,
           g_final):
    batch, seq, d_model = x.shape
    assert w_in.shape[0] == 1, "the final RMSNorm is fused into the single layer's FFN call"
    assert d_model == D_MODEL and seq % Q_BLK == 0 and seq // GRID_W >= NA_MAX_KEY_ROWS
    assert B_KV_WIDTH == LANES, "the windowed k (and v) heads together fill one lane tile"
    rope = _rope_tables(seq)
    x2d = x.reshape(batch * seq, d_model)
    proj, (wo, wg, wu, wd) = _in_proj(x2d, g_norm_mix[0][None, :], w_in[0], rope, seq,
                                      (w_out[0], w_gate[0], w_up[0], w_down[0]))
    o_na, o_swa = _attention(proj, na_rpb[0].reshape(-1), swa_sink[0], seq, batch)
    out = _out_ffn(x2d, o_na, o_swa, g_out_na[0][None, :], g_out_swa[0][None, :], wo,
                   g_norm_ffn[0][None, :], wg, wu, wd, g_final[None, :])
    return out.reshape(batch, seq, d_model)
```

```python
import functools
from typing import NamedTuple

import numpy as np
import jax
import jax.numpy as jnp
from jax import lax
from jax.experimental import pallas as pl
from jax.experimental.pallas import tpu as pltpu

D_MODEL = 1024
HEAD_DIM = 64
NA_HEADS = 8
SWA_HEADS = 8
SWA_KV_HEADS = 2
A_WIDTH = NA_HEADS * HEAD_DIM
B_WIDTH = SWA_HEADS * HEAD_DIM
B_KV_WIDTH = SWA_KV_HEADS * HEAD_DIM
IN_WIDTH = 3 * A_WIDTH + B_WIDTH + 2 * B_KV_WIDTH
GRID_W = 64
NA_ROWS = 8
NA_COLS = 16
SWA_WINDOW = 128
ROPE_THETA = 500000.0
ROPE_DIM = HEAD_DIM // 4
RMS_EPS = 1e-6
NEG_INF = -1e30
LOG2E = 1.4426950408889634
Q_SCALE = HEAD_DIM ** -0.5 * LOG2E

LANES = 128
SUBLANES = 8
BF16_ROWS = 16
VMEM_CAPACITY = 64 * 1024 * 1024

PROJ_WIDTH = 3 * A_WIDTH + B_WIDTH + 2 * SWA_KV_HEADS * LANES
COL_QA, COL_KA, COL_VA = 0, A_WIDTH, 2 * A_WIDTH
COL_QB = 3 * A_WIDTH
COL_KD = COL_QB + B_WIDTH
COL_VD = COL_KD + SWA_KV_HEADS * LANES
W_COL_KB = COL_QB + B_WIDTH

Q_BLK = 128
NA_Q_ROWS = Q_BLK // GRID_W
NA_MAX_KEY_ROWS = NA_ROWS + NA_Q_ROWS - 1
NA_MAX_KEYS = NA_MAX_KEY_ROWS * GRID_W
NA_CLASSES = 5
ATTN_SKEW = 6
ATTN_PAIRS = 2
IN_PROJ_ROWS = 1024
FFN_ROWS = 1024
FFN_SUB_ROWS = 512
FFN_CHUNK = 256


def _vmem_limit(nbytes):
    return int(min(VMEM_CAPACITY - (2 << 20), max(nbytes, 16 << 20)))


def _attn_vmem_limit(nbytes):
    assert nbytes + (16 << 20) <= VMEM_CAPACITY - (2 << 20)
    return VMEM_CAPACITY - (2 << 20)


def _na_class_of_block(i, nblk):
    if i < 2:
        return i
    if i >= nblk - 2:
        return NA_CLASSES - (nblk - i)
    return 2


def _na_key_rows(i, rows):
    first = NA_Q_ROWS * i
    lo = int(np.clip(first - NA_ROWS // 2, 0, rows - NA_ROWS))
    hi = int(np.clip(first + NA_Q_ROWS - 1 - NA_ROWS // 2, 0, rows - NA_ROWS)) + NA_ROWS
    return lo, hi


def _na_static_tables(seq):
    rows = seq // GRID_W
    nblk = seq // Q_BLK
    masks = np.full((NA_CLASSES, NA_MAX_KEYS, Q_BLK), NEG_INF, np.float32)
    ridx = np.full((NA_CLASSES, NA_Q_ROWS, NA_MAX_KEY_ROWS), -1, np.int64)
    seen = {}
    for i in range(nblk):
        cls = _na_class_of_block(i, nblk)
        lo, hi = _na_key_rows(i, rows)
        assert (lo * GRID_W) % LANES == 0 and hi - lo <= NA_MAX_KEY_ROWS
        q = np.arange(Q_BLK) + i * Q_BLK
        qr, qc = q // GRID_W, q % GRID_W
        k = np.arange((hi - lo) * GRID_W) + lo * GRID_W
        kr, kc = k // GRID_W, k % GRID_W
        rs = np.clip(qr - NA_ROWS // 2, 0, rows - NA_ROWS)
        cs = np.clip(qc - NA_COLS // 2, 0, GRID_W - NA_COLS)
        ok_r = (kr[:, None] >= rs[None, :]) & (kr[:, None] < rs[None, :] + NA_ROWS)
        ok_c = (kc[:, None] >= cs[None, :]) & (kc[:, None] < cs[None, :] + NA_COLS)
        m = np.full((NA_MAX_KEYS, Q_BLK), NEG_INF, np.float32)
        m[:k.size] = np.where(ok_r & ok_c, 0.0, NEG_INF)
        rr = np.full((NA_Q_ROWS, NA_MAX_KEY_ROWS), -1, np.int64)
        for a in range(NA_Q_ROWS):
            for b in range(hi - lo):
                r, krow = NA_Q_ROWS * i + a, lo + b
                if rs[a * GRID_W] <= krow < rs[a * GRID_W] + NA_ROWS:
                    rr[a, b] = krow - r + NA_ROWS - 1
        if cls in seen:
            assert np.array_equal(seen[cls][0], m) and np.array_equal(seen[cls][1], rr)
        seen[cls] = (m, rr)
        masks[cls], ridx[cls] = m, rr
    return masks, ridx


def _swa_edge_masks():
    assert SWA_WINDOW == Q_BLK
    key = np.arange(Q_BLK)[:, None]
    qry = np.arange(Q_BLK)[None, :]
    tri = np.stack([np.where(key >= qry, 0.0, NEG_INF), np.where(key <= qry, 0.0, NEG_INF)]).astype(np.float32)
    return np.ascontiguousarray(np.concatenate([tri, tri], axis=2))


def _rope_tables(seq):
    half = ROPE_DIM // 2
    inv_freq = np.float32(ROPE_THETA) ** (-np.arange(0, ROPE_DIM, 2, dtype=np.float32) / np.float32(ROPE_DIM))
    ang = np.arange(seq, dtype=np.float32)[:, None] * inv_freq[None, :].astype(np.float32)
    cos, sin = np.cos(ang).astype(np.float32), np.sin(ang).astype(np.float32)
    ones = np.ones((seq, HEAD_DIM - ROPE_DIM), np.float32)
    zeros = np.zeros((seq, HEAD_DIM - ROPE_DIM), np.float32)
    zh = np.zeros((seq, half), np.float32)
    c = np.concatenate([cos, cos, ones], axis=1)
    sa = np.concatenate([-sin, zh, zeros], axis=1)
    sb = np.concatenate([zh, sin, zeros], axis=1)
    rep = LANES // HEAD_DIM
    return tuple(jnp.asarray(np.tile(t, (1, rep))) for t in (c, sa, sb))


def _rms(x, g):
    return x * lax.rsqrt(jnp.mean(x * x, axis=-1, keepdims=True) + RMS_EPS) * g


def _rope_tile(a, c, sa, sb):
    return a * c + pltpu.roll(a, LANES - ROPE_DIM // 2, axis=1) * sa + pltpu.roll(a, ROPE_DIM // 2, axis=1) * sb


def _in_proj_kernel(x_ref, g_ref, w_ref, c_ref, sa_ref, sb_ref, *rest):
    n_extra = (len(rest) - 2) // 2
    extra_in, o_ref, extra_out, w_scr = rest[:n_extra], rest[n_extra], rest[n_extra + 1:-1], rest[-1]

    @pl.when(pl.program_id(0) == 0)
    def _():
        for c0 in range(0, w_ref.shape[1], LANES):
            w_scr[:, c0:c0 + LANES] = w_ref[:, c0:c0 + LANES].astype(w_scr.dtype)

    for src, dst in zip(extra_in, extra_out):
        dst[...] = src[...].astype(dst.dtype)

    h = _rms(x_ref[...], g_ref[...]).astype(jnp.bfloat16)
    tm, seq = x_ref.shape[0], c_ref.shape[0]
    pos = pl.ds(pl.multiple_of((pl.program_id(0) % (seq // tm)) * tm, tm), tm)
    c, sa, sb = c_ref[pos, :], sa_ref[pos, :], sb_ref[pos, :]

    def proj(col, width):
        return jnp.dot(h, w_scr[:, col:col + width], preferred_element_type=jnp.float32)

    kv = proj(W_COL_KB, 2 * B_KV_WIDTH)
    left = lax.broadcasted_iota(jnp.int32, (h.shape[0], LANES), 1) < HEAD_DIM
    for col, tile in ((COL_KD, _rope_tile(kv[:, :B_KV_WIDTH], c, sa, sb)), (COL_VD, kv[:, B_KV_WIDTH:])):
        swapped = pltpu.roll(tile, HEAD_DIM, axis=1)
        o_ref[:, col:col + LANES] = jnp.where(left, tile, swapped).astype(o_ref.dtype)
        o_ref[:, col + LANES:col + 2 * LANES] = jnp.where(left, swapped, tile).astype(o_ref.dtype)
    qb = proj(COL_QB, B_WIDTH)
    for t in range(B_WIDTH // LANES):
        lo = t * LANES
        tile = _rope_tile(qb[:, lo:lo + LANES], c, sa, sb) * Q_SCALE
        o_ref[:, COL_QB + lo:COL_QB + lo + LANES] = tile.astype(o_ref.dtype)
    o_ref[:, COL_QA:COL_QA + A_WIDTH] = (proj(COL_QA, A_WIDTH) * Q_SCALE).astype(o_ref.dtype)
    o_ref[:, COL_KA:COL_KA + A_WIDTH] = proj(COL_KA, A_WIDTH).astype(o_ref.dtype)
    o_ref[:, COL_VA:COL_VA + A_WIDTH] = proj(COL_VA, A_WIDTH).astype(o_ref.dtype)


def _in_proj(x2d, g, w_in, rope, seq, later_weights):
    n_tok = x2d.shape[0]
    tm = IN_PROJ_ROWS
    steps = n_tok // tm
    assert seq % tm == 0
    const = lambda i: (0, 0)
    row = lambda i: (i, 0)
    slices = [(w.shape[0] // steps, w.shape[1]) for w in later_weights]
    assert all(w.shape[0] % steps == 0 and rows % BF16_ROWS == 0 for w, (rows, _) in zip(later_weights, slices))
    slice_bytes = sum(rows * cols for rows, cols in slices)
    vmem = (2 * tm * D_MODEL * 4 + D_MODEL * IN_WIDTH * (4 + 2) + 2 * tm * PROJ_WIDTH * 2 + 3 * tm * B_WIDTH * 4 * 2
            + 2 * slice_bytes * (4 + 2) + 3 * 2 * seq * LANES * 4)
    outs = pl.pallas_call(
        _in_proj_kernel,
        out_shape=[jax.ShapeDtypeStruct((n_tok, PROJ_WIDTH), jnp.bfloat16)]
        + [jax.ShapeDtypeStruct(w.shape, jnp.bfloat16) for w in later_weights],
        grid=(steps,),
        in_specs=[
            pl.BlockSpec((tm, D_MODEL), row),
            pl.BlockSpec((1, D_MODEL), const),
            pl.BlockSpec((D_MODEL, IN_WIDTH), const, pipeline_mode=pl.Buffered(1)),
            pl.BlockSpec((seq, LANES), const),
            pl.BlockSpec((seq, LANES), const),
            pl.BlockSpec((seq, LANES), const),
        ] + [pl.BlockSpec(blk, row) for blk in slices],
        out_specs=[pl.BlockSpec((tm, PROJ_WIDTH), row)] + [pl.BlockSpec(blk, row) for blk in slices],
        scratch_shapes=[pltpu.VMEM((D_MODEL, IN_WIDTH), jnp.bfloat16)],
        compiler_params=pltpu.CompilerParams(
            dimension_semantics=("arbitrary",), vmem_limit_bytes=_vmem_limit(vmem + (8 << 20))),
        name="in_proj",
    )(x2d, g, w_in, *rope, *later_weights)
    return outs[0], outs[1:]


def _transpose_keys(v_ref, vt_ref, ones_below=False):
    top = lax.broadcasted_iota(jnp.int32, (LANES, LANES), 0) < HEAD_DIM
    for j in range(v_ref.shape[0] // LANES):
        tile = v_ref[j * LANES:(j + 1) * LANES, :].T
        vt_ref[j] = jnp.where(top, tile, jnp.ones_like(tile)) if ones_below else tile


class _Stream(NamedTuple):
    q_ref: object
    k_ref: object
    vt_ref: object
    o_ref: object
    sink_row: object = None
    sums_below: bool = False


class _Block(NamedTuple):
    stream: _Stream
    q0: int
    k0: int
    n_keys: int
    add_bias: object


def _attend(blocks, skew):
    lane = lax.broadcasted_iota(jnp.int32, (Q_BLK, LANES), 1)
    left = lane < HEAD_DIM
    top = lax.broadcasted_iota(jnp.int32, (LANES, Q_BLK), 0) < HEAD_DIM

    def scores_of(blk):
        q2 = blk.stream.q_ref[blk.q0:blk.q0 + Q_BLK, :]
        zero = jnp.zeros_like(q2)
        qq = jnp.concatenate([jnp.where(left, q2, zero), jnp.where(left, zero, q2)], axis=0)
        st = lax.dot_general(blk.stream.k_ref[blk.k0:blk.k0 + blk.n_keys, :], qq, (((1,), (1,)), ((), ())),
                             preferred_element_type=jnp.float32)
        return blk.add_bias(st)

    def finish(blk, st):
        _, _, vt_ref, o_ref, sink_row, sums_below = blk.stream
        m = jnp.max(st, axis=0, keepdims=True)
        if sink_row is not None:
            m = jnp.maximum(m, sink_row)
        p = jnp.exp2(st - m)
        tile0, full, part = blk.k0 // LANES, blk.n_keys // LANES, blk.n_keys % LANES
        pieces = [vt_ref[tile0 + j] for j in range(full)]
        if part:
            pieces.append(vt_ref[tile0 + full][:, :part])
        ot = jnp.dot(jnp.concatenate(pieces, axis=1), p.astype(jnp.bfloat16), preferred_element_type=jnp.float32)
        den = ot[HEAD_DIM:HEAD_DIM + 1, :] if sums_below else jnp.sum(p, axis=0, keepdims=True)
        if sink_row is not None:
            den = den + jnp.exp2(sink_row - m)
        if sums_below:
            ot = ot[:HEAD_DIM, :] / den
            x = jnp.concatenate([ot[:, :Q_BLK], ot[:, Q_BLK:]], axis=0)
        else:
            ot = ot / den
            x = jnp.where(top, ot[:, :Q_BLK], ot[:, Q_BLK:])
        o_ref[blk.q0:blk.q0 + Q_BLK, :] = x.astype(o_ref.dtype).T

    pending = [scores_of(blk) for blk in blocks[:skew]]
    for c, blk in enumerate(blocks):
        st = pending.pop(0)
        if c + skew < len(blocks):
            pending.append(scores_of(blocks[c + skew]))
        finish(blk, st)


def _build_na_bias(rpb_ref, mask_ref, t_scr, hp, ridx):
    n_rrow, n_rcol = 2 * NA_ROWS - 1, 2 * NA_COLS - 1
    lane8 = lax.broadcasted_iota(jnp.int32, (SUBLANES, LANES), 1)
    second_row = lax.broadcasted_iota(jnp.int32, (GRID_W, LANES), 1) >= GRID_W
    used_rows = sorted({int(r) for r in ridx.reshape(-1) if r >= 0})
    zero = jnp.zeros((GRID_W, LANES), jnp.float32)
    for hh in range(2):
        h = 2 * hp + hh
        toeplitz = {-1: zero}
        for rho in used_rows:
            band = jnp.zeros((SUBLANES, LANES), jnp.float32)
            for d in range(n_rcol):
                band = jnp.where(lane8 == (NA_COLS - 1 - d) % LANES, rpb_ref[(h * n_rrow + rho) * n_rcol + d] * LOG2E, band)
            rows = jnp.concatenate([band] * (GRID_W // SUBLANES), axis=0)
            first = pltpu.roll(rows, 0, axis=1, stride=1, stride_axis=0)
            second = pltpu.roll(rows, GRID_W, axis=1, stride=1, stride_axis=0)
            toeplitz[rho] = jnp.where(second_row, second, first)
        for cls in range(NA_CLASSES):
            for b in range(NA_MAX_KEY_ROWS):
                blk = jnp.where(second_row, toeplitz[int(ridx[cls, 1, b])], toeplitz[int(ridx[cls, 0, b])])
                ks = slice(b * GRID_W, (b + 1) * GRID_W)
                t_scr[cls, ks, hh * Q_BLK:(hh + 1) * Q_BLK] = blk + mask_ref[cls, ks, :]


def _swa_add_mask(st, edge_ref, first_tile, n):
    parts = []
    for j in range(st.shape[0] // Q_BLK):
        part = st[j * Q_BLK:(j + 1) * Q_BLK, :]
        if first_tile + j != n:
            part = part + edge_ref[0 if first_tile + j < n else 1]
        parts.append(part)
    return jnp.concatenate(parts, axis=0)


def _attn_kernel(rpb_ref, sink_ref, qa_ref, ka_ref, va_ref, qb_ref, kd_ref, vd_ref, mask_ref, edge_ref,
                 oa_ref, ob_ref, t_scr, vta_scr, vtb_scr, *, ridx, seq):
    nblk = seq // Q_BLK
    pairs = [ATTN_PAIRS * pl.program_id(0) + j for j in range(ATTN_PAIRS)]
    lanes = [slice(j * LANES, (j + 1) * LANES) for j in range(ATTN_PAIRS)]

    @pl.when(pl.program_id(1) == 0)
    def _():
        for j, pair in enumerate(pairs):
            _build_na_bias(rpb_ref, mask_ref, t_scr.at[j], pair, ridx)

    for j in range(ATTN_PAIRS):
        _transpose_keys(va_ref.at[:, lanes[j]], vta_scr.at[j])
    _transpose_keys(vd_ref, vtb_scr, ones_below=True)
    lane = lax.broadcasted_iota(jnp.int32, (1, 2 * Q_BLK), 1)
    streams = []
    for j, pair in enumerate(pairs):
        sink_row = jnp.where(lane < Q_BLK, sink_ref[2 * pair], sink_ref[2 * pair + 1]) * LOG2E
        streams.append((_Stream(qa_ref.at[:, lanes[j]], ka_ref.at[:, lanes[j]], vta_scr.at[j], oa_ref.at[:, lanes[j]]),
                        _Stream(qb_ref.at[:, lanes[j]], kd_ref, vtb_scr, ob_ref.at[:, lanes[j]], sink_row, sums_below=True)))
    blocks = []
    for i in range(nblk):
        lo, hi = _na_key_rows(i, seq // GRID_W)
        n_keys = (hi - lo) * GRID_W
        first, last = max(i - 1, 0), min(i + 1, nblk - 1)
        add_mask = functools.partial(_swa_add_mask, edge_ref=edge_ref, first_tile=first, n=i)
        for j, (na, swa) in enumerate(streams):
            add_bias = functools.partial(lambda st, j, cls, n: st + t_scr[j, cls, :n, :],
                                         j=j, cls=_na_class_of_block(i, nblk), n=n_keys)
            blocks.append(_Block(na, i * Q_BLK, lo * GRID_W, n_keys, add_bias))
            blocks.append(_Block(swa, i * Q_BLK, first * Q_BLK, (last - first + 1) * Q_BLK, add_mask))
    _attend(blocks, ATTN_SKEW)


def _attention(proj, rpb_flat, sink, seq, batch):
    masks, ridx = _na_static_tables(seq)
    edges = _swa_edge_masks()
    n_tok = proj.shape[0]
    n_steps = A_WIDTH // (ATTN_PAIRS * LANES)
    assert B_WIDTH == A_WIDTH and B_WIDTH // (SWA_KV_HEADS * LANES) == ATTN_PAIRS, "one windowed kv head per step"
    width = ATTN_PAIRS * LANES
    col = lambda base: pl.BlockSpec((seq, width), lambda s, b: (b, base // width + s))
    kv_col = lambda base: pl.BlockSpec((seq, LANES), lambda s, b: (b, base // LANES + s))
    whole = lambda a: pl.BlockSpec(a.shape, lambda s, b: (0,) * a.ndim)
    table_bytes = ATTN_PAIRS * NA_CLASSES * NA_MAX_KEYS * 2 * Q_BLK * 4
    vmem = (2 * (6 * ATTN_PAIRS + 2) * seq * LANES * 2 + 2 * (masks.size + edges.size) * 4 + table_bytes
            + (ATTN_PAIRS + 1) * seq * LANES * 2)
    out = jax.ShapeDtypeStruct((n_tok, A_WIDTH), jnp.bfloat16)
    out_tile = pl.BlockSpec((seq, width), lambda s, b: (b, s))
    return pl.pallas_call(
        functools.partial(_attn_kernel, ridx=ridx, seq=seq),
        out_shape=[out, out],
        grid=(n_steps, batch),
        in_specs=[
            pl.BlockSpec(memory_space=pltpu.SMEM),
            pl.BlockSpec(memory_space=pltpu.SMEM),
            col(COL_QA), col(COL_KA), col(COL_VA), col(COL_QB), kv_col(COL_KD), kv_col(COL_VD),
            whole(masks), whole(edges),
        ],
        out_specs=[out_tile, out_tile],
        scratch_shapes=[pltpu.VMEM((ATTN_PAIRS, NA_CLASSES, NA_MAX_KEYS, 2 * Q_BLK), jnp.float32),
                        pltpu.VMEM((ATTN_PAIRS, seq // LANES, LANES, LANES), jnp.bfloat16),
                        pltpu.VMEM((seq // LANES, LANES, LANES), jnp.bfloat16)],
        compiler_params=pltpu.CompilerParams(
            dimension_semantics=("arbitrary", "arbitrary"), vmem_limit_bytes=_attn_vmem_limit(vmem)),
        name="attention",
    )(rpb_flat, sink, proj, proj, proj, proj, proj, proj, jnp.asarray(masks), jnp.asarray(edges))


def _out_ffn_kernel(x_ref, oa_ref, ob_ref, gna_ref, gswa_ref, wo_ref, gffn_ref, wg_ref, wu_ref, wd_ref,
                    gfin_ref, o_ref, *, ff_chunk, sub_rows):
    d_ff = wg_ref.shape[1]
    row_slices = [slice(r0, r0 + sub_rows) for r0 in range(0, x_ref.shape[0], sub_rows)]
    mixed = []
    for rows in row_slices:
        na = _rms(oa_ref[rows, :].astype(jnp.float32), gna_ref[...]).astype(jnp.bfloat16)
        nb = _rms(ob_ref[rows, :].astype(jnp.float32), gswa_ref[...]).astype(jnp.bfloat16)
        x1 = x_ref[rows, :] + jnp.dot(jnp.concatenate([na, nb], axis=1), wo_ref[...],
                                      preferred_element_type=jnp.float32)
        mixed.append((x1, _rms(x1, gffn_ref[...]).astype(jnp.bfloat16)))
    for rows, (x1, h) in zip(row_slices, mixed):
        acts = []
        for c0 in range(0, d_ff, ff_chunk):
            gate = jnp.dot(h, wg_ref[:, c0:c0 + ff_chunk], preferred_element_type=jnp.float32)
            up = jnp.dot(h, wu_ref[:, c0:c0 + ff_chunk], preferred_element_type=jnp.float32)
            acts.append((jax.nn.silu(gate) * up).astype(jnp.bfloat16))
        acc = jnp.dot(jnp.concatenate(acts, axis=1), wd_ref[...], preferred_element_type=jnp.float32)
        o_ref[rows, :] = _rms(x1 + acc, gfin_ref[...])


def _out_ffn(x2d, o_na, o_swa, g_na, g_swa, wo, g_ffn, wg, wu, wd, g_fin):
    n_tok = x2d.shape[0]
    tm, sub_rows, ff_chunk = FFN_ROWS, FFN_SUB_ROWS, FFN_CHUNK
    d_ff = wg.shape[1]
    const = lambda i: (0, 0)
    row = lambda i: (i, 0)
    resident = lambda shape: pl.BlockSpec(shape, const, pipeline_mode=pl.Buffered(1))
    weights = (D_MODEL * D_MODEL + 3 * D_MODEL * d_ff) * 2
    tiles = 2 * 2 * tm * D_MODEL * 4 + 2 * 2 * tm * A_WIDTH * 2
    vmem = weights + tiles + 2 * (6 * sub_rows * D_MODEL * 4 + 4 * sub_rows * ff_chunk * 4)
    return pl.pallas_call(
        functools.partial(_out_ffn_kernel, ff_chunk=ff_chunk, sub_rows=sub_rows),
        out_shape=jax.ShapeDtypeStruct((n_tok, D_MODEL), jnp.float32),
        grid=(n_tok // tm,),
        in_specs=[
            pl.BlockSpec((tm, D_MODEL), row),
            pl.BlockSpec((tm, A_WIDTH), row),
            pl.BlockSpec((tm, B_WIDTH), row),
            pl.BlockSpec((1, A_WIDTH), const),
            pl.BlockSpec((1, B_WIDTH), const),
            resident((D_MODEL, D_MODEL)),
            pl.BlockSpec((1, D_MODEL), const),
            resident((D_MODEL, d_ff)),
            resident((D_MODEL, d_ff)),
            resident((d_ff, D_MODEL)),
            pl.BlockSpec((1, D_MODEL), const),
        ],
        out_specs=pl.BlockSpec((tm, D_MODEL), row),
        compiler_params=pltpu.CompilerParams(
            dimension_semantics=("arbitrary",), vmem_limit_bytes=_vmem_limit(vmem + (8 << 20))),
        name="out_ffn",
    )(x2d, o_na, o_swa, g_na, g_swa, wo, g_ffn, wg, wu, wd, g_fin)


def kernel(x, g_norm_mix, w_in, na_rpb, swa_sink, g_out_na, g_out_swa, w_out, g_norm_ffn, w_gate, w_up, w_down,
           g_final):
    batch, seq, d_model = x.shape
    assert w_in.shape[0] == 1, "the final RMSNorm is fused into the single layer's FFN call"
    assert d_model == D_MODEL and seq % Q_BLK == 0 and seq // GRID_W >= NA_MAX_KEY_ROWS
    assert B_KV_WIDTH == LANES, "the windowed k (and v) heads together fill one lane tile"
    rope = _rope_tables(seq)
    x2d = x.reshape(batch * seq, d_model)
    proj, (wo, wg, wu, wd) = _in_proj(x2d, g_norm_mix[0][None, :], w_in[0], rope, seq,
                                      (w_out[0], w_gate[0], w_up[0], w_down[0]))
    o_na, o_swa = _attention(proj, na_rpb[0].reshape(-1), swa_sink[0], seq, batch)
    out = _out_ffn(x2d, o_na, o_swa, g_out_na[0][None, :], g_out_swa[0][None, :], wo,
                   g_norm_ffn[0][None, :], wg, wu, wd, g_final[None, :])
    return out.reshape(batch, seq, d_model)
```

```python
import functools
from typing import NamedTuple

import numpy as np
import jax
import jax.numpy as jnp
from jax import lax
from jax.experimental import pallas as pl
from jax.experimental.pallas import tpu as pltpu

D_MODEL = 1024
HEAD_DIM = 64
NA_HEADS = 8
SWA_HEADS = 8
SWA_KV_HEADS = 2
A_WIDTH = NA_HEADS * HEAD_DIM
B_WIDTH = SWA_HEADS * HEAD_DIM
B_KV_WIDTH = SWA_KV_HEADS * HEAD_DIM
IN_WIDTH = 3 * A_WIDTH + B_WIDTH + 2 * B_KV_WIDTH
GRID_W = 64
NA_ROWS = 8
NA_COLS = 16
SWA_WINDOW = 128
ROPE_THETA = 500000.0
ROPE_DIM = HEAD_DIM // 4
RMS_EPS = 1e-6
NEG_INF = -1e30
LOG2E = 1.4426950408889634
Q_SCALE = HEAD_DIM ** -0.5 * LOG2E

LANES = 128
SUBLANES = 8
BF16_ROWS = 16
VMEM_CAPACITY = 64 * 1024 * 1024

PROJ_WIDTH = 3 * A_WIDTH + B_WIDTH + 2 * SWA_KV_HEADS * LANES
COL_QA, COL_KA, COL_VA = 0, A_WIDTH, 2 * A_WIDTH
COL_QB = 3 * A_WIDTH
COL_KD = COL_QB + B_WIDTH
COL_VD = COL_KD + SWA_KV_HEADS * LANES
W_COL_KB = COL_QB + B_WIDTH

Q_BLK = 128
NA_Q_ROWS = Q_BLK // GRID_W
NA_MAX_KEY_ROWS = NA_ROWS + NA_Q_ROWS - 1
NA_MAX_KEYS = NA_MAX_KEY_ROWS * GRID_W
NA_CLASSES = 5
ATTN_SKEW = 6
ATTN_PAIRS = 2
IN_PROJ_ROWS = 1024
FFN_ROWS = 1024
FFN_SUB_ROWS = 512
FFN_CHUNK = 256


def _vmem_limit(nbytes):
    return int(min(VMEM_CAPACITY - (2 << 20), max(nbytes, 16 << 20)))


def _attn_vmem_limit(nbytes):
    assert nbytes + (16 << 20) <= VMEM_CAPACITY - (2 << 20)
    return VMEM_CAPACITY - (2 << 20)


def _na_class_of_block(i, nblk):
    if i < 2:
        return i
    if i >= nblk - 2:
        return NA_CLASSES - (nblk - i)
    return 2


def _na_key_rows(i, rows):
    first = NA_Q_ROWS * i
    lo = int(np.clip(first - NA_ROWS // 2, 0, rows - NA_ROWS))
    hi = int(np.clip(first + NA_Q_ROWS - 1 - NA_ROWS // 2, 0, rows - NA_ROWS)) + NA_ROWS
    return lo, hi


def _na_static_tables(seq):
    rows = seq // GRID_W
    nblk = seq // Q_BLK
    masks = np.full((NA_CLASSES, NA_MAX_KEYS, Q_BLK), NEG_INF, np.float32)
    ridx = np.full((NA_CLASSES, NA_Q_ROWS, NA_MAX_KEY_ROWS), -1, np.int64)
    seen = {}
    for i in range(nblk):
        cls = _na_class_of_block(i, nblk)
        lo, hi = _na_key_rows(i, rows)
        assert (lo * GRID_W) % LANES == 0 and hi - lo <= NA_MAX_KEY_ROWS
        q = np.arange(Q_BLK) + i * Q_BLK
        qr, qc = q // GRID_W, q % GRID_W
        k = np.arange((hi - lo) * GRID_W) + lo * GRID_W
        kr, kc = k // GRID_W, k % GRID_W
        rs = np.clip(qr - NA_ROWS // 2, 0, rows - NA_ROWS)
        cs = np.clip(qc - NA_COLS // 2, 0, GRID_W - NA_COLS)
        ok_r = (kr[:, None] >= rs[None, :]) & (kr[:, None] < rs[None, :] + NA_ROWS)
        ok_c = (kc[:, None] >= cs[None, :]) & (kc[:, None] < cs[None, :] + NA_COLS)
        m = np.full((NA_MAX_KEYS, Q_BLK), NEG_INF, np.float32)
        m[:k.size] = np.where(ok_r & ok_c, 0.0, NEG_INF)
        rr = np.full((NA_Q_ROWS, NA_MAX_KEY_ROWS), -1, np.int64)
        for a in range(NA_Q_ROWS):
            for b in range(hi - lo):
                r, krow = NA_Q_ROWS * i + a, lo + b
                if rs[a * GRID_W] <= krow < rs[a * GRID_W] + NA_ROWS:
                    rr[a, b] = krow - r + NA_ROWS - 1
        if cls in seen:
            assert np.array_equal(seen[cls][0], m) and np.array_equal(seen[cls][1], rr)
        seen[cls] = (m, rr)
        masks[cls], ridx[cls] = m, rr
    return masks, ridx


def _swa_edge_masks():
    assert SWA_WINDOW == Q_BLK
    key = np.arange(Q_BLK)[:, None]
    qry = np.arange(Q_BLK)[None, :]
    tri = np.stack([np.where(key >= qry, 0.0, NEG_INF), np.where(key <= qry, 0.0, NEG_INF)]).astype(np.float32)
    return np.ascontiguousarray(np.concatenate([tri, tri], axis=2))


def _rope_tables(seq):
    half = ROPE_DIM // 2
    inv_freq = np.float32(ROPE_THETA) ** (-np.arange(0, ROPE_DIM, 2, dtype=np.float32) / np.float32(ROPE_DIM))
    ang = np.arange(seq, dtype=np.float32)[:, None] * inv_freq[None, :].astype(np.float32)
    cos, sin = np.cos(ang).astype(np.float32), np.sin(ang).astype(np.float32)
    ones = np.ones((seq, HEAD_DIM - ROPE_DIM), np.float32)
    zeros = np.zeros((seq, HEAD_DIM - ROPE_DIM), np.float32)
    zh = np.zeros((seq, half), np.float32)
    c = np.concatenate([cos, cos, ones], axis=1)
    sa = np.concatenate([-sin, zh, zeros], axis=1)
    sb = np.concatenate([zh, sin, zeros], axis=1)
    rep = LANES // HEAD_DIM
    return tuple(jnp.asarray(np.tile(t, (1, rep))) for t in (c, sa, sb))


def _rms(x, g):
    return x * lax.rsqrt(jnp.mean(x * x, axis=-1, keepdims=True) + RMS_EPS) * g


def _rope_tile(a, c, sa, sb):
    return a * c + pltpu.roll(a, LANES - ROPE_DIM // 2, axis=1) * sa + pltpu.roll(a, ROPE_DIM // 2, axis=1) * sb


def _in_proj_kernel(x_ref, g_ref, w_ref, c_ref, sa_ref, sb_ref, *rest):
    n_extra = (len(rest) - 2) // 2
    extra_in, o_ref, extra_out, w_scr = rest[:n_extra], rest[n_extra], rest[n_extra + 1:-1], rest[-1]

    @pl.when(pl.program_id(0) == 0)
    def _():
        for c0 in range(0, w_ref.shape[1], LANES):
            w_scr[:, c0:c0 + LANES] = w_ref[:, c0:c0 + LANES].astype(w_scr.dtype)

    for src, dst in zip(extra_in, extra_out):
        dst[...] = src[...].astype(dst.dtype)

    h = _rms(x_ref[...], g_ref[...]).astype(jnp.bfloat16)
    tm, seq = x_ref.shape[0], c_ref.shape[0]
    pos = pl.ds(pl.multiple_of((pl.program_id(0) % (seq // tm)) * tm, tm), tm)
    c, sa, sb = c_ref[pos, :], sa_ref[pos, :], sb_ref[pos, :]

    def proj(col, width):
        return jnp.dot(h, w_scr[:, col:col + width], preferred_element_type=jnp.float32)

    kv = proj(W_COL_KB, 2 * B_KV_WIDTH)
    left = lax.broadcasted_iota(jnp.int32, (h.shape[0], LANES), 1) < HEAD_DIM
    for col, tile in ((COL_KD, _rope_tile(kv[:, :B_KV_WIDTH], c, sa, sb)), (COL_VD, kv[:, B_KV_WIDTH:])):
        swapped = pltpu.roll(tile, HEAD_DIM, axis=1)
        o_ref[:, col:col + LANES] = jnp.where(left, tile, swapped).astype(o_ref.dtype)
        o_ref[:, col + LANES:col + 2 * LANES] = jnp.where(left, swapped, tile).astype(o_ref.dtype)
    qb = proj(COL_QB, B_WIDTH)
    for t in range(B_WIDTH // LANES):
        lo = t * LANES
        tile = _rope_tile(qb[:, lo:lo + LANES], c, sa, sb) * Q_SCALE
        o_ref[:, COL_QB + lo:COL_QB + lo + LANES] = tile.astype(o_ref.dtype)
    o_ref[:, COL_QA:COL_QA + A_WIDTH] = (proj(COL_QA, A_WIDTH) * Q_SCALE).astype(o_ref.dtype)
    o_ref[:, COL_KA:COL_KA + A_WIDTH] = proj(COL_KA, A_WIDTH).astype(o_ref.dtype)
    o_ref[:, COL_VA:COL_VA + A_WIDTH] = proj(COL_VA, A_WIDTH).astype(o_ref.dtype)


def _in_proj(x2d, g, w_in, rope, seq, later_weights):
    n_tok = x2d.shape[0]
    tm = IN_PROJ_ROWS
    steps = n_tok // tm
    assert seq % tm == 0
    const = lambda i: (0, 0)
    row = lambda i: (i, 0)
    slices = [(w.shape[0] // steps, w.shape[1]) for w in later_weights]
    assert all(w.shape[0] % steps == 0 and rows % BF16_ROWS == 0 for w, (rows, _) in zip(later_weights, slices))
    slice_bytes = sum(rows * cols for rows, cols in slices)
    vmem = (2 * tm * D_MODEL * 4 + D_MODEL * IN_WIDTH * (4 + 2) + 2 * tm * PROJ_WIDTH * 2 + 3 * tm * B_WIDTH * 4 * 2
            + 2 * slice_bytes * (4 + 2) + 3 * 2 * seq * LANES * 4)
    outs = pl.pallas_call(
        _in_proj_kernel,
        out_shape=[jax.ShapeDtypeStruct((n_tok, PROJ_WIDTH), jnp.bfloat16)]
        + [jax.ShapeDtypeStruct(w.shape, jnp.bfloat16) for w in later_weights],
        grid=(steps,),
        in_specs=[
            pl.BlockSpec((tm, D_MODEL), row),
            pl.BlockSpec((1, D_MODEL), const),
            pl.BlockSpec((D_MODEL, IN_WIDTH), const, pipeline_mode=pl.Buffered(1)),
            pl.BlockSpec((seq, LANES), const),
            pl.BlockSpec((seq, LANES), const),
            pl.BlockSpec((seq, LANES), const),
        ] + [pl.BlockSpec(blk, row) for blk in slices],
        out_specs=[pl.BlockSpec((tm, PROJ_WIDTH), row)] + [pl.BlockSpec(blk, row) for blk in slices],
        scratch_shapes=[pltpu.VMEM((D_MODEL, IN_WIDTH), jnp.bfloat16)],
        compiler_params=pltpu.CompilerParams(
            dimension_semantics=("arbitrary",), vmem_limit_bytes=_vmem_limit(vmem + (8 << 20))),
        name="in_proj",
    )(x2d, g, w_in, *rope, *later_weights)
    return outs[0], outs[1:]


def _transpose_keys(v_ref, vt_ref, ones_below=False):
    top = lax.broadcasted_iota(jnp.int32, (LANES, LANES), 0) < HEAD_DIM
    for j in range(v_ref.shape[0] // LANES):
        tile = v_ref[j * LANES:(j + 1) * LANES, :].T
        vt_ref[j] = jnp.where(top, tile, jnp.ones_like(tile)) if ones_below else tile


class _Stream(NamedTuple):
    q_ref: object
    k_ref: object
    vt_ref: object
    o_ref: object
    sink_row: object = None
    sums_below: bool = False


class _Block(NamedTuple):
    stream: _Stream
    q0: int
    k0: int
    n_keys: int
    add_bias: object


def _attend(blocks, skew):
    lane = lax.broadcasted_iota(jnp.int32, (Q_BLK, LANES), 1)
    left = lane < HEAD_DIM
    top = lax.broadcasted_iota(jnp.int32, (LANES, Q_BLK), 0) < HEAD_DIM

    def scores_of(blk):
        q2 = blk.stream.q_ref[blk.q0:blk.q0 + Q_BLK, :]
        zero = jnp.zeros_like(q2)
        qq = jnp.concatenate([jnp.where(left, q2, zero), jnp.where(left, zero, q2)], axis=0)
        st = lax.dot_general(blk.stream.k_ref[blk.k0:blk.k0 + blk.n_keys, :], qq, (((1,), (1,)), ((), ())),
                             preferred_element_type=jnp.float32)
        return blk.add_bias(st)

    def finish(blk, st):
        _, _, vt_ref, o_ref, sink_row, sums_below = blk.stream
        m = jnp.max(st, axis=0, keepdims=True)
        if sink_row is not None:
            m = jnp.maximum(m, sink_row)
        p = jnp.exp2(st - m)
        tile0, full, part = blk.k0 // LANES, blk.n_keys // LANES, blk.n_keys % LANES
        pieces = [vt_ref[tile0 + j] for j in range(full)]
        if part:
            pieces.append(vt_ref[tile0 + full][:, :part])
        ot = jnp.dot(jnp.concatenate(pieces, axis=1), p.astype(jnp.bfloat16), preferred_element_type=jnp.float32)
        den = ot[HEAD_DIM:HEAD_DIM + 1, :] if sums_below else jnp.sum(p, axis=0, keepdims=True)
        if sink_row is not None:
            den = den + jnp.exp2(sink_row - m)
        if sums_below:
            ot = ot[:HEAD_DIM, :] / den
            x = jnp.concatenate([ot[:, :Q_BLK], ot[:, Q_BLK:]], axis=0)
        else:
            ot = ot / den
            x = jnp.where(top, ot[:, :Q_BLK], ot[:, Q_BLK:])
        o_ref[blk.q0:blk.q0 + Q_BLK, :] = x.astype(o_ref.dtype).T

    pending = [scores_of(blk) for blk in blocks[:skew]]
    for c, blk in enumerate(blocks):
        st = pending.pop(0)
        if c + skew < len(blocks):
            pending.append(scores_of(blocks[c + skew]))
        finish(blk, st)


def _build_na_bias(rpb_ref, mask_ref, t_scr, hp, ridx):
    n_rrow, n_rcol = 2 * NA_ROWS - 1, 2 * NA_COLS - 1
    lane8 = lax.broadcasted_iota(jnp.int32, (SUBLANES, LANES), 1)
    second_row = lax.broadcasted_iota(jnp.int32, (GRID_W, LANES), 1) >= GRID_W
    used_rows = sorted({int(r) for r in ridx.reshape(-1) if r >= 0})
    zero = jnp.zeros((GRID_W, LANES), jnp.float32)
    for hh in range(2):
        h = 2 * hp + hh
        toeplitz = {-1: zero}
        for rho in used_rows:
            band = jnp.zeros((SUBLANES, LANES), jnp.float32)
            for d in range(n_rcol):
                band = jnp.where(lane8 == (NA_COLS - 1 - d) % LANES, rpb_ref[(h * n_rrow + rho) * n_rcol + d] * LOG2E, band)
            rows = jnp.concatenate([band] * (GRID_W // SUBLANES), axis=0)
            first = pltpu.roll(rows, 0, axis=1, stride=1, stride_axis=0)
            second = pltpu.roll(rows, GRID_W, axis=1, stride=1, stride_axis=0)
            toeplitz[rho] = jnp.where(second_row, second, first)
        for cls in range(NA_CLASSES):
            for b in range(NA_MAX_KEY_ROWS):
                blk = jnp.where(second_row, toeplitz[int(ridx[cls, 1, b])], toeplitz[int(ridx[cls, 0, b])])
                ks = slice(b * GRID_W, (b + 1) * GRID_W)
                t_scr[cls, ks, hh * Q_BLK:(hh + 1) * Q_BLK] = blk + mask_ref[cls, ks, :]


def _swa_add_mask(st, edge_ref, first_tile, n):
    parts = []
    for j in range(st.shape[0] // Q_BLK):
        part = st[j * Q_BLK:(j + 1) * Q_BLK, :]
        if first_tile + j != n:
            part = part + edge_ref[0 if first_tile + j < n else 1]
        parts.append(part)
    return jnp.concatenate(parts, axis=0)


def _attn_kernel(rpb_ref, sink_ref, qa_ref, ka_ref, va_ref, qb_ref, kd_ref, vd_ref, mask_ref, edge_ref,
                 *rest, ridx, seq):
    n_extra = (len(rest) - 5) // 2
    extra_in, (oa_ref, ob_ref), extra_out = rest[:n_extra], rest[n_extra:n_extra + 2], rest[n_extra + 2:-3]
    t_scr, vta_scr, vtb_scr = rest[-3:]
    nblk = seq // Q_BLK
    for src, dst in zip(extra_in, extra_out):
        dst[...] = src[...].astype(dst.dtype)
    pairs = [ATTN_PAIRS * pl.program_id(0) + j for j in range(ATTN_PAIRS)]
    lanes = [slice(j * LANES, (j + 1) * LANES) for j in range(ATTN_PAIRS)]

    @pl.when(pl.program_id(1) == 0)
    def _():
        for j, pair in enumerate(pairs):
            _build_na_bias(rpb_ref, mask_ref, t_scr.at[j], pair, ridx)

    for j in range(ATTN_PAIRS):
        _transpose_keys(va_ref.at[:, lanes[j]], vta_scr.at[j])
    _transpose_keys(vd_ref, vtb_scr, ones_below=True)
    lane = lax.broadcasted_iota(jnp.int32, (1, 2 * Q_BLK), 1)
    streams = []
    for j, pair in enumerate(pairs):
        sink_row = jnp.where(lane < Q_BLK, sink_ref[2 * pair], sink_ref[2 * pair + 1]) * LOG2E
        streams.append((_Stream(qa_ref.at[:, lanes[j]], ka_ref.at[:, lanes[j]], vta_scr.at[j], oa_ref.at[:, lanes[j]]),
                        _Stream(qb_ref.at[:, lanes[j]], kd_ref, vtb_scr, ob_ref.at[:, lanes[j]], sink_row, sums_below=True)))
    blocks = []
    for i in range(nblk):
        lo, hi = _na_key_rows(i, seq // GRID_W)
        n_keys = (hi - lo) * GRID_W
        first, last = max(i - 1, 0), min(i + 1, nblk - 1)
        add_mask = functools.partial(_swa_add_mask, edge_ref=edge_ref, first_tile=first, n=i)
        for j, (na, swa) in enumerate(streams):
            add_bias = functools.partial(lambda st, j, cls, n: st + t_scr[j, cls, :n, :],
                                         j=j, cls=_na_class_of_block(i, nblk), n=n_keys)
            blocks.append(_Block(na, i * Q_BLK, lo * GRID_W, n_keys, add_bias))
            blocks.append(_Block(swa, i * Q_BLK, first * Q_BLK, (last - first + 1) * Q_BLK, add_mask))
    _attend(blocks, ATTN_SKEW)


def _attention(proj, rpb_flat, sink, seq, batch, later_weights):
    masks, ridx = _na_static_tables(seq)
    edges = _swa_edge_masks()
    n_tok = proj.shape[0]
    n_steps = A_WIDTH // (ATTN_PAIRS * LANES)
    assert B_WIDTH == A_WIDTH and B_WIDTH // (SWA_KV_HEADS * LANES) == ATTN_PAIRS, "one windowed kv head per step"
    width = ATTN_PAIRS * LANES
    col = lambda base: pl.BlockSpec((seq, width), lambda s, b: (b, base // width + s))
    kv_col = lambda base: pl.BlockSpec((seq, LANES), lambda s, b: (b, base // LANES + s))
    whole = lambda a: pl.BlockSpec(a.shape, lambda s, b: (0,) * a.ndim)
    table_bytes = ATTN_PAIRS * NA_CLASSES * NA_MAX_KEYS * 2 * Q_BLK * 4
    vmem = (2 * (6 * ATTN_PAIRS + 2) * seq * LANES * 2 + 2 * (masks.size + edges.size) * 4 + table_bytes
            + (ATTN_PAIRS + 1) * seq * LANES * 2)
    out = jax.ShapeDtypeStruct((n_tok, A_WIDTH), jnp.bfloat16)
    out_tile = pl.BlockSpec((seq, width), lambda s, b: (b, s))
    steps = n_steps * batch
    slices = [(w.shape[0] // steps, w.shape[1]) for w in later_weights]
    assert all(w.shape[0] % steps == 0 and rows % BF16_ROWS == 0 for w, (rows, _) in zip(later_weights, slices))
    w_slices = [pl.BlockSpec(blk, lambda s, b: (s * batch + b, 0)) for blk in slices]
    outs = pl.pallas_call(
        functools.partial(_attn_kernel, ridx=ridx, seq=seq),
        out_shape=[out, out] + [jax.ShapeDtypeStruct(w.shape, jnp.bfloat16) for w in later_weights],
        grid=(n_steps, batch),
        in_specs=[
            pl.BlockSpec(memory_space=pltpu.SMEM),
            pl.BlockSpec(memory_space=pltpu.SMEM),
            col(COL_QA), col(COL_KA), col(COL_VA), col(COL_QB), kv_col(COL_KD), kv_col(COL_VD),
            whole(masks), whole(edges),
        ] + w_slices,
        out_specs=[out_tile, out_tile] + w_slices,
        scratch_shapes=[pltpu.VMEM((ATTN_PAIRS, NA_CLASSES, NA_MAX_KEYS, 2 * Q_BLK), jnp.float32),
                        pltpu.VMEM((ATTN_PAIRS, seq // LANES, LANES, LANES), jnp.bfloat16),
                        pltpu.VMEM((seq // LANES, LANES, LANES), jnp.bfloat16)],
        compiler_params=pltpu.CompilerParams(
            dimension_semantics=("arbitrary", "arbitrary"), vmem_limit_bytes=_attn_vmem_limit(vmem)),
        name="attention",
    )(rpb_flat, sink, proj, proj, proj, proj, proj, proj, jnp.asarray(masks), jnp.asarray(edges), *later_weights)
    return outs[0], outs[1], outs[2:]


def _out_ffn_kernel(x_ref, oa_ref, ob_ref, gna_ref, gswa_ref, wo_ref, gffn_ref, wg_ref, wu_ref, wd_ref,
                    gfin_ref, o_ref, *, ff_chunk, sub_rows):
    d_ff = wg_ref.shape[1]
    row_slices = [slice(r0, r0 + sub_rows) for r0 in range(0, x_ref.shape[0], sub_rows)]
    mixed = []
    for rows in row_slices:
        na = _rms(oa_ref[rows, :].astype(jnp.float32), gna_ref[...]).astype(jnp.bfloat16)
        nb = _rms(ob_ref[rows, :].astype(jnp.float32), gswa_ref[...]).astype(jnp.bfloat16)
        x1 = x_ref[rows, :] + jnp.dot(jnp.concatenate([na, nb], axis=1), wo_ref[...],
                                      preferred_element_type=jnp.float32)
        mixed.append((x1, _rms(x1, gffn_ref[...]).astype(jnp.bfloat16)))
    for rows, (x1, h) in zip(row_slices, mixed):
        acts = []
        for c0 in range(0, d_ff, ff_chunk):
            gate = jnp.dot(h, wg_ref[:, c0:c0 + ff_chunk], preferred_element_type=jnp.float32)
            up = jnp.dot(h, wu_ref[:, c0:c0 + ff_chunk], preferred_element_type=jnp.float32)
            acts.append((jax.nn.silu(gate) * up).astype(jnp.bfloat16))
        acc = jnp.dot(jnp.concatenate(acts, axis=1), wd_ref[...], preferred_element_type=jnp.float32)
        o_ref[rows, :] = _rms(x1 + acc, gfin_ref[...])


def _out_ffn(x2d, o_na, o_swa, g_na, g_swa, wo, g_ffn, wg, wu, wd, g_fin):
    n_tok = x2d.shape[0]
    tm, sub_rows, ff_chunk = FFN_ROWS, FFN_SUB_ROWS, FFN_CHUNK
    d_ff = wg.shape[1]
    const = lambda i: (0, 0)
    row = lambda i: (i, 0)
    resident = lambda shape: pl.BlockSpec(shape, const, pipeline_mode=pl.Buffered(1))
    weights = (D_MODEL * D_MODEL + 3 * D_MODEL * d_ff) * 2
    tiles = 2 * 2 * tm * D_MODEL * 4 + 2 * 2 * tm * A_WIDTH * 2
    vmem = weights + tiles + 2 * (6 * sub_rows * D_MODEL * 4 + 4 * sub_rows * ff_chunk * 4)
    return pl.pallas_call(
        functools.partial(_out_ffn_kernel, ff_chunk=ff_chunk, sub_rows=sub_rows),
        out_shape=jax.ShapeDtypeStruct((n_tok, D_MODEL), jnp.float32),
        grid=(n_tok // tm,),
        in_specs=[
            pl.BlockSpec((tm, D_MODEL), row),
            pl.BlockSpec((tm, A_WIDTH), row),
            pl.BlockSpec((tm, B_WIDTH), row),
            pl.BlockSpec((1, A_WIDTH), const),
            pl.BlockSpec((1, B_WIDTH), const),
            resident((D_MODEL, D_MODEL)),
            pl.BlockSpec((1, D_MODEL), const),
            resident((D_MODEL, d_ff)),
            resident((D_MODEL, d_ff)),
            resident((d_ff, D_MODEL)),
            pl.BlockSpec((1, D_MODEL), const),
        ],
        out_specs=pl.BlockSpec((tm, D_MODEL), row),
        compiler_params=pltpu.CompilerParams(
            dimension_semantics=("arbitrary",), vmem_limit_bytes=_vmem_limit(vmem + (8 << 20))),
        name="out_ffn",
    )(x2d, o_na, o_swa, g_na, g_swa, wo, g_ffn, wg, wu, wd, g_fin)


def kernel(x, g_norm_mix, w_in, na_rpb, swa_sink, g_out_na, g_out_swa, w_out, g_norm_ffn, w_gate, w_up, w_down,
           g_final):
    batch, seq, d_model = x.shape
    assert w_in.shape[0] == 1, "the final RMSNorm is fused into the single layer's FFN call"
    assert d_model == D_MODEL and seq % Q_BLK == 0 and seq // GRID_W >= NA_MAX_KEY_ROWS
    assert B_KV_WIDTH == LANES, "the windowed k (and v) heads together fill one lane tile"
    rope = _rope_tables(seq)
    x2d = x.reshape(batch * seq, d_model)
    proj, _ = _in_proj(x2d, g_norm_mix[0][None, :], w_in[0], rope, seq, ())
    o_na, o_swa, (wo, wg, wu, wd) = _attention(proj, na_rpb[0].reshape(-1), swa_sink[0], seq, batch,
                                               (w_out[0], w_gate[0], w_up[0], w_down[0]))
    out = _out_ffn(x2d, o_na, o_swa, g_out_na[0][None, :], g_out_swa[0][None, :], wo,
                   g_norm_ffn[0][None, :], wg, wu, wd, g_final[None, :])
    return out.reshape(batch, seq, d_model)
```
